```python
import math
import jax
import jax.numpy as jnp
from jax import lax
import numpy as np

D_MODEL = 2048
BATCH = 16
SEQ = 256
DEPTH = 2
DEC_BATCH = 4
DEC_SEQ = 4096
PAST_LEN = 256

EPS = 1e-6
GRID_W = 64
NA_HEAD_DIM = 64
NA_W = D_MODEL // 2
NA_HEADS = NA_W // NA_HEAD_DIM
NA_KH = 8
NA_KW = 16
ATTN_BLOCK = 128
SSM_HEADDIM = 64
SSM_W = D_MODEL // 2
SSM_HEADS = SSM_W // SSM_HEADDIM
SSM_GROUPS = 4
SSM_STATE = 128
SSM_CONV = 3
SSM_CHUNK = 128
CONV_CH = SSM_W + 2 * SSM_GROUPS * SSM_STATE
SGU_W = D_MODEL // 2
SGU_GROUPS = 8
SGU_CHUNK = 128
N_IN = 4 * NA_W + CONV_CH + SSM_W + 2 * SSM_HEADS + 3 * SGU_W + 3 * D_MODEL

kernel_name = 'hybrid_na_ssd_sgu_diffusion_step'


def _in_splits():
    sizes = (NA_W, NA_W, NA_W, NA_W, CONV_CH, SSM_W, 2 * SSM_HEADS, SGU_W, SGU_W, SGU_W)
    return tuple(np.cumsum(sizes).tolist())


def _rms(x, g):
    xf = x.astype(jnp.float32)
    xf = xf * lax.rsqrt(jnp.mean(jnp.square(xf), -1, keepdims=True) + EPS)
    return (xf * g.astype(jnp.float32)).astype(x.dtype)


def _attn_context(q, k, v):
    b_, L, H, d = q.shape
    nb = L // ATTN_BLOCK
    qb = jnp.moveaxis(q.reshape(b_, nb, ATTN_BLOCK, H, d), 1, 0)
    scale = d ** -0.5

    def blk(q_i):
        s = jnp.einsum('bqhd,bkhd->bhqk', q_i, k).astype(jnp.float32) * scale
        p = jax.nn.softmax(s, -1).astype(v.dtype)
        return jnp.einsum('bhqk,bkhd->bqhd', p, v)

    o = lax.map(blk, qb)
    return jnp.moveaxis(o, 0, 1).reshape(b_, L, H * d)


def _neighbourhood_attn(q, k, v, k_ctx, v_ctx, rpb):
    b_, L, H, d = q.shape
    rows = L // GRID_W
    kh = min(NA_KH, rows)
    kw = NA_KW
    n_loc = kh * kw
    scale = d ** -0.5
    kg = k.reshape(b_, rows, GRID_W, H, d)
    vg = v.reshape(b_, rows, GRID_W, H, d)
    q_rows = jnp.moveaxis(q.reshape(b_, rows, GRID_W, H, d), 1, 0)
    cols = jnp.arange(GRID_W)
    col_start = jnp.clip(cols - kw // 2, 0, GRID_W - kw)
    col_idx = col_start[:, None] + jnp.arange(kw)[None, :]
    col_bias_idx = col_idx - cols[:, None] + (NA_KW - 1)
    rpb_f = rpb.astype(jnp.float32)

    def row_block(args):
        r, q_r = args
        rs = jnp.clip(r - kh // 2, 0, rows - kh)
        k_rows = lax.dynamic_slice_in_dim(kg, rs, kh, axis=1)
        v_rows = lax.dynamic_slice_in_dim(vg, rs, kh, axis=1)
        k_win = k_rows[:, :, col_idx]
        v_win = v_rows[:, :, col_idx]
        s_loc = jnp.einsum('bjhd,bajkhd->bhjak', q_r, k_win).astype(jnp.float32) * scale
        row_bias_idx = rs + jnp.arange(kh) - r + (NA_KH - 1)
        bias = rpb_f[:, row_bias_idx[:, None, None], col_bias_idx[None]]
        s_loc = s_loc + jnp.transpose(bias, (0, 2, 1, 3))[None]
        s_ctx = jnp.einsum('bjhd,bkhd->bhjk', q_r, k_ctx).astype(jnp.float32) * scale
        s = jnp.concatenate([s_loc.reshape(b_, H, GRID_W, n_loc), s_ctx], -1)
        p = jax.nn.softmax(s, -1).astype(v.dtype)
        p_loc = p[..., :n_loc].reshape(b_, H, GRID_W, kh, kw)
        o = jnp.einsum('bhjak,bajkhd->bjhd', p_loc, v_win)
        return o + jnp.einsum('bhjk,bkhd->bjhd', p[..., n_loc:], v_ctx)

    out = lax.map(row_block, (jnp.arange(rows), q_rows))
    return jnp.moveaxis(out, 0, 1).reshape(b_, L, H * d)


def _dwconv(x, w, b):
    kk = w.shape[-1]
    y = lax.conv_general_dilated(
        x, jnp.transpose(w)[:, None, :].astype(x.dtype), window_strides=(1,),
        padding=[(kk // 2, kk // 2)], dimension_numbers=('NWC', 'WIO', 'NWC'),
        feature_group_count=x.shape[-1])
    return y + b.astype(x.dtype)


def _ssd(x, dt, a, bm, cm, h0):
    b_, L, H, P = x.shape
    G, N = bm.shape[-2:]
    hg = H // G
    nc = L // SSM_CHUNK
    Q = SSM_CHUNK
    x = x.reshape(b_, nc, Q, G, hg, P)
    dt = dt.reshape(b_, nc, Q, G, hg)
    bm = bm.reshape(b_, nc, Q, G, N)
    cm = cm.reshape(b_, nc, Q, G, N)
    acum = jnp.cumsum(dt * a.reshape(G, hg), axis=2)
    seg = acum[:, :, :, None] - acum[:, :, None]
    lower = jnp.tril(jnp.ones((Q, Q), dtype=bool))[:, :, None, None]
    lmat = jnp.exp(jnp.where(lower, seg, -jnp.inf))
    xdt = x * dt[..., None]
    cb = jnp.einsum('bcign,bcjgn->bcijg', cm, bm)
    y_diag = jnp.einsum('bcijg,bcijgh,bcjghp->bcighp', cb, lmat, xdt)
    decay_end = jnp.exp(acum[:, :, -1:] - acum)
    states = jnp.einsum('bcjgn,bcjgh,bcjghp->bcghpn', bm, decay_end, xdt)
    chunk_decay = jnp.exp(acum[:, :, -1])

    def step(h, inp):
        s_c, d_c = inp
        return h * d_c[..., None, None] + s_c, h

    h_init = h0.astype(jnp.float32).reshape(b_, G, hg, P, N)
    h_last, h_prev = lax.scan(step, h_init, (jnp.moveaxis(states, 1, 0), jnp.moveaxis(chunk_decay, 1, 0)))
    h_prev = jnp.moveaxis(h_prev, 0, 1)
    y_off = jnp.einsum('bcign,bcghpn,bcigh->bcighp', cm, h_prev, jnp.exp(acum))
    y = (y_diag + y_off).reshape(b_, L, H, P)
    return y, h_last.reshape(b_, H, P, N)


def _ssm_branch(xbc, z, dt_raw, p, h0_f, h0_b):
    b_, L, _ = xbc.shape
    f32 = jnp.float32
    xbc = jax.nn.silu(_dwconv(xbc, p['conv_w'], p['conv_b']).astype(f32))
    xs, bm, cm = jnp.split(xbc, [SSM_W, SSM_W + SSM_GROUPS * SSM_STATE], -1)
    xs = xs.reshape(b_, L, SSM_HEADS, SSM_HEADDIM)
    bm = bm.reshape(b_, L, SSM_GROUPS, SSM_STATE)
    cm = cm.reshape(b_, L, SSM_GROUPS, SSM_STATE)
    dt = jax.nn.softplus(dt_raw.astype(f32).reshape(b_, L, 2, SSM_HEADS) + p['dt_bias'].astype(f32))
    a = -jnp.exp(p['a_log'].astype(f32))
    y_f, h_f = _ssd(xs, dt[:, :, 0], a[0], bm, cm, h0_f)
    y_b, h_b = _ssd(jnp.flip(xs, 1), jnp.flip(dt[:, :, 1], 1), a[1], jnp.flip(bm, 1), jnp.flip(cm, 1), h0_b)
    y = y_f + jnp.flip(y_b, 1) + xs * p['d_skip'].astype(f32)[:, None]
    y = y.reshape(b_, L, SSM_W) * jax.nn.silu(z.astype(f32))
    return _rms(y, p['g_ssm']).astype(z.dtype), h_f.astype(h0_f.dtype), h_b.astype(h0_b.dtype)


def _sgu_branch(u, v, gate, p):
    b_, L, _ = v.shape
    nc = L // SGU_CHUNK
    vf = v.astype(jnp.float32)
    mu = jnp.mean(vf, -1, keepdims=True)
    var = jnp.mean(jnp.square(vf - mu), -1, keepdims=True)
    vn = (vf - mu) * lax.rsqrt(var + EPS) * p['g_sgu'].astype(jnp.float32)
    vn = vn.reshape(b_, nc, SGU_CHUNK, SGU_GROUPS, SGU_W // SGU_GROUPS)
    vs = jnp.einsum('gpq,bcqge->bcpge', p['w_s'].astype(jnp.float32), vn)
    vs = vs + jnp.transpose(p['b_s'].astype(jnp.float32))[:, :, None]
    y = u.astype(jnp.float32) * vs.reshape(b_, L, SGU_W) * jax.nn.silu(gate.astype(jnp.float32))
    return y.astype(u.dtype)


def _layer(x, cvec, p, ctx):
    b_, L, _ = x.shape
    m = jnp.matmul(jax.nn.silu(cvec), p['w_mod']) + p['b_mod']
    shift, scale, gate = jnp.split(m[:, None, :], 3, -1)
    h = (_rms(x, p['g_pre']) * (1 + scale) + shift).astype(x.dtype)
    proj = jnp.matmul(h, p['w_in'])
    q, k, v, g_a, xbc, z, dt_raw, u, v_c, g_c, g_m = jnp.split(proj, _in_splits(), -1)
    q = q.reshape(b_, L, NA_HEADS, NA_HEAD_DIM)
    k = k.reshape(b_, L, NA_HEADS, NA_HEAD_DIM)
    v = v.reshape(b_, L, NA_HEADS, NA_HEAD_DIM)
    if ctx is None:
        o_a = _attn_context(q, k, v)
        h0_f = jnp.zeros((b_, SSM_HEADS, SSM_HEADDIM, SSM_STATE), x.dtype)
        h0_b = h0_f
    else:
        k_ctx, v_ctx, h0_f, h0_b = ctx
        o_a = _neighbourhood_attn(q, k, v, k_ctx, v_ctx, p['rpb'])
    o_a = o_a * jax.nn.silu(g_a)
    o_b, h_f, h_b = _ssm_branch(xbc, z, dt_raw, p, h0_f, h0_b)
    o_c = _sgu_branch(u, v_c, g_c, p)
    gate_a, gate_b, gate_c = jnp.split(jax.nn.sigmoid(g_m), 3, -1)
    merged = (gate_a * jnp.matmul(o_a, p['w_br_a']) + gate_b * jnp.matmul(o_b, p['w_br_b'])
              + gate_c * jnp.matmul(o_c, p['w_br_c']))
    y = _rms(jnp.matmul(merged, p['w_out']), p['g_post'])
    return (x + gate * y).astype(x.dtype), (k, v, h_f, h_b)


def setup_inputs(seed: int = 0) -> dict:
    key = jax.random.key(seed)
    ks = jax.random.split(key, 32)
    f32 = jnp.float32
    D = D_MODEL

    def nrm(k, shape, scale=1.0):
        return jax.random.normal(k, shape, f32) * scale

    dt0 = jnp.exp(jax.random.uniform(ks[20], (DEPTH, 2, SSM_HEADS), f32, math.log(1e-3), math.log(1e-1)))
    dt_bias = dt0 + jnp.log(-jnp.expm1(-dt0))
    a_log = jnp.log(jax.random.uniform(ks[21], (DEPTH, 2, SSM_HEADS), f32, 1.0, 16.0))
    return {
        'x_prompt': nrm(ks[0], (BATCH, SEQ, D)),
        'x_sample': nrm(ks[1], (DEC_BATCH, DEC_SEQ, D)),
        'c': nrm(ks[2], (DEC_BATCH, D)),
        'cache_k': nrm(ks[3], (DEC_BATCH, DEPTH, PAST_LEN, NA_HEADS, NA_HEAD_DIM)),
        'cache_v': nrm(ks[4], (DEC_BATCH, DEPTH, PAST_LEN, NA_HEADS, NA_HEAD_DIM)),
        'state_ssm_fwd': nrm(ks[5], (DEC_BATCH, DEPTH, SSM_HEADS, SSM_HEADDIM, SSM_STATE), 0.5),
        'state_ssm_bwd': nrm(ks[6], (DEC_BATCH, DEPTH, SSM_HEADS, SSM_HEADDIM, SSM_STATE), 0.5),
        'c_ctx': nrm(ks[7], (D,)),
        'w_mod': nrm(ks[8], (DEPTH, D, 3 * D), 0.2 * D ** -0.5),
        'b_mod': nrm(ks[9], (DEPTH, 3 * D), 0.01),
        'g_pre': 1.0 + nrm(ks[10], (DEPTH, D), 0.01),
        'g_post': 1.0 + nrm(ks[11], (DEPTH, D), 0.01),
        'w_in': nrm(ks[12], (DEPTH, D, N_IN), D ** -0.5),
        'rpb': nrm(ks[13], (DEPTH, NA_HEADS, 2 * NA_KH - 1, 2 * NA_KW - 1), 0.1),
        'conv_w': nrm(ks[14], (DEPTH, CONV_CH, SSM_CONV), SSM_CONV ** -0.5),
        'conv_b': nrm(ks[15], (DEPTH, CONV_CH), 0.01),
        'dt_bias': dt_bias,
        'a_log': a_log,
        'd_skip': 1.0 + nrm(ks[16], (DEPTH, SSM_HEADS), 0.01),
        'g_ssm': 1.0 + nrm(ks[17], (DEPTH, SSM_W), 0.01),
        'w_s': nrm(ks[18], (DEPTH, SGU_GROUPS, SGU_CHUNK, SGU_CHUNK), SGU_CHUNK ** -0.5),
        'b_s': nrm(ks[19], (DEPTH, SGU_GROUPS, SGU_CHUNK), 0.01),
        'g_sgu': 1.0 + nrm(ks[22], (DEPTH, SGU_W), 0.01),
        'w_br_a': nrm(ks[23], (DEPTH, NA_W, D), NA_W ** -0.5),
        'w_br_b': nrm(ks[24], (DEPTH, SSM_W, D), SSM_W ** -0.5),
        'w_br_c': nrm(ks[25], (DEPTH, SGU_W, D), SGU_W ** -0.5),
        'w_out': nrm(ks[26], (DEPTH, D, D), D ** -0.5),
    }


def reference(x_prompt, x_sample, c, cache_k, cache_v, state_ssm_fwd, state_ssm_bwd, c_ctx,
              w_mod, b_mod, g_pre, g_post, w_in, rpb, conv_w, conv_b, dt_bias, a_log, d_skip,
              g_ssm, w_s, b_s, g_sgu, w_br_a, w_br_b, w_br_c, w_out):
    y_p = x_prompt
    y_s = x_sample
    ks_l, vs_l, hf_l, hb_l = [], [], [], []
    for l in range(DEPTH):
        p = {'w_mod': w_mod[l], 'b_mod': b_mod[l], 'g_pre': g_pre[l], 'g_post': g_post[l],
             'w_in': w_in[l], 'rpb': rpb[l], 'conv_w': conv_w[l], 'conv_b': conv_b[l],
             'dt_bias': dt_bias[l], 'a_log': a_log[l], 'd_skip': d_skip[l], 'g_ssm': g_ssm[l],
             'w_s': w_s[l], 'b_s': b_s[l], 'g_sgu': g_sgu[l], 'w_br_a': w_br_a[l],
             'w_br_b': w_br_b[l], 'w_br_c': w_br_c[l], 'w_out': w_out[l]}
        y_p, (k_l, v_l, h_f, h_b) = _layer(y_p, c_ctx[None, :], p, None)
        ks_l.append(k_l)
        vs_l.append(v_l)
        hf_l.append(h_f)
        hb_l.append(h_b)
        y_s, _ = _layer(y_s, c, p, (cache_k[:, l], cache_v[:, l], state_ssm_fwd[:, l], state_ssm_bwd[:, l]))
    new_cache_k = jnp.stack(ks_l, axis=1)
    new_cache_v = jnp.stack(vs_l, axis=1)
    new_state_fwd = jnp.stack(hf_l, axis=1)
    new_state_bwd = jnp.stack(hb_l, axis=1)
    return (y_p, y_s, new_cache_k, new_cache_v, new_state_fwd, new_state_bwd)
```

```python
import functools

import jax
import jax.numpy as jnp
from jax import lax
from jax.experimental import pallas as pl
from jax.experimental.pallas import tpu as pltpu

D_MODEL = 2048
DEPTH = 2
EPS = 1e-6
GRID_W = 64
NA_HEAD_DIM = 64
NA_W = D_MODEL // 2
NA_HEADS = NA_W // NA_HEAD_DIM
NA_KH = 8
NA_KW = 16
SSM_HEADDIM = 64
SSM_W = D_MODEL // 2
SSM_HEADS = SSM_W // SSM_HEADDIM
SSM_GROUPS = 4
SSM_STATE = 128
SSM_CHUNK = 128
CONV_CH = SSM_W + 2 * SSM_GROUPS * SSM_STATE
SGU_W = D_MODEL // 2
SGU_GROUPS = 8
SGU_CHUNK = 128

N_MAIN = 16 * 1024
DT_PAD = 128
COL_Q, COL_K, COL_V, COL_GA, COL_XBC, COL_Z, COL_U, COL_VC, COL_GC, COL_GM = 0, 1, 2, 3, 4, 6, 7, 8, 9, 10

NEG = -1e30
MOD_ROWS = 8
VMEM_LIMIT = 56 * 1024 * 1024

_F32 = jnp.float32
_BF16 = jnp.bfloat16
_NT = (((1,), (1,)), ((), ()))


def _sigmoid(x):
    return 1.0 / (1.0 + jnp.exp(-x))


def _silu(x):
    return x * _sigmoid(x)


def _softplus(x):
    return jnp.maximum(x, 0.0) + jnp.log(1.0 + jnp.exp(-jnp.abs(x)))


def _dot(a, b):
    return jnp.dot(a, b, preferred_element_type=_F32)


def _params(*sem):
    return pltpu.CompilerParams(dimension_semantics=sem, vmem_limit_bytes=VMEM_LIMIT)


def _mod_body(c_ref, w_ref, b_ref, o_ref):
    s = _silu(c_ref[...]).astype(_BF16)
    o_ref[...] = _dot(s, w_ref[...].astype(_BF16)) + b_ref[...]


def _mod_call(cvecs, w_mod, b_mod):
    tn = 1024
    n3 = 3 * D_MODEL
    return pl.pallas_call(
        _mod_body,
        grid=(DEPTH, n3 // tn),
        in_specs=[pl.BlockSpec((MOD_ROWS, D_MODEL), lambda l, j: (0, 0)),
                  pl.BlockSpec((None, D_MODEL, tn), lambda l, j: (l, 0, j)),
                  pl.BlockSpec((None, 1, tn), lambda l, j: (l, 0, j))],
        out_specs=pl.BlockSpec((None, MOD_ROWS, tn), lambda l, j: (l, 0, j)),
        out_shape=jax.ShapeDtypeStruct((DEPTH, MOD_ROWS, n3), _F32),
        compiler_params=_params("arbitrary", "arbitrary"),
        name="modulation",
    )(cvecs, w_mod, b_mod.reshape(DEPTH, 1, n3))


def _inproj_body(x_ref, mod_ref, g_ref, w_ref, wdt_ref, o_ref, dt_ref, h_scr):
    @pl.when(pl.program_id(1) == 0)
    def _():
        x = x_ref[...]
        xn = x * lax.rsqrt(jnp.mean(x * x, axis=-1, keepdims=True) + EPS) * g_ref[...]
        shift = mod_ref[:, 0:D_MODEL]
        scale = mod_ref[:, D_MODEL:2 * D_MODEL]
        h = (xn * (1.0 + scale) + shift).astype(_BF16)
        h_scr[...] = h
        dt_ref[...] = _dot(h, wdt_ref[...])

    o_ref[...] = _dot(h_scr[...], w_ref[...])


def _inproj_call(x2d, mod_l, g_pre_l, w_main, w_dt, mod_row):
    t = x2d.shape[0]
    tm, tn = 1024, 1024
    return pl.pallas_call(
        _inproj_body,
        grid=(t // tm, N_MAIN // tn),
        in_specs=[pl.BlockSpec((tm, D_MODEL), lambda i, j: (i, 0)),
                  pl.BlockSpec((None, 1, 3 * D_MODEL), lambda i, j: (mod_row(i, tm), 0, 0)),
                  pl.BlockSpec((1, D_MODEL), lambda i, j: (0, 0)),
                  pl.BlockSpec((D_MODEL, tn), lambda i, j: (0, j)),
                  pl.BlockSpec((D_MODEL, DT_PAD), lambda i, j: (0, 0))],
        out_specs=[pl.BlockSpec((tm, tn), lambda i, j: (i, j)),
                   pl.BlockSpec((tm, DT_PAD), lambda i, j: (i, 0))],
        out_shape=[jax.ShapeDtypeStruct((t, N_MAIN), _F32),
                   jax.ShapeDtypeStruct((t, DT_PAD), _F32)],
        scratch_shapes=[pltpu.VMEM((tm, D_MODEL), _BF16)],
        compiler_params=_params("arbitrary", "arbitrary"),
        name="inproj",
    )(x2d, mod_l, g_pre_l, w_main, w_dt)


def _ctx_attn_body(q_ref, k_ref, v_ref, ga_ref, o_ref):
    q = (q_ref[...] * (NA_HEAD_DIM ** -0.5)).astype(_BF16)
    k = k_ref[...].astype(_BF16)
    v = v_ref[...].astype(_BF16)
    outs = []
    for hh in range(2):
        sl = slice(hh * NA_HEAD_DIM, (hh + 1) * NA_HEAD_DIM)
        s = lax.dot_general(q[:, sl], k[:, sl], _NT, preferred_element_type=_F32)
        p = jnp.exp(s - jnp.max(s, axis=-1, keepdims=True))
        den = jnp.sum(p, axis=-1, keepdims=True)
        outs.append(_dot(p.astype(_BF16), v[:, sl]) / den)
    o = jnp.concatenate(outs, axis=-1)
    o_ref[...] = (o * _silu(ga_ref[...])).astype(_BF16)


def _ctx_attn_call(p3):
    b, l, _ = p3.shape
    hp = 2 * NA_HEAD_DIM
    nblk = N_MAIN // hp // 16

    def spec(col):
        return pl.BlockSpec((None, l, hp), lambda i, j: (i, 0, col * nblk + j))

    return pl.pallas_call(
        _ctx_attn_body,
        grid=(b, NA_HEADS // 2),
        in_specs=[spec(COL_Q), spec(COL_K), spec(COL_V), spec(COL_GA)],
        out_specs=pl.BlockSpec((None, l, hp), lambda i, j: (i, 0, j)),
        out_shape=jax.ShapeDtypeStruct((b, l, NA_W), _BF16),
        compiler_params=_params("arbitrary", "arbitrary"),
        name="ctx_attn",
    )(p3, p3, p3, p3)


def _na_bias_slabs(rpb_l):
    j = jnp.arange(GRID_W)[:, None]
    c = jnp.arange(GRID_W)[None, :]
    cs = jnp.clip(j - NA_KW // 2, 0, GRID_W - NA_KW)
    valid = (c >= cs) & (c < cs + NA_KW)
    cidx = jnp.clip(c - j + (NA_KW - 1), 0, 2 * NA_KW - 2)
    tbl = jnp.where(valid[None, None], rpb_l.astype(_F32)[:, :, cidx], NEG)
    slabs = jnp.stack([tbl[:, v:v + NA_KH] for v in range(NA_KH)], axis=0)
    return slabs.transpose(0, 1, 3, 2, 4).reshape(NA_KH, NA_HEADS, GRID_W, NA_KH * GRID_W)


def _na_body(q_ref, k_ref, v_ref, ga_ref, kc_ref, vc_ref, bias_ref, o_ref, kb_scr, vb_scr, *, rows):
    kb_scr[...] = k_ref[...].astype(_BF16)
    vb_scr[...] = v_ref[...].astype(_BF16)
    kc = kc_ref[...].astype(_BF16)
    vc = vc_ref[...].astype(_BF16)
    win = NA_KH * GRID_W

    def row(r, carry):
        rs = jnp.clip(r - NA_KH // 2, 0, rows - NA_KH)
        dr0 = rs - r + (NA_KH - 1)
        q0 = pl.multiple_of(r * GRID_W, GRID_W)
        k0 = pl.multiple_of(rs * GRID_W, GRID_W)
        q = (q_ref[pl.ds(q0, GRID_W), :] * (NA_HEAD_DIM ** -0.5)).astype(_BF16)
        kw = kb_scr[pl.ds(k0, win), :]
        vw = vb_scr[pl.ds(k0, win), :]
        outs = []
        for hh in range(2):
            sl = slice(hh * NA_HEAD_DIM, (hh + 1) * NA_HEAD_DIM)
            qh = q[:, sl]
            s_loc = lax.dot_general(qh, kw[:, sl], _NT, preferred_element_type=_F32) + bias_ref[dr0, hh]
            s_ctx = lax.dot_general(qh, kc[:, sl], _NT, preferred_element_type=_F32)
            m = jnp.maximum(jnp.max(s_loc, axis=-1, keepdims=True), jnp.max(s_ctx, axis=-1, keepdims=True))
            p_loc = jnp.exp(s_loc - m)
            p_ctx = jnp.exp(s_ctx - m)
            den = jnp.sum(p_loc, axis=-1, keepdims=True) + jnp.sum(p_ctx, axis=-1, keepdims=True)
            o = _dot(p_loc.astype(_BF16), vw[:, sl]) + _dot(p_ctx.astype(_BF16), vc[:, sl])
            outs.append(o / den)
        o = jnp.concatenate(outs, axis=-1)
        o_ref[pl.ds(q0, GRID_W), :] = (o * _silu(ga_ref[pl.ds(q0, GRID_W), :])).astype(_BF16)
        return carry

    lax.fori_loop(0, rows, row, 0)


def _na_call(p3, cache_k4, cache_v4, slabs, layer):
    b, l, _ = p3.shape
    rows = l // GRID_W
    assert rows >= NA_KH
    hp = 2 * NA_HEAD_DIM
    nblk = N_MAIN // hp // 16
    lc = cache_k4.shape[2]

    def spec(col):
        return pl.BlockSpec((None, l, hp), lambda i, j: (i, 0, col * nblk + j))

    cspec = pl.BlockSpec((None, None, lc, hp), lambda i, j: (i, layer, 0, j))
    return pl.pallas_call(
        functools.partial(_na_body, rows=rows),
        grid=(b, NA_HEADS // 2),
        in_specs=[spec(COL_Q), spec(COL_K), spec(COL_V), spec(COL_GA), cspec, cspec,
                  pl.BlockSpec((NA_KH, 2, GRID_W, NA_KH * GRID_W), lambda i, j: (0, j, 0, 0))],
        out_specs=pl.BlockSpec((None, l, hp), lambda i, j: (i, 0, j)),
        out_shape=jax.ShapeDtypeStruct((b, l, NA_W), _BF16),
        scratch_shapes=[pltpu.VMEM((l, hp), _BF16), pltpu.VMEM((l, hp), _BF16)],
        compiler_params=_params("arbitrary", "arbitrary"),
        name="na_attn",
    )(p3, p3, p3, p3, cache_k4, cache_v4, slabs)


def _split3(x):
    hi = x.astype(_BF16)
    r1 = x - hi.astype(_F32)
    mid = r1.astype(_BF16)
    lo = (r1 - mid.astype(_F32)).astype(_BF16)
    return hi, mid, lo


def _ssd_body(*refs, reverse, has_h0, nc):
    it = iter(refs)
    x_ref, prev_ref, next_ref, dt_ref = next(it), next(it), next(it), next(it)
    cw_ref, cb_ref, dtb_row_ref, alog_row_ref, dtb_col_ref, alog_col_ref = (next(it) for _ in range(6))
    h0_ref = next(it) if has_h0 else None
    if reverse:
        yf_ref, z_ref, dskip_ref, gssm_ref = next(it), next(it), next(it), next(it)
    y_out_ref, hout_ref = next(it), next(it)
    h_scr, y_scr = next(it), next(it)

    q = SSM_CHUNK
    c = pl.program_id(1)
    cc = (nc - 1 - c) if reverse else c
    d = 1 if reverse else 0
    last = 0 if reverse else q - 1

    @pl.when(c == 0)
    def _():
        if has_h0:
            h_scr[...] = h0_ref[...]
        else:
            h_scr[...] = jnp.zeros_like(h_scr)

    x = x_ref[...]
    xp = jnp.where(cc > 0, prev_ref[7:8, :], 0.0)
    xn = jnp.where(cc < nc - 1, next_ref[0:1, :], 0.0)
    ridx = lax.broadcasted_iota(jnp.int32, (q, 1), 0)
    x_m1 = jnp.where(ridx == 0, xp, pltpu.roll(x, 1, axis=0))
    x_p1 = jnp.where(ridx == q - 1, xn, pltpu.roll(x, q - 1, axis=0))
    act = _silu(cw_ref[0:1, :] * x_m1 + cw_ref[1:2, :] * x + cw_ref[2:3, :] * x_p1 + cb_ref[...])
    xs = act[:, :SSM_W]
    bm = act[:, SSM_W:SSM_W + SSM_GROUPS * SSM_STATE].astype(_BF16)
    cm = act[:, SSM_W + SSM_GROUPS * SSM_STATE:].astype(_BF16)

    dtraw = dt_ref[...]
    dt_col = _softplus(dtraw + dtb_row_ref[...])
    dta_col = dt_col * (-jnp.exp(alog_row_ref[...]))
    dt_row_all = _softplus(dtraw.T + dtb_col_ref[...])
    dta_row_all = dt_row_all * (-jnp.exp(alog_col_ref[...]))
    dt_row = dt_row_all[d * SSM_HEADS:(d + 1) * SSM_HEADS, :]
    dta_row = dta_row_all[d * SSM_HEADS:(d + 1) * SSM_HEADS, :]

    ri = lax.broadcasted_iota(jnp.int32, (q, q), 0)
    ci = lax.broadcasted_iota(jnp.int32, (q, q), 1)
    causal = (ci >= ri) if reverse else (ci <= ri)
    t_col = jnp.where(causal, 1.0, 0.0).astype(_BF16)
    t_row = jnp.where((ri >= ci) if reverse else (ri <= ci), 1.0, 0.0).astype(_BF16)
    acum_col = sum(_dot(t_col, part) for part in _split3(dta_col))
    acum_row = sum(_dot(part, t_row) for part in _split3(dta_row))

    ea_col = jnp.exp(acum_col)
    de_row = jnp.exp(acum_row[:, last:last + 1] - acum_row)
    dtde_row = dt_row * de_row
    cdecay = jnp.exp(acum_row[:, last:last + 1])
    xs_t = xs.T
    xs_b = xs.astype(_BF16)

    hg = SSM_HEADS // SSM_GROUPS
    for g in range(SSM_GROUPS):
        nsl = slice(g * SSM_STATE, (g + 1) * SSM_STATE)
        b_g = bm[:, nsl]
        c_g = cm[:, nsl]
        cb = lax.dot_general(c_g, b_g, _NT, preferred_element_type=_F32)
        for hh in range(hg):
            h = g * hg + hh
            psl = slice(h * SSM_HEADDIM, (h + 1) * SSM_HEADDIM)
            lane = d * SSM_HEADS + h
            seg = acum_col[:, lane:lane + 1] - acum_row[h:h + 1, :]
            lmat = jnp.exp(jnp.where(causal, seg, NEG))
            m = (cb * lmat * dt_row[h:h + 1, :]).astype(_BF16)
            y_diag = _dot(m, xs_b[:, psl])
            h_prev = h_scr[h]
            y_off = ea_col[:, lane:lane + 1] * lax.dot_general(
                c_g, h_prev.astype(_BF16), _NT, preferred_element_type=_F32)
            y_scr[:, psl] = y_diag + y_off
            st = _dot((xs_t[psl, :] * dtde_row[h:h + 1, :]).astype(_BF16), b_g)
            h_scr[h] = h_prev * cdecay[h:h + 1, :] + st

    if reverse:
        y = yf_ref[...] + y_scr[...] + xs * dskip_ref[...]
        y = y * _silu(z_ref[...])
        y = y * lax.rsqrt(jnp.mean(y * y, axis=-1, keepdims=True) + EPS) * gssm_ref[...]
        y_out_ref[...] = y.astype(_BF16)
    else:
        y_out_ref[...] = y_scr[...]

    @pl.when(c == nc - 1)
    def _():
        hout_ref[...] = h_scr[...]


def _ssd_call(p3, dt3, prm, h0, layer, reverse, y_fwd=None):
    b, l, _ = p3.shape
    q = SSM_CHUNK
    nc = l // q
    has_h0 = h0 is not None
    xbc_blk = COL_XBC // 2

    def cidx(c):
        return (nc - 1 - c) if reverse else c

    in_specs = [
        pl.BlockSpec((None, q, CONV_CH), lambda i, c: (i, cidx(c), xbc_blk)),
        pl.BlockSpec((None, 8, CONV_CH), lambda i, c: (i, jnp.maximum(cidx(c) * (q // 8) - 1, 0), xbc_blk)),
        pl.BlockSpec((None, 8, CONV_CH), lambda i, c: (i, jnp.minimum((cidx(c) + 1) * (q // 8), l // 8 - 1), xbc_blk)),
        pl.BlockSpec((None, q, DT_PAD), lambda i, c: (i, cidx(c), 0)),
        pl.BlockSpec((3, CONV_CH), lambda i, c: (0, 0)),
        pl.BlockSpec((1, CONV_CH), lambda i, c: (0, 0)),
        pl.BlockSpec((1, DT_PAD), lambda i, c: (0, 0)),
        pl.BlockSpec((1, DT_PAD), lambda i, c: (0, 0)),
        pl.BlockSpec((DT_PAD, 1), lambda i, c: (0, 0)),
        pl.BlockSpec((DT_PAD, 1), lambda i, c: (0, 0)),
    ]
    args = [p3, p3, p3, dt3, prm["conv_wt"], prm["conv_b"], prm["dtb_row"], prm["alog_row"],
            prm["dtb_col"], prm["alog_col"]]
    if has_h0:
        in_specs.append(pl.BlockSpec((None, None, SSM_HEADS, SSM_HEADDIM, SSM_STATE),
                                     lambda i, c: (i, layer, 0, 0, 0)))
        args.append(h0)
    if reverse:
        in_specs += [
            pl.BlockSpec((None, q, SSM_W), lambda i, c: (i, cidx(c), 0)),
            pl.BlockSpec((None, q, SSM_W), lambda i, c: (i, cidx(c), COL_Z)),
            pl.BlockSpec((1, SSM_W), lambda i, c: (0, 0)),
            pl.BlockSpec((1, SSM_W), lambda i, c: (0, 0)),
        ]
        args += [y_fwd, p3, prm["dskip"], prm["g_ssm"]]
    y_dtype = _BF16 if reverse else _F32
    return pl.pallas_call(
        functools.partial(_ssd_body, reverse=reverse, has_h0=has_h0, nc=nc),
        grid=(b, nc),
        in_specs=in_specs,
        out_specs=[pl.BlockSpec((None, q, SSM_W), lambda i, c: (i, cidx(c), 0)),
                   pl.BlockSpec((None, SSM_HEADS, SSM_HEADDIM, SSM_STATE), lambda i, c: (i, 0, 0, 0))],
        out_shape=[jax.ShapeDtypeStruct((b, l, SSM_W), y_dtype),
                   jax.ShapeDtypeStruct((b, SSM_HEADS, SSM_HEADDIM, SSM_STATE), _F32)],
        scratch_shapes=[pltpu.VMEM((SSM_HEADS, SSM_HEADDIM, SSM_STATE), _F32),
                        pltpu.VMEM((q, SSM_W), _F32)],
        compiler_params=_params("arbitrary", "arbitrary"),
        name="ssd_bwd" if reverse else "ssd_fwd",
    )(*args)


def _sgu_body(u_ref, v_ref, g_ref, gs_ref, ws_ref, bs_ref, o_ref, *, chunks):
    v = v_ref[...]
    mu = jnp.mean(v, axis=-1, keepdims=True)
    vc = v - mu
    var = jnp.mean(vc * vc, axis=-1, keepdims=True)
    vn = (vc * lax.rsqrt(var + EPS) * gs_ref[...]).astype(_BF16)
    ge = SGU_W // SGU_GROUPS
    for ch in range(chunks):
        rsl = slice(ch * SGU_CHUNK, (ch + 1) * SGU_CHUNK)
        for g in range(SGU_GROUPS):
            csl = slice(g * ge, (g + 1) * ge)
            vs = _dot(ws_ref[g], vn[rsl, csl]) + bs_ref[:, g:g + 1]
            y = u_ref[rsl, csl] * vs * _silu(g_ref[rsl, csl])
            o_ref[rsl, csl] = y.astype(_BF16)


def _sgu_call(p2, g_sgu, w_s, b_s_t):
    t = p2.shape[0]
    chunks = 4
    tr = chunks * SGU_CHUNK
    nblk = N_MAIN // SGU_W // 16

    def spec(col):
        return pl.BlockSpec((tr, SGU_W), lambda i: (i, col * nblk))

    return pl.pallas_call(
        functools.partial(_sgu_body, chunks=chunks),
        grid=(t // tr,),
        in_specs=[spec(COL_U), spec(COL_VC), spec(COL_GC),
                  pl.BlockSpec((1, SGU_W), lambda i: (0, 0)),
                  pl.BlockSpec((SGU_GROUPS, SGU_CHUNK, SGU_CHUNK), lambda i: (0, 0, 0)),
                  pl.BlockSpec((SGU_CHUNK, SGU_GROUPS), lambda i: (0, 0))],
        out_specs=pl.BlockSpec((tr, SGU_W), lambda i: (i, 0)),
        out_shape=jax.ShapeDtypeStruct((t, SGU_W), _BF16),
        compiler_params=_params("arbitrary"),
        name="sgu",
    )(p2, p2, p2, g_sgu, w_s, b_s_t)


def _merge_body(oa_ref, ob_ref, oc_ref, ga_ref, gb_ref, gc_ref, x_ref, mod_ref, gpost_ref,
                wa_ref, wb_ref, wc_ref, wo_ref, out_ref):
    merged = _sigmoid(ga_ref[...]) * _dot(oa_ref[...], wa_ref[...])
    merged = merged + _sigmoid(gb_ref[...]) * _dot(ob_ref[...], wb_ref[...])
    merged = merged + _sigmoid(gc_ref[...]) * _dot(oc_ref[...], wc_ref[...])
    y = _dot(merged.astype(_BF16), wo_ref[...])
    y = y * lax.rsqrt(jnp.mean(y * y, axis=-1, keepdims=True) + EPS) * gpost_ref[...]
    out_ref[...] = x_ref[...] + mod_ref[:, 2 * D_MODEL:] * y


def _merge_call(o_a, o_b, o_c, p2, x2d, mod_l, g_post_l, wa, wb, wc, wo, mod_row):
    t = x2d.shape[0]
    tm = 256
    gm_blk = COL_GM // 2
    once = pl.Buffered(1)

    def row_spec(w):
        return pl.BlockSpec((tm, w), lambda i: (i, 0))

    def gm_spec(k):
        return pl.BlockSpec((tm, D_MODEL), lambda i: (i, gm_blk + k))

    def w_spec(kdim):
        return pl.BlockSpec((kdim, D_MODEL), lambda i: (0, 0), pipeline_mode=once)

    return pl.pallas_call(
        _merge_body,
        grid=(t // tm,),
        in_specs=[row_spec(NA_W), row_spec(SSM_W), row_spec(SGU_W), gm_spec(0), gm_spec(1), gm_spec(2),
                  row_spec(D_MODEL),
                  pl.BlockSpec((None, 1, 3 * D_MODEL), lambda i: (mod_row(i, tm), 0, 0)),
                  pl.BlockSpec((1, D_MODEL), lambda i: (0, 0)),
                  w_spec(NA_W), w_spec(SSM_W), w_spec(SGU_W), w_spec(D_MODEL)],
        out_specs=row_spec(D_MODEL),
        out_shape=jax.ShapeDtypeStruct((t, D_MODEL), _F32),
        compiler_params=_params("arbitrary"),
        name="merge_out",
    )(o_a, o_b, o_c, p2, p2, p2, x2d, mod_l, g_post_l, wa, wb, wc, wo)


def _layer(x3, mod_l, wts, mod_row, ctx, layer):
    b, l, _ = x3.shape
    t = b * l
    x2d = x3.reshape(t, D_MODEL)
    p2, dt2 = _inproj_call(x2d, mod_l, wts["g_pre"], wts["w_main"], wts["w_dt"], mod_row)
    p3 = p2.reshape(b, l, N_MAIN)
    dt3 = dt2.reshape(b, l, DT_PAD)
    if ctx is None:
        o_a = _ctx_attn_call(p3)
        h0_f = h0_b = None
    else:
        cache_k4, cache_v4, slabs, h0_f, h0_b = ctx
        o_a = _na_call(p3, cache_k4, cache_v4, slabs, layer)
    y_f, h_f = _ssd_call(p3, dt3, wts, h0_f, layer, reverse=False)
    o_b, h_b = _ssd_call(p3, dt3, wts, h0_b, layer, reverse=True, y_fwd=y_f)
    o_c = _sgu_call(p2, wts["g_sgu"], wts["w_s"], wts["b_s_t"])
    y2d = _merge_call(o_a.reshape(t, NA_W), o_b.reshape(t, SSM_W), o_c, p2, x2d, mod_l, wts["g_post"],
                      wts["w_br_a"], wts["w_br_b"], wts["w_br_c"], wts["w_out"], mod_row)
    k = p3[:, :, COL_K * 1024:(COL_K + 1) * 1024].reshape(b, l, NA_HEADS, NA_HEAD_DIM)
    v = p3[:, :, COL_V * 1024:(COL_V + 1) * 1024].reshape(b, l, NA_HEADS, NA_HEAD_DIM)
    return y2d.reshape(b, l, D_MODEL), (k, v, h_f, h_b)


def _layer_weights(l, g_pre, g_post, w_in, conv_w, conv_b, dt_bias, a_log, d_skip, g_ssm, w_s, b_s, g_sgu,
                   w_br_a, w_br_b, w_br_c, w_out):
    n_dt = 2 * SSM_HEADS
    o_dt = 4 * NA_W + CONV_CH + SSM_W
    w = w_in[l]
    w_main = jnp.concatenate([w[:, :o_dt], w[:, o_dt + n_dt:]], axis=1).astype(_BF16)
    w_dt = jnp.pad(w[:, o_dt:o_dt + n_dt], ((0, 0), (0, DT_PAD - n_dt))).astype(_BF16)
    dtb = jnp.pad(dt_bias[l].reshape(n_dt).astype(_F32), (0, DT_PAD - n_dt))
    alog = jnp.pad(a_log[l].reshape(n_dt).astype(_F32), (0, DT_PAD - n_dt))
    return {
        "g_pre": g_pre[l].reshape(1, D_MODEL), "g_post": g_post[l].reshape(1, D_MODEL),
        "w_main": w_main, "w_dt": w_dt,
        "conv_wt": conv_w[l].T, "conv_b": conv_b[l].reshape(1, CONV_CH),
        "dtb_row": dtb.reshape(1, DT_PAD), "alog_row": alog.reshape(1, DT_PAD),
        "dtb_col": dtb.reshape(DT_PAD, 1), "alog_col": alog.reshape(DT_PAD, 1),
        "dskip": jnp.repeat(d_skip[l].astype(_F32), SSM_HEADDIM).reshape(1, SSM_W),
        "g_ssm": g_ssm[l].reshape(1, SSM_W),
        "g_sgu": g_sgu[l].reshape(1, SGU_W), "w_s": w_s[l].astype(_BF16), "b_s_t": b_s[l].T,
        "w_br_a": w_br_a[l].astype(_BF16), "w_br_b": w_br_b[l].astype(_BF16),
        "w_br_c": w_br_c[l].astype(_BF16), "w_out": w_out[l].astype(_BF16),
    }


def kernel(x_prompt, x_sample, c, cache_k, cache_v, state_ssm_fwd, state_ssm_bwd, c_ctx, w_mod, b_mod, g_pre,
           g_post, w_in, rpb, conv_w, conv_b, dt_bias, a_log, d_skip, g_ssm, w_s, b_s, g_sgu, w_br_a, w_br_b,
           w_br_c, w_out):
    nb, ls, _ = x_sample.shape
    assert 1 + nb <= MOD_ROWS
    cvecs = jnp.concatenate([c_ctx[None, :], c, jnp.zeros((MOD_ROWS - 1 - nb, D_MODEL), _F32)], axis=0)
    mod = _mod_call(cvecs, w_mod, b_mod).reshape(DEPTH, MOD_ROWS, 1, 3 * D_MODEL)
    past = cache_k.shape[2]
    cache_k4 = cache_k.reshape(nb, DEPTH, past, NA_W)
    cache_v4 = cache_v.reshape(nb, DEPTH, past, NA_W)

    def prompt_row(i, tm):
        return 0

    def sample_row(i, tm):
        return 1 + (i * tm) // ls

    y_p, y_s = x_prompt, x_sample
    ks_l, vs_l, hf_l, hb_l = [], [], [], []
    for l in range(DEPTH):
        wts = _layer_weights(l, g_pre, g_post, w_in, conv_w, conv_b, dt_bias, a_log, d_skip, g_ssm, w_s, b_s,
                             g_sgu, w_br_a, w_br_b, w_br_c, w_out)
        y_p, (k_l, v_l, h_f, h_b) = _layer(y_p, mod[l], wts, prompt_row, None, l)
        ks_l.append(k_l)
        vs_l.append(v_l)
        hf_l.append(h_f)
        hb_l.append(h_b)
        ctx = (cache_k4, cache_v4, _na_bias_slabs(rpb[l]), state_ssm_fwd, state_ssm_bwd)
        y_s, _ = _layer(y_s, mod[l], wts, sample_row, ctx, l)
    return (y_p, y_s, jnp.stack(ks_l, axis=1), jnp.stack(vs_l, axis=1),
            jnp.stack(hf_l, axis=1), jnp.stack(hb_l, axis=1))
```

```python
import functools

import jax
import jax.numpy as jnp
from jax import lax
from jax.experimental import pallas as pl
from jax.experimental.pallas import tpu as pltpu

D_MODEL = 2048
DEPTH = 2
EPS = 1e-6
GRID_W = 64
NA_HEAD_DIM = 64
NA_W = D_MODEL // 2
NA_HEADS = NA_W // NA_HEAD_DIM
NA_KH = 8
NA_KW = 16
SSM_HEADDIM = 64
SSM_W = D_MODEL // 2
SSM_HEADS = SSM_W // SSM_HEADDIM
SSM_GROUPS = 4
SSM_STATE = 128
SSM_CHUNK = 128
CONV_CH = SSM_W + 2 * SSM_GROUPS * SSM_STATE
SGU_W = D_MODEL // 2
SGU_GROUPS = 8
SGU_CHUNK = 128

N_MAIN = 16 * 1024
DT_PAD = 128
COL_Q, COL_K, COL_V, COL_GA, COL_XBC, COL_Z, COL_U, COL_VC, COL_GC, COL_GM = 0, 1, 2, 3, 4, 6, 7, 8, 9, 10

NEG = -1e30
MOD_ROWS = 8
VMEM_LIMIT = 56 * 1024 * 1024

_F32 = jnp.float32
_BF16 = jnp.bfloat16
_NT = (((1,), (1,)), ((), ()))


def _sigmoid(x):
    return 1.0 / (1.0 + jnp.exp(-x))


def _silu(x):
    return x * _sigmoid(x)


def _softplus(x):
    return jnp.maximum(x, 0.0) + jnp.log(1.0 + jnp.exp(-jnp.abs(x)))


def _dot(a, b):
    return jnp.dot(a, b, preferred_element_type=_F32)


def _params(*sem):
    return pltpu.CompilerParams(dimension_semantics=sem, vmem_limit_bytes=VMEM_LIMIT)


def _mod_body(c_ref, w_ref, b_ref, o_ref):
    s = _silu(c_ref[...]).astype(_BF16)
    o_ref[...] = _dot(s, w_ref[...].astype(_BF16)) + b_ref[...]


def _mod_call(cvecs, w_mod, b_mod):
    tn = 1024
    n3 = 3 * D_MODEL
    return pl.pallas_call(
        _mod_body,
        grid=(DEPTH, n3 // tn),
        in_specs=[pl.BlockSpec((MOD_ROWS, D_MODEL), lambda l, j: (0, 0)),
                  pl.BlockSpec((None, D_MODEL, tn), lambda l, j: (l, 0, j)),
                  pl.BlockSpec((None, 1, tn), lambda l, j: (l, 0, j))],
        out_specs=pl.BlockSpec((None, MOD_ROWS, tn), lambda l, j: (l, 0, j)),
        out_shape=jax.ShapeDtypeStruct((DEPTH, MOD_ROWS, n3), _F32),
        compiler_params=_params("arbitrary", "arbitrary"),
        name="modulation",
    )(cvecs, w_mod, b_mod.reshape(DEPTH, 1, n3))


N_DT = 2 * SSM_HEADS
OFF_DT = 4 * NA_W + CONV_CH + SSM_W
REPACK_BLK = 1024


def _repack_body(a_ref, b_ref, o_ref):
    k = pl.program_id(2)

    @pl.when(k < OFF_DT // REPACK_BLK)
    def _():
        o_ref[...] = a_ref[...].astype(_BF16)

    @pl.when(k >= OFF_DT // REPACK_BLK)
    def _():
        cat = jnp.concatenate([a_ref[...], b_ref[...]], axis=1)
        o_ref[...] = cat[:, N_DT:N_DT + REPACK_BLK].astype(_BF16)


def _repack_call(w_in):
    tr = 512
    return pl.pallas_call(
        _repack_body,
        grid=(DEPTH, D_MODEL // tr, N_MAIN // REPACK_BLK),
        in_specs=[pl.BlockSpec((None, tr, REPACK_BLK), lambda l, r, k: (l, r, k)),
                  pl.BlockSpec((None, tr, 128), lambda l, r, k: (l, r, (k + 1) * (REPACK_BLK // 128)))],
        out_specs=pl.BlockSpec((None, tr, REPACK_BLK), lambda l, r, k: (l, r, k)),
        out_shape=jax.ShapeDtypeStruct((DEPTH, D_MODEL, N_MAIN), _BF16),
        compiler_params=_params("arbitrary", "arbitrary", "arbitrary"),
        name="repack_w_in",
    )(w_in, w_in)


def _inproj_body(x_ref, mod_ref, g_ref, w_ref, wdt_ref, o_ref, dt_ref, h_scr):
    @pl.when(pl.program_id(1) == 0)
    def _():
        x = x_ref[...]
        xn = x * lax.rsqrt(jnp.mean(x * x, axis=-1, keepdims=True) + EPS) * g_ref[...]
        shift = mod_ref[:, 0:D_MODEL]
        scale = mod_ref[:, D_MODEL:2 * D_MODEL]
        h = (xn * (1.0 + scale) + shift).astype(_BF16)
        h_scr[...] = h
        dt_ref[...] = _dot(h, wdt_ref[...])

    o_ref[...] = _dot(h_scr[...], w_ref[...])


def _inproj_call(x2d, mod_l, g_pre_l, w_main, layer, w_dt, mod_row):
    t = x2d.shape[0]
    tm, tn = 1024, 1024
    return pl.pallas_call(
        _inproj_body,
        grid=(t // tm, N_MAIN // tn),
        in_specs=[pl.BlockSpec((tm, D_MODEL), lambda i, j: (i, 0)),
                  pl.BlockSpec((None, 1, 3 * D_MODEL), lambda i, j: (mod_row(i, tm), 0, 0)),
                  pl.BlockSpec((1, D_MODEL), lambda i, j: (0, 0)),
                  pl.BlockSpec((None, D_MODEL, tn), lambda i, j: (layer, 0, j)),
                  pl.BlockSpec((D_MODEL, DT_PAD), lambda i, j: (0, 0))],
        out_specs=[pl.BlockSpec((tm, tn), lambda i, j: (i, j)),
                   pl.BlockSpec((tm, DT_PAD), lambda i, j: (i, 0))],
        out_shape=[jax.ShapeDtypeStruct((t, N_MAIN), _F32),
                   jax.ShapeDtypeStruct((t, DT_PAD), _F32)],
        scratch_shapes=[pltpu.VMEM((tm, D_MODEL), _BF16)],
        compiler_params=_params("arbitrary", "arbitrary"),
        name="inproj",
    )(x2d, mod_l, g_pre_l, w_main, w_dt)


def _ctx_attn_body(*refs):
    q_ref, k_ref, v_ref, ga_ref = refs[:4]
    o_ref, ko_ref, vo_ref = refs[-3:]
    l, hp = q_ref.shape
    k = k_ref[...]
    v = v_ref[...]
    ko_ref[...] = k
    vo_ref[...] = v
    head0 = lax.broadcasted_iota(jnp.int32, (l, hp), 1) < NA_HEAD_DIM
    q = q_ref[...] * (NA_HEAD_DIM ** -0.5)
    q2 = jnp.concatenate([jnp.where(head0, q, 0.0), jnp.where(head0, 0.0, q)], axis=0).astype(_BF16)
    s = lax.dot_general(q2, k.astype(_BF16), _NT, preferred_element_type=_F32)
    p = jnp.exp(s - jnp.max(s, axis=-1, keepdims=True)).astype(_BF16)
    v_ext = jnp.concatenate([v.astype(_BF16), jnp.ones((l, hp), _BF16)], axis=1)
    oe = _dot(p, v_ext)
    on = oe[:, :hp] / oe[:, hp:]
    o = jnp.where(head0, on[:l], on[l:])
    o_ref[...] = (o * _silu(ga_ref[...])).astype(_BF16)


def _ctx_attn_call(p3, layer, caches):
    b, l, _ = p3.shape
    hp = 2 * NA_HEAD_DIM
    nblk = N_MAIN // hp // 16

    def spec(col):
        return pl.BlockSpec((None, l, hp), lambda i, j: (i, 0, col * nblk + j))

    cache_spec = pl.BlockSpec((None, None, l, hp), lambda i, j: (i, layer, 0, j))
    cache_shape = jax.ShapeDtypeStruct((b, DEPTH, l, NA_W), _F32)
    in_specs = [spec(COL_Q), spec(COL_K), spec(COL_V), spec(COL_GA)]
    args = [p3, p3, p3, p3]
    aliases = {}
    if caches is not None:
        in_specs += [pl.BlockSpec(memory_space=pl.ANY)] * 2
        args += list(caches)
        aliases = {4: 1, 5: 2}
    o_a, k_all, v_all = pl.pallas_call(
        _ctx_attn_body,
        grid=(b, NA_HEADS // 2),
        in_specs=in_specs,
        out_specs=[pl.BlockSpec((None, l, hp), lambda i, j: (i, 0, j)), cache_spec, cache_spec],
        out_shape=[jax.ShapeDtypeStruct((b, l, NA_W), _BF16), cache_shape, cache_shape],
        input_output_aliases=aliases,
        compiler_params=_params("arbitrary", "arbitrary"),
        name="ctx_attn",
    )(*args)
    return o_a, (k_all, v_all)


def _na_bias_table(rpb_l):
    j = jnp.arange(GRID_W)[None, :, None]
    c = jnp.arange(GRID_W)[None, None, :]
    k = jnp.arange(2 * NA_KW - 1)[:, None, None]
    cs = jnp.clip(j - NA_KW // 2, 0, GRID_W - NA_KW)
    valid = (c >= cs) & (c < cs + NA_KW)
    sel = (valid & (c - j + (NA_KW - 1) == k)).astype(_F32)
    tbl = jnp.einsum("hdk,kjc->hdjc", rpb_l.astype(_F32), sel, precision=lax.Precision.HIGHEST)
    tbl = tbl + jnp.where(valid, 0.0, NEG)[None]
    return jnp.concatenate([tbl[:, :-1], tbl[:, 1:]], axis=-1)


def _na_body(q_ref, k_ref, v_ref, ga_ref, kc_ref, vc_ref, bias_ref, o_ref, kb_scr, vb_scr, kcb_scr, vcb_scr,
             *, rows):
    hp = 2 * NA_HEAD_DIM
    win = NA_KH * GRID_W
    kb_scr[...] = k_ref[...].astype(_BF16)
    vb_scr[:, :hp] = v_ref[...].astype(_BF16)
    vb_scr[:, hp:] = jnp.ones((vb_scr.shape[0], hp), _BF16)
    kcb_scr[...] = kc_ref[...].astype(_BF16)
    vcb_scr[:, :hp] = vc_ref[...].astype(_BF16)
    vcb_scr[:, hp:] = jnp.ones((vcb_scr.shape[0], hp), _BF16)
    head0 = lax.broadcasted_iota(jnp.int32, (GRID_W, hp), 1) < NA_HEAD_DIM

    def row(r, carry):
        rs = jnp.clip(r - NA_KH // 2, 0, rows - NA_KH)
        dr0 = rs - r + (NA_KH - 1)
        q0 = pl.multiple_of(r * GRID_W, GRID_W)
        k0 = pl.multiple_of(rs * GRID_W, GRID_W)
        q = q_ref[pl.ds(q0, GRID_W), :] * (NA_HEAD_DIM ** -0.5)
        q2 = jnp.concatenate([jnp.where(head0, q, 0.0), jnp.where(head0, 0.0, q)], axis=0).astype(_BF16)
        bias = jnp.concatenate(
            [jnp.concatenate([bias_ref[hh, dr0 + a] for a in range(0, NA_KH, 2)], axis=1) for hh in range(2)],
            axis=0)
        s_loc = lax.dot_general(q2, kb_scr[pl.ds(k0, win), :], _NT, preferred_element_type=_F32) + bias
        s_ctx = lax.dot_general(q2, kcb_scr[...], _NT, preferred_element_type=_F32)
        m = jnp.maximum(jnp.max(s_loc, axis=-1, keepdims=True), jnp.max(s_ctx, axis=-1, keepdims=True))
        p_loc = jnp.exp(s_loc - m).astype(_BF16)
        p_ctx = jnp.exp(s_ctx - m).astype(_BF16)
        oe = _dot(p_loc, vb_scr[pl.ds(k0, win), :]) + _dot(p_ctx, vcb_scr[...])
        on = oe[:, :hp] / oe[:, hp:]
        o = jnp.where(head0, on[:GRID_W], on[GRID_W:])
        o_ref[pl.ds(q0, GRID_W), :] = (o * _silu(ga_ref[pl.ds(q0, GRID_W), :])).astype(_BF16)
        return carry

    lax.fori_loop(0, rows, row, 0, unroll=8)


def _na_call(p3, cache_k4, cache_v4, bias_tbl, layer):
    b, l, _ = p3.shape
    rows = l // GRID_W
    assert rows >= NA_KH and rows % 2 == 0
    hp = 2 * NA_HEAD_DIM
    nblk = N_MAIN // hp // 16
    lc = cache_k4.shape[2]

    def spec(col):
        return pl.BlockSpec((None, l, hp), lambda i, j: (i, 0, col * nblk + j))

    cspec = pl.BlockSpec((None, None, lc, hp), lambda i, j: (i, layer, 0, j))
    return pl.pallas_call(
        functools.partial(_na_body, rows=rows),
        grid=(b, NA_HEADS // 2),
        in_specs=[spec(COL_Q), spec(COL_K), spec(COL_V), spec(COL_GA), cspec, cspec,
                  pl.BlockSpec((2, 2 * NA_KH - 2, GRID_W, 2 * GRID_W), lambda i, j: (j, 0, 0, 0))],
        out_specs=pl.BlockSpec((None, l, hp), lambda i, j: (i, 0, j)),
        out_shape=jax.ShapeDtypeStruct((b, l, NA_W), _BF16),
        scratch_shapes=[pltpu.VMEM((l, hp), _BF16), pltpu.VMEM((l, 2 * hp), _BF16),
                        pltpu.VMEM((lc, hp), _BF16), pltpu.VMEM((lc, 2 * hp), _BF16)],
        compiler_params=_params("arbitrary", "arbitrary"),
        name="na_attn",
    )(p3, p3, p3, p3, cache_k4, cache_v4, bias_tbl)


def _split3(x):
    hi = x.astype(_BF16)
    r1 = x - hi.astype(_F32)
    mid = r1.astype(_BF16)
    lo = (r1 - mid.astype(_F32)).astype(_BF16)
    return hi, mid, lo


def _ssd_body(*refs, reverse, has_h0, nc):
    it = iter(refs)
    x_ref, prev_ref, next_ref, dt_ref = next(it), next(it), next(it), next(it)
    cw_ref, cb_ref, dtb_row_ref, alog_row_ref, dtb_col_ref, alog_col_ref = (next(it) for _ in range(6))
    h0_ref = next(it) if has_h0 else None
    if reverse:
        yf_ref, z_ref, dskip_ref, gssm_ref = next(it), next(it), next(it), next(it)
    y_out_ref, hout_ref = next(it), next(it)
    h_scr, y_scr = next(it), next(it)

    q = SSM_CHUNK
    c = pl.program_id(1)
    cc = (nc - 1 - c) if reverse else c
    d = 1 if reverse else 0
    last = 0 if reverse else q - 1

    @pl.when(c == 0)
    def _():
        if has_h0:
            h_scr[...] = h0_ref[...]
        else:
            h_scr[...] = jnp.zeros_like(h_scr)

    x = x_ref[...]
    xp = jnp.where(cc > 0, prev_ref[7:8, :], 0.0)
    xn = jnp.where(cc < nc - 1, next_ref[0:1, :], 0.0)
    ridx = lax.broadcasted_iota(jnp.int32, (q, 1), 0)
    x_m1 = jnp.where(ridx == 0, xp, pltpu.roll(x, 1, axis=0))
    x_p1 = jnp.where(ridx == q - 1, xn, pltpu.roll(x, q - 1, axis=0))
    act = _silu(cw_ref[0:1, :] * x_m1 + cw_ref[1:2, :] * x + cw_ref[2:3, :] * x_p1 + cb_ref[...])
    xs = act[:, :SSM_W]
    bm = act[:, SSM_W:SSM_W + SSM_GROUPS * SSM_STATE].astype(_BF16)
    cm = act[:, SSM_W + SSM_GROUPS * SSM_STATE:].astype(_BF16)

    dtraw = dt_ref[...]
    dt_col = _softplus(dtraw + dtb_row_ref[...])
    dta_col = dt_col * (-jnp.exp(alog_row_ref[...]))
    dt_row_all = _softplus(dtraw.T + dtb_col_ref[...])
    dta_row_all = dt_row_all * (-jnp.exp(alog_col_ref[...]))
    dt_row = dt_row_all[d * SSM_HEADS:(d + 1) * SSM_HEADS, :]
    dta_row = dta_row_all[d * SSM_HEADS:(d + 1) * SSM_HEADS, :]

    ri = lax.broadcasted_iota(jnp.int32, (q, q), 0)
    ci = lax.broadcasted_iota(jnp.int32, (q, q), 1)
    causal = (ci >= ri) if reverse else (ci <= ri)
    t_col = jnp.where(causal, 1.0, 0.0).astype(_BF16)
    t_row = jnp.where((ri >= ci) if reverse else (ri <= ci), 1.0, 0.0).astype(_BF16)
    acum_col = sum(_dot(t_col, part) for part in _split3(dta_col))
    acum_row = sum(_dot(part, t_row) for part in _split3(dta_row))

    ea_col = jnp.exp(acum_col)
    de_row = jnp.exp(acum_row[:, last:last + 1] - acum_row)
    dtde_row = dt_row * de_row
    cdecay = jnp.exp(acum_row[:, last:last + 1])
    xs_t = xs.T
    xs_b = xs.astype(_BF16)

    hg = SSM_HEADS // SSM_GROUPS
    for g in range(SSM_GROUPS):
        nsl = slice(g * SSM_STATE, (g + 1) * SSM_STATE)
        b_g = bm[:, nsl]
        c_g = cm[:, nsl]
        cb = lax.dot_general(c_g, b_g, _NT, preferred_element_type=_F32)
        for hh in range(hg):
            h = g * hg + hh
            psl = slice(h * SSM_HEADDIM, (h + 1) * SSM_HEADDIM)
            lane = d * SSM_HEADS + h
            seg = acum_col[:, lane:lane + 1] - acum_row[h:h + 1, :]
            lmat = jnp.exp(jnp.where(causal, seg, NEG))
            m = (cb * lmat * dt_row[h:h + 1, :]).astype(_BF16)
            y_diag = _dot(m, xs_b[:, psl])
            h_prev = h_scr[h]
            y_off = ea_col[:, lane:lane + 1] * lax.dot_general(
                c_g, h_prev.astype(_BF16), _NT, preferred_element_type=_F32)
            y_scr[:, psl] = y_diag + y_off
            st = _dot((xs_t[psl, :] * dtde_row[h:h + 1, :]).astype(_BF16), b_g)
            h_scr[h] = h_prev * cdecay[h:h + 1, :] + st

    if reverse:
        y = yf_ref[...] + y_scr[...] + xs * dskip_ref[...]
        y = y * _silu(z_ref[...])
        y = y * lax.rsqrt(jnp.mean(y * y, axis=-1, keepdims=True) + EPS) * gssm_ref[...]
        y_out_ref[...] = y.astype(_BF16)
    else:
        y_out_ref[...] = y_scr[...]

    @pl.when(c == nc - 1)
    def _():
        hout_ref[...] = h_scr[...]


def _ssd_call(p3, dt3, prm, h0, layer, reverse, y_fwd=None):
    b, l, _ = p3.shape
    q = SSM_CHUNK
    nc = l // q
    has_h0 = h0 is not None
    xbc_blk = COL_XBC // 2

    def cidx(c):
        return (nc - 1 - c) if reverse else c

    in_specs = [
        pl.BlockSpec((None, q, CONV_CH), lambda i, c: (i, cidx(c), xbc_blk)),
        pl.BlockSpec((None, 8, CONV_CH), lambda i, c: (i, jnp.maximum(cidx(c) * (q // 8) - 1, 0), xbc_blk)),
        pl.BlockSpec((None, 8, CONV_CH), lambda i, c: (i, jnp.minimum((cidx(c) + 1) * (q // 8), l // 8 - 1), xbc_blk)),
        pl.BlockSpec((None, q, DT_PAD), lambda i, c: (i, cidx(c), 0)),
        pl.BlockSpec((3, CONV_CH), lambda i, c: (0, 0)),
        pl.BlockSpec((1, CONV_CH), lambda i, c: (0, 0)),
        pl.BlockSpec((1, DT_PAD), lambda i, c: (0, 0)),
        pl.BlockSpec((1, DT_PAD), lambda i, c: (0, 0)),
        pl.BlockSpec((DT_PAD, 1), lambda i, c: (0, 0)),
        pl.BlockSpec((DT_PAD, 1), lambda i, c: (0, 0)),
    ]
    args = [p3, p3, p3, dt3, prm["conv_wt"], prm["conv_b"], prm["dtb_row"], prm["alog_row"],
            prm["dtb_col"], prm["alog_col"]]
    if has_h0:
        in_specs.append(pl.BlockSpec((None, None, SSM_HEADS, SSM_HEADDIM, SSM_STATE),
                                     lambda i, c: (i, layer, 0, 0, 0)))
        args.append(h0)
    if reverse:
        in_specs += [
            pl.BlockSpec((None, q, SSM_W), lambda i, c: (i, cidx(c), 0)),
            pl.BlockSpec((None, q, SSM_W), lambda i, c: (i, cidx(c), COL_Z)),
            pl.BlockSpec((1, SSM_W), lambda i, c: (0, 0)),
            pl.BlockSpec((1, SSM_W), lambda i, c: (0, 0)),
        ]
        args += [y_fwd, p3, prm["dskip"], prm["g_ssm"]]
    y_dtype = _BF16 if reverse else _F32
    return pl.pallas_call(
        functools.partial(_ssd_body, reverse=reverse, has_h0=has_h0, nc=nc),
        grid=(b, nc),
        in_specs=in_specs,
        out_specs=[pl.BlockSpec((None, q, SSM_W), lambda i, c: (i, cidx(c), 0)),
                   pl.BlockSpec((None, SSM_HEADS, SSM_HEADDIM, SSM_STATE), lambda i, c: (i, 0, 0, 0))],
        out_shape=[jax.ShapeDtypeStruct((b, l, SSM_W), y_dtype),
                   jax.ShapeDtypeStruct((b, SSM_HEADS, SSM_HEADDIM, SSM_STATE), _F32)],
        scratch_shapes=[pltpu.VMEM((SSM_HEADS, SSM_HEADDIM, SSM_STATE), _F32),
                        pltpu.VMEM((q, SSM_W), _F32)],
        compiler_params=_params("arbitrary", "arbitrary"),
        name="ssd_bwd" if reverse else "ssd_fwd",
    )(*args)


def _sgu_body(u_ref, v_ref, g_ref, gs_ref, ws_ref, bs_ref, o_ref, *, chunks):
    v = v_ref[...]
    mu = jnp.mean(v, axis=-1, keepdims=True)
    vc = v - mu
    var = jnp.mean(vc * vc, axis=-1, keepdims=True)
    vn = (vc * lax.rsqrt(var + EPS) * gs_ref[...]).astype(_BF16)
    ge = SGU_W // SGU_GROUPS
    for ch in range(chunks):
        rsl = slice(ch * SGU_CHUNK, (ch + 1) * SGU_CHUNK)
        for g in range(SGU_GROUPS):
            csl = slice(g * ge, (g + 1) * ge)
            vs = _dot(ws_ref[g], vn[rsl, csl]) + bs_ref[:, g:g + 1]
            y = u_ref[rsl, csl] * vs * _silu(g_ref[rsl, csl])
            o_ref[rsl, csl] = y.astype(_BF16)


def _sgu_call(p2, g_sgu, w_s, b_s_t):
    t = p2.shape[0]
    chunks = 4
    tr = chunks * SGU_CHUNK
    nblk = N_MAIN // SGU_W // 16

    def spec(col):
        return pl.BlockSpec((tr, SGU_W), lambda i: (i, col * nblk))

    return pl.pallas_call(
        functools.partial(_sgu_body, chunks=chunks),
        grid=(t // tr,),
        in_specs=[spec(COL_U), spec(COL_VC), spec(COL_GC),
                  pl.BlockSpec((1, SGU_W), lambda i: (0, 0)),
                  pl.BlockSpec((SGU_GROUPS, SGU_CHUNK, SGU_CHUNK), lambda i: (0, 0, 0)),
                  pl.BlockSpec((SGU_CHUNK, SGU_GROUPS), lambda i: (0, 0))],
        out_specs=pl.BlockSpec((tr, SGU_W), lambda i: (i, 0)),
        out_shape=jax.ShapeDtypeStruct((t, SGU_W), _BF16),
        compiler_params=_params("arbitrary"),
        name="sgu",
    )(p2, p2, p2, g_sgu, w_s, b_s_t)


def _merge_body(oa_ref, ob_ref, oc_ref, ga_ref, gb_ref, gc_ref, x_ref, mod_ref, gpost_ref,
                wa_ref, wb_ref, wc_ref, wo_ref, out_ref):
    merged = _sigmoid(ga_ref[...]) * _dot(oa_ref[...], wa_ref[...])
    merged = merged + _sigmoid(gb_ref[...]) * _dot(ob_ref[...], wb_ref[...])
    merged = merged + _sigmoid(gc_ref[...]) * _dot(oc_ref[...], wc_ref[...])
    y = _dot(merged.astype(_BF16), wo_ref[...])
    y = y * lax.rsqrt(jnp.mean(y * y, axis=-1, keepdims=True) + EPS) * gpost_ref[...]
    out_ref[...] = x_ref[...] + mod_ref[:, 2 * D_MODEL:] * y


def _merge_call(o_a, o_b, o_c, p2, x2d, mod_l, g_post_l, wa, wb, wc, wo, mod_row):
    t = x2d.shape[0]
    tm = 256
    gm_blk = COL_GM // 2
    once = pl.Buffered(1)

    def row_spec(w):
        return pl.BlockSpec((tm, w), lambda i: (i, 0))

    def gm_spec(k):
        return pl.BlockSpec((tm, D_MODEL), lambda i: (i, gm_blk + k))

    def w_spec(kdim):
        return pl.BlockSpec((kdim, D_MODEL), lambda i: (0, 0), pipeline_mode=once)

    return pl.pallas_call(
        _merge_body,
        grid=(t // tm,),
        in_specs=[row_spec(NA_W), row_spec(SSM_W), row_spec(SGU_W), gm_spec(0), gm_spec(1), gm_spec(2),
                  row_spec(D_MODEL),
                  pl.BlockSpec((None, 1, 3 * D_MODEL), lambda i: (mod_row(i, tm), 0, 0)),
                  pl.BlockSpec((1, D_MODEL), lambda i: (0, 0)),
                  w_spec(NA_W), w_spec(SSM_W), w_spec(SGU_W), w_spec(D_MODEL)],
        out_specs=row_spec(D_MODEL),
        out_shape=jax.ShapeDtypeStruct((t, D_MODEL), _F32),
        compiler_params=_params("arbitrary"),
        name="merge_out",
    )(o_a, o_b, o_c, p2, p2, p2, x2d, mod_l, g_post_l, wa, wb, wc, wo)


def _layer(x3, mod_l, wts, mod_row, ctx, layer, caches=None):
    b, l, _ = x3.shape
    t = b * l
    x2d = x3.reshape(t, D_MODEL)
    p2, dt2 = _inproj_call(x2d, mod_l, wts["g_pre"], wts["w_main"], layer, wts["w_dt"], mod_row)
    p3 = p2.reshape(b, l, N_MAIN)
    dt3 = dt2.reshape(b, l, DT_PAD)
    if ctx is None:
        o_a, caches = _ctx_attn_call(p3, layer, caches)
        h0_f = h0_b = None
    else:
        cache_k4, cache_v4, bias_tbl, h0_f, h0_b = ctx
        o_a = _na_call(p3, cache_k4, cache_v4, bias_tbl, layer)
    y_f, h_f = _ssd_call(p3, dt3, wts, h0_f, layer, reverse=False)
    o_b, h_b = _ssd_call(p3, dt3, wts, h0_b, layer, reverse=True, y_fwd=y_f)
    o_c = _sgu_call(p2, wts["g_sgu"], wts["w_s"], wts["b_s_t"])
    y2d = _merge_call(o_a.reshape(t, NA_W), o_b.reshape(t, SSM_W), o_c, p2, x2d, mod_l, wts["g_post"],
                      wts["w_br_a"], wts["w_br_b"], wts["w_br_c"], wts["w_out"], mod_row)
    return y2d.reshape(b, l, D_MODEL), (caches, h_f, h_b)


def _layer_weights(l, g_pre, g_post, w_in, w_main, conv_w, conv_b, dt_bias, a_log, d_skip, g_ssm, w_s, b_s, g_sgu,
                   w_br_a, w_br_b, w_br_c, w_out):
    w_dt = jnp.pad(w_in[l, :, OFF_DT:OFF_DT + N_DT], ((0, 0), (0, DT_PAD - N_DT))).astype(_BF16)
    dtb = jnp.pad(dt_bias[l].reshape(N_DT).astype(_F32), (0, DT_PAD - N_DT))
    alog = jnp.pad(a_log[l].reshape(N_DT).astype(_F32), (0, DT_PAD - N_DT))
    return {
        "g_pre": g_pre[l].reshape(1, D_MODEL), "g_post": g_post[l].reshape(1, D_MODEL),
        "w_main": w_main, "w_dt": w_dt,
        "conv_wt": conv_w[l].T, "conv_b": conv_b[l].reshape(1, CONV_CH),
        "dtb_row": dtb.reshape(1, DT_PAD), "alog_row": alog.reshape(1, DT_PAD),
        "dtb_col": dtb.reshape(DT_PAD, 1), "alog_col": alog.reshape(DT_PAD, 1),
        "dskip": jnp.repeat(d_skip[l].astype(_F32), SSM_HEADDIM).reshape(1, SSM_W),
        "g_ssm": g_ssm[l].reshape(1, SSM_W),
        "g_sgu": g_sgu[l].reshape(1, SGU_W), "w_s": w_s[l].astype(_BF16), "b_s_t": b_s[l].T,
        "w_br_a": w_br_a[l].astype(_BF16), "w_br_b": w_br_b[l].astype(_BF16),
        "w_br_c": w_br_c[l].astype(_BF16), "w_out": w_out[l].astype(_BF16),
    }


def kernel(x_prompt, x_sample, c, cache_k, cache_v, state_ssm_fwd, state_ssm_bwd, c_ctx, w_mod, b_mod, g_pre,
           g_post, w_in, rpb, conv_w, conv_b, dt_bias, a_log, d_skip, g_ssm, w_s, b_s, g_sgu, w_br_a, w_br_b,
           w_br_c, w_out):
    nb, ls, _ = x_sample.shape
    assert 1 + nb <= MOD_ROWS
    cvecs = jnp.concatenate([c_ctx[None, :], c, jnp.zeros((MOD_ROWS - 1 - nb, D_MODEL), _F32)], axis=0)
    mod = _mod_call(cvecs, w_mod, b_mod).reshape(DEPTH, MOD_ROWS, 1, 3 * D_MODEL)
    past = cache_k.shape[2]
    cache_k4 = cache_k.reshape(nb, DEPTH, past, NA_W)
    cache_v4 = cache_v.reshape(nb, DEPTH, past, NA_W)

    def prompt_row(i, tm):
        return 0

    def sample_row(i, tm):
        return 1 + (i * tm) // ls

    y_p, y_s = x_prompt, x_sample
    caches = None
    hf_l, hb_l = [], []
    w_main = _repack_call(w_in)
    for l in range(DEPTH):
        wts = _layer_weights(l, g_pre, g_post, w_in, w_main, conv_w, conv_b, dt_bias, a_log, d_skip, g_ssm, w_s,
                             b_s, g_sgu, w_br_a, w_br_b, w_br_c, w_out)
        y_p, (caches, h_f, h_b) = _layer(y_p, mod[l], wts, prompt_row, None, l, caches)
        hf_l.append(h_f)
        hb_l.append(h_b)
        ctx = (cache_k4, cache_v4, _na_bias_table(rpb[l]), state_ssm_fwd, state_ssm_bwd)
        y_s, _ = _layer(y_s, mod[l], wts, sample_row, ctx, l)
    bp, lp, _ = x_prompt.shape
    new_k = caches[0].reshape(bp, DEPTH, lp, NA_HEADS, NA_HEAD_DIM)
    new_v = caches[1].reshape(bp, DEPTH, lp, NA_HEADS, NA_HEAD_DIM)
    return (y_p, y_s, new_k, new_v, jnp.stack(hf_l, axis=1), jnp.stack(hb_l, axis=1))
```

```python
import functools

import jax
import jax.numpy as jnp
from jax import lax
from jax.experimental import pallas as pl
from jax.experimental.pallas import tpu as pltpu

D_MODEL = 2048
DEPTH = 2
EPS = 1e-6
GRID_W = 64
NA_HEAD_DIM = 64
NA_W = D_MODEL // 2
NA_HEADS = NA_W // NA_HEAD_DIM
NA_KH = 8
NA_KW = 16
SSM_HEADDIM = 64
SSM_W = D_MODEL // 2
SSM_HEADS = SSM_W // SSM_HEADDIM
SSM_GROUPS = 4
SSM_STATE = 128
SSM_CHUNK = 128
CONV_CH = SSM_W + 2 * SSM_GROUPS * SSM_STATE
SGU_W = D_MODEL // 2
SGU_GROUPS = 8
SGU_CHUNK = 128

N_MAIN = 16 * 1024
DT_PAD = 128
COL_Q, COL_K, COL_V, COL_GA, COL_XBC, COL_Z, COL_U, COL_VC, COL_GC, COL_GM = 0, 1, 2, 3, 4, 6, 7, 8, 9, 10

NEG = -1e30
MOD_ROWS = 8
VMEM_LIMIT = 56 * 1024 * 1024

_F32 = jnp.float32
_BF16 = jnp.bfloat16
_NT = (((1,), (1,)), ((), ()))


def _sigmoid(x):
    return 1.0 / (1.0 + jnp.exp(-x))


def _silu(x):
    return x * _sigmoid(x)


def _softplus(x):
    return jnp.maximum(x, 0.0) + jnp.log(1.0 + jnp.exp(-jnp.abs(x)))


def _dot(a, b):
    return jnp.dot(a, b, preferred_element_type=_F32)


def _params(*sem):
    return pltpu.CompilerParams(dimension_semantics=sem, vmem_limit_bytes=VMEM_LIMIT)


def _mod_body(c_ref, w_ref, b_ref, o_ref):
    s = _silu(c_ref[...]).astype(_BF16)
    o_ref[...] = _dot(s, w_ref[...].astype(_BF16)) + b_ref[...]


def _mod_call(cvecs, w_mod, b_mod):
    tn = 1024
    n3 = 3 * D_MODEL
    return pl.pallas_call(
        _mod_body,
        grid=(DEPTH, n3 // tn),
        in_specs=[pl.BlockSpec((MOD_ROWS, D_MODEL), lambda l, j: (0, 0)),
                  pl.BlockSpec((None, D_MODEL, tn), lambda l, j: (l, 0, j)),
                  pl.BlockSpec((None, 1, tn), lambda l, j: (l, 0, j))],
        out_specs=pl.BlockSpec((None, MOD_ROWS, tn), lambda l, j: (l, 0, j)),
        out_shape=jax.ShapeDtypeStruct((DEPTH, MOD_ROWS, n3), _F32),
        compiler_params=_params("arbitrary", "arbitrary"),
        name="modulation",
    )(cvecs, w_mod, b_mod.reshape(DEPTH, 1, n3))


N_DT = 2 * SSM_HEADS
OFF_DT = 4 * NA_W + CONV_CH + SSM_W
REPACK_BLK = 512


def _repack_body(a_ref, b_ref, o_ref, odt_ref):
    r = pl.program_id(1)
    cut = OFF_DT // REPACK_BLK

    @pl.when(r < cut)
    def _():
        o_ref[...] = a_ref[...].T.astype(_BF16)

    @pl.when(r >= cut)
    def _():
        o_ref[...] = jnp.concatenate([a_ref[N_DT:, :], b_ref[...]], axis=0).T.astype(_BF16)

    @pl.when(r == cut)
    def _():
        dt_rows = jnp.concatenate([a_ref[:N_DT, :], jnp.zeros((DT_PAD - N_DT, D_MODEL), _F32)], axis=0)
        odt_ref[...] = dt_rows.T.astype(_BF16)


def _repack_call(w_in_t):
    return pl.pallas_call(
        _repack_body,
        grid=(DEPTH, N_MAIN // REPACK_BLK),
        in_specs=[pl.BlockSpec((None, REPACK_BLK, D_MODEL), lambda l, r: (l, r, 0)),
                  pl.BlockSpec((None, N_DT, D_MODEL), lambda l, r: (l, (r + 1) * (REPACK_BLK // N_DT), 0))],
        out_specs=[pl.BlockSpec((None, D_MODEL, REPACK_BLK), lambda l, r: (l, 0, r)),
                   pl.BlockSpec((None, D_MODEL, DT_PAD), lambda l, r: (l, 0, 0))],
        out_shape=[jax.ShapeDtypeStruct((DEPTH, D_MODEL, N_MAIN), _BF16),
                   jax.ShapeDtypeStruct((DEPTH, D_MODEL, DT_PAD), _BF16)],
        compiler_params=_params("arbitrary", "arbitrary"),
        name="repack_w_in",
    )(w_in_t, w_in_t)


def _inproj_body(x_ref, mod_ref, g_ref, w_ref, wdt_ref, o_ref, dt_ref, h_scr):
    @pl.when(pl.program_id(1) == 0)
    def _():
        x = x_ref[...]
        xn = x * lax.rsqrt(jnp.mean(x * x, axis=-1, keepdims=True) + EPS) * g_ref[...]
        shift = mod_ref[:, 0:D_MODEL]
        scale = mod_ref[:, D_MODEL:2 * D_MODEL]
        h = (xn * (1.0 + scale) + shift).astype(_BF16)
        h_scr[...] = h
        dt_ref[...] = _dot(h, wdt_ref[...])

    o_ref[...] = _dot(h_scr[...], w_ref[...])


def _inproj_call(x2d, mod_l, g_pre_l, w_main, layer, w_dt, mod_row):
    t = x2d.shape[0]
    tm, tn = 1024, 1024
    return pl.pallas_call(
        _inproj_body,
        grid=(t // tm, N_MAIN // tn),
        in_specs=[pl.BlockSpec((tm, D_MODEL), lambda i, j: (i, 0)),
                  pl.BlockSpec((None, 1, 3 * D_MODEL), lambda i, j: (mod_row(i, tm), 0, 0)),
                  pl.BlockSpec((1, D_MODEL), lambda i, j: (0, 0)),
                  pl.BlockSpec((None, D_MODEL, tn), lambda i, j: (layer, 0, j)),
                  pl.BlockSpec((None, D_MODEL, DT_PAD), lambda i, j: (layer, 0, 0))],
        out_specs=[pl.BlockSpec((tm, tn), lambda i, j: (i, j)),
                   pl.BlockSpec((tm, DT_PAD), lambda i, j: (i, 0))],
        out_shape=[jax.ShapeDtypeStruct((t, N_MAIN), _F32),
                   jax.ShapeDtypeStruct((t, DT_PAD), _F32)],
        scratch_shapes=[pltpu.VMEM((tm, D_MODEL), _BF16)],
        compiler_params=_params("arbitrary", "arbitrary"),
        name="inproj",
    )(x2d, mod_l, g_pre_l, w_main, w_dt)


def _ctx_attn_body(*refs):
    q_ref, k_ref, v_ref, ga_ref = refs[:4]
    o_ref, ko_ref, vo_ref = refs[-3:]
    l, hp = q_ref.shape
    k = k_ref[...]
    v = v_ref[...]
    ko_ref[...] = k
    vo_ref[...] = v
    head0 = lax.broadcasted_iota(jnp.int32, (l, hp), 1) < NA_HEAD_DIM
    q = q_ref[...] * (NA_HEAD_DIM ** -0.5)
    q2 = jnp.concatenate([jnp.where(head0, q, 0.0), jnp.where(head0, 0.0, q)], axis=0).astype(_BF16)
    s = lax.dot_general(q2, k.astype(_BF16), _NT, preferred_element_type=_F32)
    p = jnp.exp(s - jnp.max(s, axis=-1, keepdims=True)).astype(_BF16)
    v_ext = jnp.concatenate([v.astype(_BF16), jnp.ones((l, hp), _BF16)], axis=1)
    oe = _dot(p, v_ext)
    on = oe[:, :hp] / oe[:, hp:]
    o = jnp.where(head0, on[:l], on[l:])
    o_ref[...] = (o * _silu(ga_ref[...])).astype(_BF16)


def _ctx_attn_call(p3, layer, caches):
    b, l, _ = p3.shape
    hp = 2 * NA_HEAD_DIM
    nblk = N_MAIN // hp // 16

    def spec(col):
        return pl.BlockSpec((None, l, hp), lambda i, j: (i, 0, col * nblk + j))

    cache_spec = pl.BlockSpec((None, None, l, hp), lambda i, j: (i, layer, 0, j))
    cache_shape = jax.ShapeDtypeStruct((b, DEPTH, l, NA_W), _F32)
    in_specs = [spec(COL_Q), spec(COL_K), spec(COL_V), spec(COL_GA)]
    args = [p3, p3, p3, p3]
    aliases = {}
    if caches is not None:
        in_specs += [pl.BlockSpec(memory_space=pl.ANY)] * 2
        args += list(caches)
        aliases = {4: 1, 5: 2}
    o_a, k_all, v_all = pl.pallas_call(
        _ctx_attn_body,
        grid=(b, NA_HEADS // 2),
        in_specs=in_specs,
        out_specs=[pl.BlockSpec((None, l, hp), lambda i, j: (i, 0, j)), cache_spec, cache_spec],
        out_shape=[jax.ShapeDtypeStruct((b, l, NA_W), _BF16), cache_shape, cache_shape],
        input_output_aliases=aliases,
        compiler_params=_params("arbitrary", "arbitrary"),
        name="ctx_attn",
    )(*args)
    return o_a, (k_all, v_all)


def _na_bias_table(rpb_l):
    j = jnp.arange(GRID_W)[None, :, None]
    c = jnp.arange(GRID_W)[None, None, :]
    k = jnp.arange(2 * NA_KW - 1)[:, None, None]
    cs = jnp.clip(j - NA_KW // 2, 0, GRID_W - NA_KW)
    valid = (c >= cs) & (c < cs + NA_KW)
    sel = (valid & (c - j + (NA_KW - 1) == k)).astype(_F32)
    tbl = jnp.einsum("hdk,kjc->hdjc", rpb_l.astype(_F32), sel, precision=lax.Precision.HIGHEST)
    tbl = tbl + jnp.where(valid, 0.0, NEG)[None]
    return jnp.concatenate([tbl[:, :-1], tbl[:, 1:]], axis=-1)


def _na_body(q_ref, k_ref, v_ref, ga_ref, kc_ref, vc_ref, bias_ref, o_ref, kb_scr, vb_scr, kcb_scr, vcb_scr,
             *, rows):
    hp = 2 * NA_HEAD_DIM
    win = NA_KH * GRID_W
    kb_scr[...] = k_ref[...].astype(_BF16)
    vb_scr[:, :hp] = v_ref[...].astype(_BF16)
    vb_scr[:, hp:] = jnp.ones((vb_scr.shape[0], hp), _BF16)
    kcb_scr[...] = kc_ref[...].astype(_BF16)
    vcb_scr[:, :hp] = vc_ref[...].astype(_BF16)
    vcb_scr[:, hp:] = jnp.ones((vcb_scr.shape[0], hp), _BF16)
    head0 = lax.broadcasted_iota(jnp.int32, (GRID_W, hp), 1) < NA_HEAD_DIM

    def row(r, carry):
        rs = jnp.clip(r - NA_KH // 2, 0, rows - NA_KH)
        dr0 = rs - r + (NA_KH - 1)
        q0 = pl.multiple_of(r * GRID_W, GRID_W)
        k0 = pl.multiple_of(rs * GRID_W, GRID_W)
        q = q_ref[pl.ds(q0, GRID_W), :] * (NA_HEAD_DIM ** -0.5)
        q2 = jnp.concatenate([jnp.where(head0, q, 0.0), jnp.where(head0, 0.0, q)], axis=0).astype(_BF16)
        bias = jnp.concatenate(
            [jnp.concatenate([bias_ref[hh, dr0 + a] for a in range(0, NA_KH, 2)], axis=1) for hh in range(2)],
            axis=0)
        s_loc = lax.dot_general(q2, kb_scr[pl.ds(k0, win), :], _NT, preferred_element_type=_F32) + bias
        s_ctx = lax.dot_general(q2, kcb_scr[...], _NT, preferred_element_type=_F32)
        m = jnp.maximum(jnp.max(s_loc, axis=-1, keepdims=True), jnp.max(s_ctx, axis=-1, keepdims=True))
        p_loc = jnp.exp(s_loc - m).astype(_BF16)
        p_ctx = jnp.exp(s_ctx - m).astype(_BF16)
        oe = _dot(p_loc, vb_scr[pl.ds(k0, win), :]) + _dot(p_ctx, vcb_scr[...])
        on = oe[:, :hp] / oe[:, hp:]
        o = jnp.where(head0, on[:GRID_W], on[GRID_W:])
        o_ref[pl.ds(q0, GRID_W), :] = (o * _silu(ga_ref[pl.ds(q0, GRID_W), :])).astype(_BF16)
        return carry

    lax.fori_loop(0, rows, row, 0, unroll=8)


def _na_call(p3, cache_k4, cache_v4, bias_tbl, layer):
    b, l, _ = p3.shape
    rows = l // GRID_W
    assert rows >= NA_KH and rows % 2 == 0
    hp = 2 * NA_HEAD_DIM
    nblk = N_MAIN // hp // 16
    lc = cache_k4.shape[2]

    def spec(col):
        return pl.BlockSpec((None, l, hp), lambda i, j: (i, 0, col * nblk + j))

    cspec = pl.BlockSpec((None, None, lc, hp), lambda i, j: (i, layer, 0, j))
    return pl.pallas_call(
        functools.partial(_na_body, rows=rows),
        grid=(b, NA_HEADS // 2),
        in_specs=[spec(COL_Q), spec(COL_K), spec(COL_V), spec(COL_GA), cspec, cspec,
                  pl.BlockSpec((2, 2 * NA_KH - 2, GRID_W, 2 * GRID_W), lambda i, j: (j, 0, 0, 0))],
        out_specs=pl.BlockSpec((None, l, hp), lambda i, j: (i, 0, j)),
        out_shape=jax.ShapeDtypeStruct((b, l, NA_W), _BF16),
        scratch_shapes=[pltpu.VMEM((l, hp), _BF16), pltpu.VMEM((l, 2 * hp), _BF16),
                        pltpu.VMEM((lc, hp), _BF16), pltpu.VMEM((lc, 2 * hp), _BF16)],
        compiler_params=_params("arbitrary", "arbitrary"),
        name="na_attn",
    )(p3, p3, p3, p3, cache_k4, cache_v4, bias_tbl)


HEADS_PER_GROUP = SSM_HEADS // SSM_GROUPS
GROUP_W = HEADS_PER_GROUP * SSM_HEADDIM
SSD_CHUNKS_PER_STEP = 2


def _split3(x):
    hi = x.astype(_BF16)
    r1 = x - hi.astype(_F32)
    mid = r1.astype(_BF16)
    lo = (r1 - mid.astype(_F32)).astype(_BF16)
    return hi, mid, lo


def _silu_tanh(x):
    h = 0.5 * x
    return h + h * jnp.tanh(h)


def _ssd_chunk(xs_b, bm, cm, dtraw, dtb_row_ref, alog_row_ref, dtb_col_ref, alog_col_ref, ht_scr, reverse):
    q = SSM_CHUNK
    d = 1 if reverse else 0
    last = 0 if reverse else q - 1

    dta_col = _softplus(dtraw + dtb_row_ref[...]) * (-jnp.exp(alog_row_ref[...]))
    dt_row_all = _softplus(dtraw.T + dtb_col_ref[...])
    dta_row_all = dt_row_all * (-jnp.exp(alog_col_ref[...]))
    dt_row = dt_row_all[d * SSM_HEADS:(d + 1) * SSM_HEADS, :]
    dta_row = dta_row_all[d * SSM_HEADS:(d + 1) * SSM_HEADS, :]

    ri = lax.broadcasted_iota(jnp.int32, (q, q), 0)
    ci = lax.broadcasted_iota(jnp.int32, (q, q), 1)
    causal = (ci >= ri) if reverse else (ci <= ri)
    t_col = jnp.where(causal, 1.0, 0.0).astype(_BF16)
    t_row = jnp.where((ri >= ci) if reverse else (ri <= ci), 1.0, 0.0).astype(_BF16)
    acum_col = sum(_dot(t_col, part) for part in _split3(dta_col))
    acum_row = sum(_dot(part, t_row) for part in _split3(dta_row))

    a_last = jnp.broadcast_to(acum_row[:, last:last + 1], (SSM_HEADS, q))
    dtde_row = dt_row * jnp.exp(a_last - acum_row)
    cdecay = jnp.exp(a_last)

    lo_half = lax.broadcasted_iota(jnp.int32, (1, q), 1) < SSM_HEADDIM
    lane_head = lax.broadcasted_iota(jnp.int32, (1, GROUP_W), 1) // SSM_HEADDIM
    zero_b = jnp.zeros((q, GROUP_W), _BF16)

    def per_head_lanes(vals):
        return jnp.concatenate([jnp.where(lo_half, vals[0], vals[1]), jnp.where(lo_half, vals[2], vals[3])], axis=1)

    ys = []
    for g in range(SSM_GROUPS):
        nsl = slice(g * SSM_STATE, (g + 1) * SSM_STATE)
        b_g = bm[:, nsl]
        c_g = cm[:, nsl]
        cb = lax.dot_general(c_g, b_g, _NT, preferred_element_type=_F32)
        b_t = b_g.astype(_F32).T
        xs_g = xs_b[:, g * GROUP_W:(g + 1) * GROUP_W]
        m_parts, bt_parts, xbd_parts, bcs = [], [], [], []
        for hh in range(HEADS_PER_GROUP):
            h = g * HEADS_PER_GROUP + hh
            lane = d * SSM_HEADS + h
            bc = jnp.broadcast_to(acum_col[:, lane:lane + 1], (q, q))
            lmat = jnp.exp(jnp.where(causal, bc - acum_row[h:h + 1, :], NEG))
            m_parts.append((cb * lmat * dt_row[h:h + 1, :]).astype(_BF16))
            bt_parts.append((b_t * dtde_row[h:h + 1, :]).astype(_BF16))
            xbd_parts.append(jnp.where(lane_head == hh, xs_g, zero_b))
            bcs.append(bc)
        lhs = jnp.concatenate([jnp.concatenate(m_parts, axis=1), jnp.concatenate(bt_parts, axis=1)], axis=0)
        res = _dot(lhs, jnp.concatenate(xbd_parts, axis=0))
        h_t = ht_scr[g]
        y_off = _dot(c_g, h_t.astype(_BF16)) * jnp.exp(per_head_lanes(bcs))
        ys.append(res[:q] + y_off)
        h0 = g * HEADS_PER_GROUP
        cd = per_head_lanes([cdecay[h0 + hh:h0 + hh + 1, :] for hh in range(HEADS_PER_GROUP)])
        ht_scr[g] = h_t * cd + res[q:]
    return ys


def _ssd_state_io(c, nc, h0_ref, hout_ref, ht_scr):
    @pl.when(c == 0)
    def _():
        for g in range(SSM_GROUPS):
            if h0_ref is None:
                ht_scr[g] = jnp.zeros((SSM_STATE, GROUP_W), _F32)
            else:
                hs = h0_ref[g * HEADS_PER_GROUP:(g + 1) * HEADS_PER_GROUP]
                ht_scr[g] = hs.reshape(GROUP_W, SSM_STATE).T

    def store_final():
        @pl.when(c == nc - 1)
        def _():
            for g in range(SSM_GROUPS):
                hout_ref[g * HEADS_PER_GROUP:(g + 1) * HEADS_PER_GROUP] = ht_scr[g].T.reshape(
                    HEADS_PER_GROUP, SSM_HEADDIM, SSM_STATE)

    return store_final


def _ssd_fwd_body(*refs, has_h0, nc):
    it = iter(refs)
    x_ref, prev_ref, next_ref, dt_ref, cw_ref, cb_ref = (next(it) for _ in range(6))
    dec_refs = [next(it) for _ in range(4)]
    dskip_ref = next(it)
    h0_ref = next(it) if has_h0 else None
    y_ref, act_ref, hout_ref, ht_scr = (next(it) for _ in range(4))
    q = SSM_CHUNK
    qb = x_ref.shape[0]
    c = pl.program_id(1)
    store_final = _ssd_state_io(c, nc, h0_ref, hout_ref, ht_scr)

    x = x_ref[...]
    xp = jnp.where(c > 0, prev_ref[7:8, :], 0.0)
    xn = jnp.where(c < nc - 1, next_ref[0:1, :], 0.0)
    sub = lax.broadcasted_iota(jnp.int32, (8, 1), 0)
    x_m1 = pltpu.roll(x, 1, axis=0)
    x_m1 = jnp.concatenate([jnp.where(sub == 0, xp, x_m1[:8]), x_m1[8:]], axis=0)
    x_p1 = pltpu.roll(x, qb - 1, axis=0)
    x_p1 = jnp.concatenate([x_p1[:qb - 8], jnp.where(sub == 7, xn, x_p1[qb - 8:])], axis=0)
    act = _silu_tanh(cw_ref[0:1, :] * x_m1 + cw_ref[1:2, :] * x + cw_ref[2:3, :] * x_p1 + cb_ref[...])
    act_b = act.astype(_BF16)
    act_ref[...] = act_b

    for s in range(qb // q):
        rows = slice(s * q, (s + 1) * q)
        ys = _ssd_chunk(act_b[rows, :SSM_W], act_b[rows, SSM_W:SSM_W + SSM_GROUPS * SSM_STATE],
                        act_b[rows, SSM_W + SSM_GROUPS * SSM_STATE:], dt_ref[rows, :], *dec_refs, ht_scr, False)
        y_ref[rows, :] = jnp.concatenate(ys, axis=1) + act[rows, :SSM_W] * dskip_ref[...]
    store_final()


def _ssd_bwd_body(*refs, has_h0, nc):
    it = iter(refs)
    act_ref, dt_ref = next(it), next(it)
    dec_refs = [next(it) for _ in range(4)]
    yf_ref, z_ref, gssm_ref = next(it), next(it), next(it)
    h0_ref = next(it) if has_h0 else None
    o_ref, hout_ref, ht_scr = next(it), next(it), next(it)
    c = pl.program_id(1)
    store_final = _ssd_state_io(c, nc, h0_ref, hout_ref, ht_scr)

    q = SSM_CHUNK
    for s in reversed(range(act_ref.shape[0] // q)):
        rows = slice(s * q, (s + 1) * q)
        ys = _ssd_chunk(act_ref[rows, :SSM_W], act_ref[rows, SSM_W:SSM_W + SSM_GROUPS * SSM_STATE],
                        act_ref[rows, SSM_W + SSM_GROUPS * SSM_STATE:], dt_ref[rows, :], *dec_refs, ht_scr, True)
        y = (yf_ref[rows, :] + jnp.concatenate(ys, axis=1)) * _silu_tanh(z_ref[rows, :])
        y = y * lax.rsqrt(jnp.mean(y * y, axis=-1, keepdims=True) + EPS) * gssm_ref[...]
        o_ref[rows, :] = y.astype(_BF16)
    store_final()


def _ssd_calls(p3, dt3, prm, h0_f, h0_b, layer):
    b, l, _ = p3.shape
    q = SSD_CHUNKS_PER_STEP * SSM_CHUNK
    nc = l // q
    has_h0 = h0_f is not None
    xbc_blk = COL_XBC // 2

    def vec(w):
        return pl.BlockSpec((1, w), lambda i, c: (0, 0))

    dec_specs = [vec(DT_PAD), vec(DT_PAD),
                 pl.BlockSpec((DT_PAD, 1), lambda i, c: (0, 0)), pl.BlockSpec((DT_PAD, 1), lambda i, c: (0, 0))]
    dec_args = [prm["dtb_row"], prm["alog_row"], prm["dtb_col"], prm["alog_col"]]
    h0_spec = pl.BlockSpec((None, None, SSM_HEADS, SSM_HEADDIM, SSM_STATE), lambda i, c: (i, layer, 0, 0, 0))
    state_spec = pl.BlockSpec((None, SSM_HEADS, SSM_HEADDIM, SSM_STATE), lambda i, c: (i, 0, 0, 0))
    state_shape = jax.ShapeDtypeStruct((b, SSM_HEADS, SSM_HEADDIM, SSM_STATE), _F32)
    ht_scratch = pltpu.VMEM((SSM_GROUPS, SSM_STATE, GROUP_W), _F32)

    in_specs = [
        pl.BlockSpec((None, q, CONV_CH), lambda i, c: (i, c, xbc_blk)),
        pl.BlockSpec((None, 8, CONV_CH), lambda i, c: (i, jnp.maximum(c * (q // 8) - 1, 0), xbc_blk)),
        pl.BlockSpec((None, 8, CONV_CH), lambda i, c: (i, jnp.minimum((c + 1) * (q // 8), l // 8 - 1), xbc_blk)),
        pl.BlockSpec((None, q, DT_PAD), lambda i, c: (i, c, 0)),
        pl.BlockSpec((3, CONV_CH), lambda i, c: (0, 0)), vec(CONV_CH)] + dec_specs + [vec(SSM_W)]
    args = [p3, p3, p3, dt3, prm["conv_wt"], prm["conv_b"]] + dec_args + [prm["dskip"]]
    if has_h0:
        in_specs.append(h0_spec)
        args.append(h0_f)
    y_f, act, h_f = pl.pallas_call(
        functools.partial(_ssd_fwd_body, has_h0=has_h0, nc=nc),
        grid=(b, nc),
        in_specs=in_specs,
        out_specs=[pl.BlockSpec((None, q, SSM_W), lambda i, c: (i, c, 0)),
                   pl.BlockSpec((None, q, CONV_CH), lambda i, c: (i, c, 0)), state_spec],
        out_shape=[jax.ShapeDtypeStruct((b, l, SSM_W), _F32),
                   jax.ShapeDtypeStruct((b, l, CONV_CH), _BF16), state_shape],
        scratch_shapes=[ht_scratch],
        compiler_params=_params("arbitrary", "arbitrary"),
        name="ssd_fwd",
    )(*args)

    def rc(c):
        return nc - 1 - c

    in_specs = [pl.BlockSpec((None, q, CONV_CH), lambda i, c: (i, rc(c), 0)),
                pl.BlockSpec((None, q, DT_PAD), lambda i, c: (i, rc(c), 0))] + dec_specs + [
        pl.BlockSpec((None, q, SSM_W), lambda i, c: (i, rc(c), 0)),
        pl.BlockSpec((None, q, SSM_W), lambda i, c: (i, rc(c), COL_Z)), vec(SSM_W)]
    args = [act, dt3] + dec_args + [y_f, p3, prm["g_ssm"]]
    if has_h0:
        in_specs.append(h0_spec)
        args.append(h0_b)
    o_b, h_b = pl.pallas_call(
        functools.partial(_ssd_bwd_body, has_h0=has_h0, nc=nc),
        grid=(b, nc),
        in_specs=in_specs,
        out_specs=[pl.BlockSpec((None, q, SSM_W), lambda i, c: (i, rc(c), 0)), state_spec],
        out_shape=[jax.ShapeDtypeStruct((b, l, SSM_W), _BF16), state_shape],
        scratch_shapes=[ht_scratch],
        compiler_params=_params("arbitrary", "arbitrary"),
        name="ssd_bwd",
    )(*args)
    return o_b, h_f, h_b


def _sgu_body(u_ref, v_ref, g_ref, gs_ref, ws_ref, bs_ref, o_ref, *, chunks):
    v = v_ref[...]
    mu = jnp.mean(v, axis=-1, keepdims=True)
    vc = v - mu
    var = jnp.mean(vc * vc, axis=-1, keepdims=True)
    vn = (vc * lax.rsqrt(var + EPS) * gs_ref[...]).astype(_BF16)
    ge = SGU_W // SGU_GROUPS
    for ch in range(chunks):
        rsl = slice(ch * SGU_CHUNK, (ch + 1) * SGU_CHUNK)
        for g in range(SGU_GROUPS):
            csl = slice(g * ge, (g + 1) * ge)
            vs = _dot(ws_ref[g], vn[rsl, csl]) + bs_ref[:, g:g + 1]
            y = u_ref[rsl, csl] * vs * _silu(g_ref[rsl, csl])
            o_ref[rsl, csl] = y.astype(_BF16)


def _sgu_call(p2, g_sgu, w_s, b_s_t):
    t = p2.shape[0]
    chunks = 4
    tr = chunks * SGU_CHUNK
    nblk = N_MAIN // SGU_W // 16

    def spec(col):
        return pl.BlockSpec((tr, SGU_W), lambda i: (i, col * nblk))

    return pl.pallas_call(
        functools.partial(_sgu_body, chunks=chunks),
        grid=(t // tr,),
        in_specs=[spec(COL_U), spec(COL_VC), spec(COL_GC),
                  pl.BlockSpec((1, SGU_W), lambda i: (0, 0)),
                  pl.BlockSpec((SGU_GROUPS, SGU_CHUNK, SGU_CHUNK), lambda i: (0, 0, 0)),
                  pl.BlockSpec((SGU_CHUNK, SGU_GROUPS), lambda i: (0, 0))],
        out_specs=pl.BlockSpec((tr, SGU_W), lambda i: (i, 0)),
        out_shape=jax.ShapeDtypeStruct((t, SGU_W), _BF16),
        compiler_params=_params("arbitrary"),
        name="sgu",
    )(p2, p2, p2, g_sgu, w_s, b_s_t)


def _merge_body(oa_ref, ob_ref, oc_ref, ga_ref, gb_ref, gc_ref, x_ref, mod_ref, gpost_ref,
                wa_ref, wb_ref, wc_ref, wo_ref, out_ref):
    merged = _sigmoid(ga_ref[...]) * _dot(oa_ref[...], wa_ref[...])
    merged = merged + _sigmoid(gb_ref[...]) * _dot(ob_ref[...], wb_ref[...])
    merged = merged + _sigmoid(gc_ref[...]) * _dot(oc_ref[...], wc_ref[...])
    y = _dot(merged.astype(_BF16), wo_ref[...])
    y = y * lax.rsqrt(jnp.mean(y * y, axis=-1, keepdims=True) + EPS) * gpost_ref[...]
    out_ref[...] = x_ref[...] + mod_ref[:, 2 * D_MODEL:] * y


def _merge_call(o_a, o_b, o_c, p2, x2d, mod_l, g_post_l, wa, wb, wc, wo, mod_row):
    t = x2d.shape[0]
    tm = 256
    gm_blk = COL_GM // 2
    once = pl.Buffered(1)

    def row_spec(w):
        return pl.BlockSpec((tm, w), lambda i: (i, 0))

    def gm_spec(k):
        return pl.BlockSpec((tm, D_MODEL), lambda i: (i, gm_blk + k))

    def w_spec(kdim):
        return pl.BlockSpec((kdim, D_MODEL), lambda i: (0, 0), pipeline_mode=once)

    return pl.pallas_call(
        _merge_body,
        grid=(t // tm,),
        in_specs=[row_spec(NA_W), row_spec(SSM_W), row_spec(SGU_W), gm_spec(0), gm_spec(1), gm_spec(2),
                  row_spec(D_MODEL),
                  pl.BlockSpec((None, 1, 3 * D_MODEL), lambda i: (mod_row(i, tm), 0, 0)),
                  pl.BlockSpec((1, D_MODEL), lambda i: (0, 0)),
                  w_spec(NA_W), w_spec(SSM_W), w_spec(SGU_W), w_spec(D_MODEL)],
        out_specs=row_spec(D_MODEL),
        out_shape=jax.ShapeDtypeStruct((t, D_MODEL), _F32),
        compiler_params=_params("arbitrary"),
        name="merge_out",
    )(o_a, o_b, o_c, p2, p2, p2, x2d, mod_l, g_post_l, wa, wb, wc, wo)


def _layer(x3, mod_l, wts, mod_row, ctx, layer, caches=None):
    b, l, _ = x3.shape
    t = b * l
    x2d = x3.reshape(t, D_MODEL)
    p2, dt2 = _inproj_call(x2d, mod_l, wts["g_pre"], wts["w_main"], layer, wts["w_dt"], mod_row)
    p3 = p2.reshape(b, l, N_MAIN)
    dt3 = dt2.reshape(b, l, DT_PAD)
    if ctx is None:
        o_a, caches = _ctx_attn_call(p3, layer, caches)
        h0_f = h0_b = None
    else:
        cache_k4, cache_v4, bias_tbl, h0_f, h0_b = ctx
        o_a = _na_call(p3, cache_k4, cache_v4, bias_tbl, layer)
    o_b, h_f, h_b = _ssd_calls(p3, dt3, wts, h0_f, h0_b, layer)
    o_c = _sgu_call(p2, wts["g_sgu"], wts["w_s"], wts["b_s_t"])
    y2d = _merge_call(o_a.reshape(t, NA_W), o_b.reshape(t, SSM_W), o_c, p2, x2d, mod_l, wts["g_post"],
                      wts["w_br_a"], wts["w_br_b"], wts["w_br_c"], wts["w_out"], mod_row)
    return y2d.reshape(b, l, D_MODEL), (caches, h_f, h_b)


def _layer_weights(l, g_pre, g_post, w_main, w_dt, conv_w, conv_b, dt_bias, a_log, d_skip, g_ssm, w_s, b_s, g_sgu,
                   w_br_a, w_br_b, w_br_c, w_out):
    dtb = jnp.pad(dt_bias[l].reshape(N_DT).astype(_F32), (0, DT_PAD - N_DT))
    alog = jnp.pad(a_log[l].reshape(N_DT).astype(_F32), (0, DT_PAD - N_DT))
    return {
        "g_pre": g_pre[l].reshape(1, D_MODEL), "g_post": g_post[l].reshape(1, D_MODEL),
        "w_main": w_main, "w_dt": w_dt,
        "conv_wt": conv_w[l].T, "conv_b": conv_b[l].reshape(1, CONV_CH),
        "dtb_row": dtb.reshape(1, DT_PAD), "alog_row": alog.reshape(1, DT_PAD),
        "dtb_col": dtb.reshape(DT_PAD, 1), "alog_col": alog.reshape(DT_PAD, 1),
        "dskip": jnp.repeat(d_skip[l].astype(_F32), SSM_HEADDIM).reshape(1, SSM_W),
        "g_ssm": g_ssm[l].reshape(1, SSM_W),
        "g_sgu": g_sgu[l].reshape(1, SGU_W), "w_s": w_s[l].astype(_BF16), "b_s_t": b_s[l].T,
        "w_br_a": w_br_a[l].astype(_BF16), "w_br_b": w_br_b[l].astype(_BF16),
        "w_br_c": w_br_c[l].astype(_BF16), "w_out": w_out[l].astype(_BF16),
    }


def kernel(x_prompt, x_sample, c, cache_k, cache_v, state_ssm_fwd, state_ssm_bwd, c_ctx, w_mod, b_mod, g_pre,
           g_post, w_in, rpb, conv_w, conv_b, dt_bias, a_log, d_skip, g_ssm, w_s, b_s, g_sgu, w_br_a, w_br_b,
           w_br_c, w_out):
    nb, ls, _ = x_sample.shape
    assert 1 + nb <= MOD_ROWS
    cvecs = jnp.concatenate([c_ctx[None, :], c, jnp.zeros((MOD_ROWS - 1 - nb, D_MODEL), _F32)], axis=0)
    mod = _mod_call(cvecs, w_mod, b_mod).reshape(DEPTH, MOD_ROWS, 1, 3 * D_MODEL)
    past = cache_k.shape[2]
    cache_k4 = cache_k.reshape(nb, DEPTH, past, NA_W)
    cache_v4 = cache_v.reshape(nb, DEPTH, past, NA_W)

    def prompt_row(i, tm):
        return 0

    def sample_row(i, tm):
        return 1 + (i * tm) // ls

    y_p, y_s = x_prompt, x_sample
    caches = None
    hf_l, hb_l = [], []
    w_main, w_dt = _repack_call(jnp.swapaxes(w_in, 1, 2))
    for l in range(DEPTH):
        wts = _layer_weights(l, g_pre, g_post, w_main, w_dt, conv_w, conv_b, dt_bias, a_log, d_skip, g_ssm, w_s,
                             b_s, g_sgu, w_br_a, w_br_b, w_br_c, w_out)
        y_p, (caches, h_f, h_b) = _layer(y_p, mod[l], wts, prompt_row, None, l, caches)
        hf_l.append(h_f)
        hb_l.append(h_b)
        ctx = (cache_k4, cache_v4, _na_bias_table(rpb[l]), state_ssm_fwd, state_ssm_bwd)
        y_s, _ = _layer(y_s, mod[l], wts, sample_row, ctx, l)
    bp, lp, _ = x_prompt.shape
    new_k = caches[0].reshape(bp, DEPTH, lp, NA_HEADS, NA_HEAD_DIM)
    new_v = caches[1].reshape(bp, DEPTH, lp, NA_HEADS, NA_HEAD_DIM)
    return (y_p, y_s, new_k, new_v, jnp.stack(hf_l, axis=1), jnp.stack(hb_l, axis=1))
```

```python
import functools

import jax
import jax.numpy as jnp
from jax import lax
from jax.experimental import pallas as pl
from jax.experimental.pallas import tpu as pltpu

D_MODEL = 2048
DEPTH = 2
EPS = 1e-6
GRID_W = 64
NA_HEAD_DIM = 64
NA_W = D_MODEL // 2
NA_HEADS = NA_W // NA_HEAD_DIM
NA_KH = 8
NA_KW = 16
SSM_HEADDIM = 64
SSM_W = D_MODEL // 2
SSM_HEADS = SSM_W // SSM_HEADDIM
SSM_GROUPS = 4
SSM_STATE = 128
SSM_CHUNK = 128
CONV_CH = SSM_W + 2 * SSM_GROUPS * SSM_STATE
SGU_W = D_MODEL // 2
SGU_GROUPS = 8
SGU_CHUNK = 128

N_MAIN = 16 * 1024
DT_PAD = 128
COL_Q, COL_K, COL_V, COL_GA, COL_XBC, COL_Z, COL_U, COL_VC, COL_GC, COL_GM = 0, 1, 2, 3, 4, 6, 7, 8, 9, 10

NEG = -1e30
MOD_ROWS = 8
VMEM_LIMIT = 56 * 1024 * 1024

_F32 = jnp.float32
_BF16 = jnp.bfloat16
_NT = (((1,), (1,)), ((), ()))


def _sigmoid(x):
    return 1.0 / (1.0 + jnp.exp(-x))


def _silu(x):
    return x * _sigmoid(x)


def _softplus(x):
    return jnp.maximum(x, 0.0) + jnp.log(1.0 + jnp.exp(-jnp.abs(x)))


def _dot(a, b):
    return jnp.dot(a, b, preferred_element_type=_F32)


def _params(*sem):
    return pltpu.CompilerParams(dimension_semantics=sem, vmem_limit_bytes=VMEM_LIMIT)


def _mod_body(c_ref, w_ref, b_ref, o_ref):
    s = _silu(c_ref[...]).astype(_BF16)
    o_ref[...] = _dot(s, w_ref[...].astype(_BF16)) + b_ref[...]


def _mod_call(cvecs, w_mod, b_mod):
    tn = 1024
    n3 = 3 * D_MODEL
    return pl.pallas_call(
        _mod_body,
        grid=(DEPTH, n3 // tn),
        in_specs=[pl.BlockSpec((MOD_ROWS, D_MODEL), lambda l, j: (0, 0)),
                  pl.BlockSpec((None, D_MODEL, tn), lambda l, j: (l, 0, j)),
                  pl.BlockSpec((None, 1, tn), lambda l, j: (l, 0, j))],
        out_specs=pl.BlockSpec((None, MOD_ROWS, tn), lambda l, j: (l, 0, j)),
        out_shape=jax.ShapeDtypeStruct((DEPTH, MOD_ROWS, n3), _F32),
        compiler_params=_params("arbitrary", "arbitrary"),
        name="modulation",
    )(cvecs, w_mod, b_mod.reshape(DEPTH, 1, n3))


N_DT = 2 * SSM_HEADS
OFF_DT = 4 * NA_W + CONV_CH + SSM_W
REPACK_BLK = 512


def _repack_body(a_ref, b_ref, o_ref, odt_ref):
    r = pl.program_id(1)
    cut = OFF_DT // REPACK_BLK

    @pl.when(r < cut)
    def _():
        o_ref[...] = a_ref[...].T.astype(_BF16)

    @pl.when(r >= cut)
    def _():
        o_ref[...] = jnp.concatenate([a_ref[N_DT:, :], b_ref[...]], axis=0).T.astype(_BF16)

    @pl.when(r == cut)
    def _():
        dt_rows = jnp.concatenate([a_ref[:N_DT, :], jnp.zeros((DT_PAD - N_DT, D_MODEL), _F32)], axis=0)
        odt_ref[...] = dt_rows.T.astype(_BF16)


def _repack_call(w_in_t):
    return pl.pallas_call(
        _repack_body,
        grid=(DEPTH, N_MAIN // REPACK_BLK),
        in_specs=[pl.BlockSpec((None, REPACK_BLK, D_MODEL), lambda l, r: (l, r, 0)),
                  pl.BlockSpec((None, N_DT, D_MODEL), lambda l, r: (l, (r + 1) * (REPACK_BLK // N_DT), 0))],
        out_specs=[pl.BlockSpec((None, D_MODEL, REPACK_BLK), lambda l, r: (l, 0, r)),
                   pl.BlockSpec((None, D_MODEL, DT_PAD), lambda l, r: (l, 0, 0))],
        out_shape=[jax.ShapeDtypeStruct((DEPTH, D_MODEL, N_MAIN), _BF16),
                   jax.ShapeDtypeStruct((DEPTH, D_MODEL, DT_PAD), _BF16)],
        compiler_params=_params("arbitrary", "arbitrary"),
        name="repack_w_in",
    )(w_in_t, w_in_t)


def _inproj_body(x_ref, mod_ref, g_ref, w_ref, wdt_ref, o_ref, dt_ref, h_scr):
    @pl.when(pl.program_id(1) == 0)
    def _():
        x = x_ref[...]
        xn = x * lax.rsqrt(jnp.mean(x * x, axis=-1, keepdims=True) + EPS) * g_ref[...]
        shift = mod_ref[:, 0:D_MODEL]
        scale = mod_ref[:, D_MODEL:2 * D_MODEL]
        h = (xn * (1.0 + scale) + shift).astype(_BF16)
        h_scr[...] = h
        dt_ref[...] = _dot(h, wdt_ref[...])

    o_ref[...] = _dot(h_scr[...], w_ref[...])


def _inproj_call(x2d, mod_l, g_pre_l, w_main, layer, w_dt, mod_row):
    t = x2d.shape[0]
    tm, tn = 1024, 1024
    return pl.pallas_call(
        _inproj_body,
        grid=(t // tm, N_MAIN // tn),
        in_specs=[pl.BlockSpec((tm, D_MODEL), lambda i, j: (i, 0)),
                  pl.BlockSpec((None, 1, 3 * D_MODEL), lambda i, j: (mod_row(i, tm), 0, 0)),
                  pl.BlockSpec((1, D_MODEL), lambda i, j: (0, 0)),
                  pl.BlockSpec((None, D_MODEL, tn), lambda i, j: (layer, 0, j)),
                  pl.BlockSpec((None, D_MODEL, DT_PAD), lambda i, j: (layer, 0, 0))],
        out_specs=[pl.BlockSpec((tm, tn), lambda i, j: (i, j)),
                   pl.BlockSpec((tm, DT_PAD), lambda i, j: (i, 0))],
        out_shape=[jax.ShapeDtypeStruct((t, N_MAIN), _F32),
                   jax.ShapeDtypeStruct((t, DT_PAD), _F32)],
        scratch_shapes=[pltpu.VMEM((tm, D_MODEL), _BF16)],
        compiler_params=_params("arbitrary", "arbitrary"),
        name="inproj",
    )(x2d, mod_l, g_pre_l, w_main, w_dt)


def _ctx_attn_body(q_ref, k_ref, v_ref, ga_ref, kbuf_ref, vbuf_ref, o_ref, ko_ref, vo_ref):
    del kbuf_ref, vbuf_ref
    l = q_ref.shape[0]
    hp = 2 * NA_HEAD_DIM
    ko_ref[...] = k_ref[...]
    vo_ref[...] = v_ref[...]
    head0 = lax.broadcasted_iota(jnp.int32, (l, hp), 1) < NA_HEAD_DIM
    ones = jnp.ones((l, hp), _BF16)
    for j in range(NA_HEADS // 2):
        cols = slice(j * hp, (j + 1) * hp)
        q = q_ref[:, cols] * (NA_HEAD_DIM ** -0.5)
        q2 = jnp.concatenate([jnp.where(head0, q, 0.0), jnp.where(head0, 0.0, q)], axis=0).astype(_BF16)
        s = lax.dot_general(q2, k_ref[:, cols].astype(_BF16), _NT, preferred_element_type=_F32)
        p = jnp.exp(s - jnp.max(s, axis=-1, keepdims=True)).astype(_BF16)
        v_ext = jnp.concatenate([v_ref[:, cols].astype(_BF16), ones], axis=1)
        oe = _dot(p, v_ext)
        on = oe[:, :hp] / oe[:, hp:]
        o = jnp.where(head0, on[:l], on[l:])
        o_ref[:, cols] = (o * _silu(ga_ref[:, cols])).astype(_BF16)


def _ctx_attn_call(p3, layer, caches):
    b, l, _ = p3.shape

    def spec(col):
        return pl.BlockSpec((None, l, NA_W), lambda i: (i, 0, col))

    cache_spec = pl.BlockSpec((None, None, l, NA_W), lambda i: (i, layer, 0, 0))
    cache_shape = jax.ShapeDtypeStruct((b, DEPTH, l, NA_W), _F32)
    any_spec = pl.BlockSpec(memory_space=pl.ANY)
    o_a, k_all, v_all = pl.pallas_call(
        _ctx_attn_body,
        grid=(b,),
        in_specs=[spec(COL_Q), spec(COL_K), spec(COL_V), spec(COL_GA), any_spec, any_spec],
        out_specs=[pl.BlockSpec((None, l, NA_W), lambda i: (i, 0, 0)), cache_spec, cache_spec],
        out_shape=[jax.ShapeDtypeStruct((b, l, NA_W), _BF16), cache_shape, cache_shape],
        input_output_aliases={4: 1, 5: 2},
        compiler_params=_params("arbitrary"),
        name="ctx_attn",
    )(p3, p3, p3, p3, *caches)
    return o_a, (k_all, v_all)


def _na_bias_table(rpb_l):
    j = jnp.arange(GRID_W)[None, :, None]
    c = jnp.arange(GRID_W)[None, None, :]
    k = jnp.arange(2 * NA_KW - 1)[:, None, None]
    cs = jnp.clip(j - NA_KW // 2, 0, GRID_W - NA_KW)
    valid = (c >= cs) & (c < cs + NA_KW)
    sel = (valid & (c - j + (NA_KW - 1) == k)).astype(_F32)
    tbl = jnp.einsum("hdk,kjc->hdjc", rpb_l.astype(_F32), sel, precision=lax.Precision.HIGHEST)
    tbl = tbl + jnp.where(valid, 0.0, NEG)[None]
    return jnp.concatenate([tbl[:, :-1], tbl[:, 1:]], axis=-1)


def _na_body(q_ref, k_ref, v_ref, ga_ref, kc_ref, vc_ref, bias_ref, o_ref, kb_scr, vb_scr, kcb_scr, vcb_scr,
             *, rows):
    hp = 2 * NA_HEAD_DIM
    win = NA_KH * GRID_W
    kb_scr[...] = k_ref[...].astype(_BF16)
    vb_scr[:, :hp] = v_ref[...].astype(_BF16)
    vb_scr[:, hp:] = jnp.ones((vb_scr.shape[0], hp), _BF16)
    kcb_scr[...] = kc_ref[...].astype(_BF16)
    vcb_scr[:, :hp] = vc_ref[...].astype(_BF16)
    vcb_scr[:, hp:] = jnp.ones((vcb_scr.shape[0], hp), _BF16)
    head0 = lax.broadcasted_iota(jnp.int32, (GRID_W, hp), 1) < NA_HEAD_DIM

    def row(r, carry):
        rs = jnp.clip(r - NA_KH // 2, 0, rows - NA_KH)
        dr0 = rs - r + (NA_KH - 1)
        q0 = pl.multiple_of(r * GRID_W, GRID_W)
        k0 = pl.multiple_of(rs * GRID_W, GRID_W)
        q = q_ref[pl.ds(q0, GRID_W), :] * (NA_HEAD_DIM ** -0.5)
        q2 = jnp.concatenate([jnp.where(head0, q, 0.0), jnp.where(head0, 0.0, q)], axis=0).astype(_BF16)
        bias = jnp.concatenate(
            [jnp.concatenate([bias_ref[hh, dr0 + a] for a in range(0, NA_KH, 2)], axis=1) for hh in range(2)],
            axis=0)
        s_loc = lax.dot_general(q2, kb_scr[pl.ds(k0, win), :], _NT, preferred_element_type=_F32) + bias
        s_ctx = lax.dot_general(q2, kcb_scr[...], _NT, preferred_element_type=_F32)
        m = jnp.maximum(jnp.max(s_loc, axis=-1, keepdims=True), jnp.max(s_ctx, axis=-1, keepdims=True))
        p_loc = jnp.exp(s_loc - m).astype(_BF16)
        p_ctx = jnp.exp(s_ctx - m).astype(_BF16)
        oe = _dot(p_loc, vb_scr[pl.ds(k0, win), :]) + _dot(p_ctx, vcb_scr[...])
        on = oe[:, :hp] / oe[:, hp:]
        o = jnp.where(head0, on[:GRID_W], on[GRID_W:])
        o_ref[pl.ds(q0, GRID_W), :] = (o * _silu(ga_ref[pl.ds(q0, GRID_W), :])).astype(_BF16)
        return carry

    lax.fori_loop(0, rows, row, 0, unroll=16)


def _na_call(p3, cache_k4, cache_v4, bias_tbl, layer):
    b, l, _ = p3.shape
    rows = l // GRID_W
    assert rows >= NA_KH and rows % 2 == 0
    hp = 2 * NA_HEAD_DIM
    nblk = N_MAIN // hp // 16
    lc = cache_k4.shape[2]

    def spec(col):
        return pl.BlockSpec((None, l, hp), lambda i, j: (i, 0, col * nblk + j))

    cspec = pl.BlockSpec((None, None, lc, hp), lambda i, j: (i, layer, 0, j))
    return pl.pallas_call(
        functools.partial(_na_body, rows=rows),
        grid=(b, NA_HEADS // 2),
        in_specs=[spec(COL_Q), spec(COL_K), spec(COL_V), spec(COL_GA), cspec, cspec,
                  pl.BlockSpec((2, 2 * NA_KH - 2, GRID_W, 2 * GRID_W), lambda i, j: (j, 0, 0, 0))],
        out_specs=pl.BlockSpec((None, l, hp), lambda i, j: (i, 0, j)),
        out_shape=jax.ShapeDtypeStruct((b, l, NA_W), _BF16),
        scratch_shapes=[pltpu.VMEM((l, hp), _BF16), pltpu.VMEM((l, 2 * hp), _BF16),
                        pltpu.VMEM((lc, hp), _BF16), pltpu.VMEM((lc, 2 * hp), _BF16)],
        compiler_params=_params("arbitrary", "arbitrary"),
        name="na_attn",
    )(p3, p3, p3, p3, cache_k4, cache_v4, bias_tbl)


HEADS_PER_GROUP = SSM_HEADS // SSM_GROUPS
GROUP_W = HEADS_PER_GROUP * SSM_HEADDIM
SSD_CHUNKS_PER_STEP = 2


def _split3(x):
    hi = x.astype(_BF16)
    r1 = x - hi.astype(_F32)
    mid = r1.astype(_BF16)
    lo = (r1 - mid.astype(_F32)).astype(_BF16)
    return hi, mid, lo


def _silu_tanh(x):
    h = 0.5 * x
    return h + h * jnp.tanh(h)


def _ssd_chunk(xs_b, bm, cm, dtraw, dtb_row_ref, alog_row_ref, dtb_col_ref, alog_col_ref, ht_scr, reverse):
    q = SSM_CHUNK
    d = 1 if reverse else 0
    last = 0 if reverse else q - 1

    dta_col = _softplus(dtraw + dtb_row_ref[...]) * (-jnp.exp(alog_row_ref[...]))
    dt_row_all = _softplus(dtraw.T + dtb_col_ref[...])
    dta_row_all = dt_row_all * (-jnp.exp(alog_col_ref[...]))
    dt_row = dt_row_all[d * SSM_HEADS:(d + 1) * SSM_HEADS, :]
    dta_row = dta_row_all[d * SSM_HEADS:(d + 1) * SSM_HEADS, :]

    ri = lax.broadcasted_iota(jnp.int32, (q, q), 0)
    ci = lax.broadcasted_iota(jnp.int32, (q, q), 1)
    causal = (ci >= ri) if reverse else (ci <= ri)
    t_col = jnp.where(causal, 1.0, 0.0).astype(_BF16)
    t_row = jnp.where((ri >= ci) if reverse else (ri <= ci), 1.0, 0.0).astype(_BF16)
    acum_col = sum(_dot(t_col, part) for part in _split3(dta_col))
    acum_row = sum(_dot(part, t_row) for part in _split3(dta_row))

    a_last = jnp.broadcast_to(acum_row[:, last:last + 1], (SSM_HEADS, q))
    dtde_row = dt_row * jnp.exp(a_last - acum_row)
    cdecay = jnp.exp(a_last)

    lo_half = lax.broadcasted_iota(jnp.int32, (1, q), 1) < SSM_HEADDIM
    lane_head = lax.broadcasted_iota(jnp.int32, (1, GROUP_W), 1) // SSM_HEADDIM
    zero_b = jnp.zeros((q, GROUP_W), _BF16)

    def per_head_lanes(vals):
        return jnp.concatenate([jnp.where(lo_half, vals[0], vals[1]), jnp.where(lo_half, vals[2], vals[3])], axis=1)

    ys = []
    for g in range(SSM_GROUPS):
        nsl = slice(g * SSM_STATE, (g + 1) * SSM_STATE)
        b_g = bm[:, nsl]
        c_g = cm[:, nsl]
        cb = lax.dot_general(c_g, b_g, _NT, preferred_element_type=_F32)
        b_t = b_g.astype(_F32).T
        xs_g = xs_b[:, g * GROUP_W:(g + 1) * GROUP_W]
        m_parts, bt_parts, xbd_parts, bcs = [], [], [], []
        for hh in range(HEADS_PER_GROUP):
            h = g * HEADS_PER_GROUP + hh
            lane = d * SSM_HEADS + h
            bc = jnp.broadcast_to(acum_col[:, lane:lane + 1], (q, q))
            lmat = jnp.exp(jnp.where(causal, bc - acum_row[h:h + 1, :], NEG))
            m_parts.append((cb * lmat * dt_row[h:h + 1, :]).astype(_BF16))
            bt_parts.append((b_t * dtde_row[h:h + 1, :]).astype(_BF16))
            xbd_parts.append(jnp.where(lane_head == hh, xs_g, zero_b))
            bcs.append(bc)
        lhs = jnp.concatenate([jnp.concatenate(m_parts, axis=1), jnp.concatenate(bt_parts, axis=1)], axis=0)
        res = _dot(lhs, jnp.concatenate(xbd_parts, axis=0))
        h_t = ht_scr[g]
        y_off = _dot(c_g, h_t.astype(_BF16)) * jnp.exp(per_head_lanes(bcs))
        ys.append(res[:q] + y_off)
        h0 = g * HEADS_PER_GROUP
        cd = per_head_lanes([cdecay[h0 + hh:h0 + hh + 1, :] for hh in range(HEADS_PER_GROUP)])
        ht_scr[g] = h_t * cd + res[q:]
    return ys


def _ssd_state_io(c, nc, h0_ref, hout_ref, ht_scr):
    @pl.when(c == 0)
    def _():
        for g in range(SSM_GROUPS):
            if h0_ref is None:
                ht_scr[g] = jnp.zeros((SSM_STATE, GROUP_W), _F32)
            else:
                hs = h0_ref[g * HEADS_PER_GROUP:(g + 1) * HEADS_PER_GROUP]
                ht_scr[g] = hs.reshape(GROUP_W, SSM_STATE).T

    def store_final():
        @pl.when(c == nc - 1)
        def _():
            for g in range(SSM_GROUPS):
                hout_ref[g * HEADS_PER_GROUP:(g + 1) * HEADS_PER_GROUP] = ht_scr[g].T.reshape(
                    HEADS_PER_GROUP, SSM_HEADDIM, SSM_STATE)

    return store_final


def _ssd_fwd_body(*refs, has_h0, nc):
    it = iter(refs)
    x_ref, prev_ref, next_ref, dt_ref, cw_ref, cb_ref = (next(it) for _ in range(6))
    dec_refs = [next(it) for _ in range(4)]
    dskip_ref = next(it)
    h0_ref = next(it) if has_h0 else None
    y_ref, act_ref, hout_ref, ht_scr = (next(it) for _ in range(4))
    q = SSM_CHUNK
    qb = x_ref.shape[0]
    c = pl.program_id(1)
    store_final = _ssd_state_io(c, nc, h0_ref, hout_ref, ht_scr)

    x = x_ref[...]
    xp = jnp.where(c > 0, prev_ref[7:8, :], 0.0)
    xn = jnp.where(c < nc - 1, next_ref[0:1, :], 0.0)
    sub = lax.broadcasted_iota(jnp.int32, (8, 1), 0)
    x_m1 = pltpu.roll(x, 1, axis=0)
    x_m1 = jnp.concatenate([jnp.where(sub == 0, xp, x_m1[:8]), x_m1[8:]], axis=0)
    x_p1 = pltpu.roll(x, qb - 1, axis=0)
    x_p1 = jnp.concatenate([x_p1[:qb - 8], jnp.where(sub == 7, xn, x_p1[qb - 8:])], axis=0)
    act = _silu_tanh(cw_ref[0:1, :] * x_m1 + cw_ref[1:2, :] * x + cw_ref[2:3, :] * x_p1 + cb_ref[...])
    act_b = act.astype(_BF16)
    act_ref[...] = act_b

    for s in range(qb // q):
        rows = slice(s * q, (s + 1) * q)
        ys = _ssd_chunk(act_b[rows, :SSM_W], act_b[rows, SSM_W:SSM_W + SSM_GROUPS * SSM_STATE],
                        act_b[rows, SSM_W + SSM_GROUPS * SSM_STATE:], dt_ref[rows, :], *dec_refs, ht_scr, False)
        y_ref[rows, :] = jnp.concatenate(ys, axis=1) + act[rows, :SSM_W] * dskip_ref[...]
    store_final()


def _ssd_bwd_body(*refs, has_h0, nc):
    it = iter(refs)
    act_ref, dt_ref = next(it), next(it)
    dec_refs = [next(it) for _ in range(4)]
    yf_ref, z_ref, gssm_ref = next(it), next(it), next(it)
    h0_ref = next(it) if has_h0 else None
    o_ref, hout_ref, ht_scr = next(it), next(it), next(it)
    c = pl.program_id(1)
    store_final = _ssd_state_io(c, nc, h0_ref, hout_ref, ht_scr)

    q = SSM_CHUNK
    for s in reversed(range(act_ref.shape[0] // q)):
        rows = slice(s * q, (s + 1) * q)
        ys = _ssd_chunk(act_ref[rows, :SSM_W], act_ref[rows, SSM_W:SSM_W + SSM_GROUPS * SSM_STATE],
                        act_ref[rows, SSM_W + SSM_GROUPS * SSM_STATE:], dt_ref[rows, :], *dec_refs, ht_scr, True)
        y = (yf_ref[rows, :] + jnp.concatenate(ys, axis=1)) * _silu_tanh(z_ref[rows, :])
        y = y * lax.rsqrt(jnp.mean(y * y, axis=-1, keepdims=True) + EPS) * gssm_ref[...]
        o_ref[rows, :] = y.astype(_BF16)
    store_final()


def _ssd_calls(p3, dt3, prm, h0_f, h0_b, layer):
    b, l, _ = p3.shape
    q = SSD_CHUNKS_PER_STEP * SSM_CHUNK
    nc = l // q
    has_h0 = h0_f is not None
    xbc_blk = COL_XBC // 2

    def vec(w):
        return pl.BlockSpec((1, w), lambda i, c: (0, 0))

    dec_specs = [vec(DT_PAD), vec(DT_PAD),
                 pl.BlockSpec((DT_PAD, 1), lambda i, c: (0, 0)), pl.BlockSpec((DT_PAD, 1), lambda i, c: (0, 0))]
    dec_args = [prm["dtb_row"], prm["alog_row"], prm["dtb_col"], prm["alog_col"]]
    h0_spec = pl.BlockSpec((None, None, SSM_HEADS, SSM_HEADDIM, SSM_STATE), lambda i, c: (i, layer, 0, 0, 0))
    state_spec = pl.BlockSpec((None, SSM_HEADS, SSM_HEADDIM, SSM_STATE), lambda i, c: (i, 0, 0, 0))
    state_shape = jax.ShapeDtypeStruct((b, SSM_HEADS, SSM_HEADDIM, SSM_STATE), _F32)
    ht_scratch = pltpu.VMEM((SSM_GROUPS, SSM_STATE, GROUP_W), _F32)

    in_specs = [
        pl.BlockSpec((None, q, CONV_CH), lambda i, c: (i, c, xbc_blk)),
        pl.BlockSpec((None, 8, CONV_CH), lambda i, c: (i, jnp.maximum(c * (q // 8) - 1, 0), xbc_blk)),
        pl.BlockSpec((None, 8, CONV_CH), lambda i, c: (i, jnp.minimum((c + 1) * (q // 8), l // 8 - 1), xbc_blk)),
        pl.BlockSpec((None, q, DT_PAD), lambda i, c: (i, c, 0)),
        pl.BlockSpec((3, CONV_CH), lambda i, c: (0, 0)), vec(CONV_CH)] + dec_specs + [vec(SSM_W)]
    args = [p3, p3, p3, dt3, prm["conv_wt"], prm["conv_b"]] + dec_args + [prm["dskip"]]
    if has_h0:
        in_specs.append(h0_spec)
        args.append(h0_f)
    y_f, act, h_f = pl.pallas_call(
        functools.partial(_ssd_fwd_body, has_h0=has_h0, nc=nc),
        grid=(b, nc),
        in_specs=in_specs,
        out_specs=[pl.BlockSpec((None, q, SSM_W), lambda i, c: (i, c, 0)),
                   pl.BlockSpec((None, q, CONV_CH), lambda i, c: (i, c, 0)), state_spec],
        out_shape=[jax.ShapeDtypeStruct((b, l, SSM_W), _F32),
                   jax.ShapeDtypeStruct((b, l, CONV_CH), _BF16), state_shape],
        scratch_shapes=[ht_scratch],
        compiler_params=_params("arbitrary", "arbitrary"),
        name="ssd_fwd",
    )(*args)

    def rc(c):
        return nc - 1 - c

    in_specs = [pl.BlockSpec((None, q, CONV_CH), lambda i, c: (i, rc(c), 0)),
                pl.BlockSpec((None, q, DT_PAD), lambda i, c: (i, rc(c), 0))] + dec_specs + [
        pl.BlockSpec((None, q, SSM_W), lambda i, c: (i, rc(c), 0)),
        pl.BlockSpec((None, q, SSM_W), lambda i, c: (i, rc(c), COL_Z)), vec(SSM_W)]
    args = [act, dt3] + dec_args + [y_f, p3, prm["g_ssm"]]
    if has_h0:
        in_specs.append(h0_spec)
        args.append(h0_b)
    o_b, h_b = pl.pallas_call(
        functools.partial(_ssd_bwd_body, has_h0=has_h0, nc=nc),
        grid=(b, nc),
        in_specs=in_specs,
        out_specs=[pl.BlockSpec((None, q, SSM_W), lambda i, c: (i, rc(c), 0)), state_spec],
        out_shape=[jax.ShapeDtypeStruct((b, l, SSM_W), _BF16), state_shape],
        scratch_shapes=[ht_scratch],
        compiler_params=_params("arbitrary", "arbitrary"),
        name="ssd_bwd",
    )(*args)
    return o_b, h_f, h_b


def _sgu_tile(u_ref, v_ref, g_ref, gs_ref, ws_ref, bs_ref, o_ref):
    v = v_ref[...]
    mu = jnp.mean(v, axis=-1, keepdims=True)
    vc = v - mu
    var = jnp.mean(vc * vc, axis=-1, keepdims=True)
    vn = (vc * lax.rsqrt(var + EPS) * gs_ref[...]).astype(_BF16)
    ge = SGU_W // SGU_GROUPS
    for ch in range(v.shape[0] // SGU_CHUNK):
        rsl = slice(ch * SGU_CHUNK, (ch + 1) * SGU_CHUNK)
        for g in range(SGU_GROUPS):
            csl = slice(g * ge, (g + 1) * ge)
            vs = _dot(ws_ref[g], vn[rsl, csl]) + bs_ref[:, g:g + 1]
            y = u_ref[rsl, csl] * vs * _silu(g_ref[rsl, csl])
            o_ref[rsl, csl] = y.astype(_BF16)


def _merge_body(oa_ref, ob_ref, u_ref, vc_ref, gc_ref, ga_ref, gb_ref, gcm_ref, x_ref, mod_ref, gpost_ref,
                gs_ref, ws_ref, bs_ref, wa_ref, wb_ref, wc_ref, wo_ref, out_ref, oc_scr):
    _sgu_tile(u_ref, vc_ref, gc_ref, gs_ref, ws_ref, bs_ref, oc_scr)
    merged = _sigmoid(ga_ref[...]) * _dot(oa_ref[...], wa_ref[...])
    merged = merged + _sigmoid(gb_ref[...]) * _dot(ob_ref[...], wb_ref[...])
    merged = merged + _sigmoid(gcm_ref[...]) * _dot(oc_scr[...], wc_ref[...])
    y = _dot(merged.astype(_BF16), wo_ref[...])
    y = y * lax.rsqrt(jnp.mean(y * y, axis=-1, keepdims=True) + EPS) * gpost_ref[...]
    out_ref[...] = x_ref[...] + mod_ref[:, 2 * D_MODEL:] * y


def _merge_call(o_a, o_b, p2, x2d, mod_l, prm, mod_row):
    t = x2d.shape[0]
    tm = 256
    gm_blk = COL_GM // 2
    once = pl.Buffered(1)

    def row_spec(w, col=0):
        return pl.BlockSpec((tm, w), lambda i: (i, col))

    def const_spec(shape):
        return pl.BlockSpec(shape, lambda i: (0,) * len(shape))

    def w_spec(kdim):
        return pl.BlockSpec((kdim, D_MODEL), lambda i: (0, 0), pipeline_mode=once)

    return pl.pallas_call(
        _merge_body,
        grid=(t // tm,),
        in_specs=[row_spec(NA_W), row_spec(SSM_W),
                  row_spec(SGU_W, COL_U), row_spec(SGU_W, COL_VC), row_spec(SGU_W, COL_GC),
                  row_spec(D_MODEL, gm_blk), row_spec(D_MODEL, gm_blk + 1), row_spec(D_MODEL, gm_blk + 2),
                  row_spec(D_MODEL),
                  pl.BlockSpec((None, 1, 3 * D_MODEL), lambda i: (mod_row(i, tm), 0, 0)),
                  const_spec((1, D_MODEL)), const_spec((1, SGU_W)),
                  const_spec((SGU_GROUPS, SGU_CHUNK, SGU_CHUNK)), const_spec((SGU_CHUNK, SGU_GROUPS)),
                  w_spec(NA_W), w_spec(SSM_W), w_spec(SGU_W), w_spec(D_MODEL)],
        out_specs=row_spec(D_MODEL),
        out_shape=jax.ShapeDtypeStruct((t, D_MODEL), _F32),
        scratch_shapes=[pltpu.VMEM((tm, SGU_W), _BF16)],
        compiler_params=_params("arbitrary"),
        name="merge_out",
    )(o_a, o_b, p2, p2, p2, p2, p2, p2, x2d, mod_l, prm["g_post"], prm["g_sgu"], prm["w_s"], prm["b_s_t"],
      prm["w_br_a"], prm["w_br_b"], prm["w_br_c"], prm["w_out"])


def _layer(x3, mod_l, wts, mod_row, ctx, layer, caches=None):
    b, l, _ = x3.shape
    t = b * l
    x2d = x3.reshape(t, D_MODEL)
    p2, dt2 = _inproj_call(x2d, mod_l, wts["g_pre"], wts["w_main"], layer, wts["w_dt"], mod_row)
    p3 = p2.reshape(b, l, N_MAIN)
    dt3 = dt2.reshape(b, l, DT_PAD)
    if ctx is None:
        o_a, caches = _ctx_attn_call(p3, layer, caches)
        h0_f = h0_b = None
    else:
        cache_k4, cache_v4, bias_tbl, h0_f, h0_b = ctx
        o_a = _na_call(p3, cache_k4, cache_v4, bias_tbl, layer)
    o_b, h_f, h_b = _ssd_calls(p3, dt3, wts, h0_f, h0_b, layer)
    y2d = _merge_call(o_a.reshape(t, NA_W), o_b.reshape(t, SSM_W), p2, x2d, mod_l, wts, mod_row)
    return y2d.reshape(b, l, D_MODEL), (caches, h_f, h_b)


def _layer_weights(l, g_pre, g_post, w_main, w_dt, conv_w, conv_b, dt_bias, a_log, d_skip, g_ssm, w_s, b_s, g_sgu,
                   w_br_a, w_br_b, w_br_c, w_out):
    dtb = jnp.pad(dt_bias[l].reshape(N_DT).astype(_F32), (0, DT_PAD - N_DT))
    alog = jnp.pad(a_log[l].reshape(N_DT).astype(_F32), (0, DT_PAD - N_DT))
    return {
        "g_pre": g_pre[l].reshape(1, D_MODEL), "g_post": g_post[l].reshape(1, D_MODEL),
        "w_main": w_main, "w_dt": w_dt,
        "conv_wt": conv_w[l].T, "conv_b": conv_b[l].reshape(1, CONV_CH),
        "dtb_row": dtb.reshape(1, DT_PAD), "alog_row": alog.reshape(1, DT_PAD),
        "dtb_col": dtb.reshape(DT_PAD, 1), "alog_col": alog.reshape(DT_PAD, 1),
        "dskip": jnp.repeat(d_skip[l].astype(_F32), SSM_HEADDIM).reshape(1, SSM_W),
        "g_ssm": g_ssm[l].reshape(1, SSM_W),
        "g_sgu": g_sgu[l].reshape(1, SGU_W), "w_s": w_s[l].astype(_BF16), "b_s_t": b_s[l].T,
        "w_br_a": w_br_a[l].astype(_BF16), "w_br_b": w_br_b[l].astype(_BF16),
        "w_br_c": w_br_c[l].astype(_BF16), "w_out": w_out[l].astype(_BF16),
    }


def kernel(x_prompt, x_sample, c, cache_k, cache_v, state_ssm_fwd, state_ssm_bwd, c_ctx, w_mod, b_mod, g_pre,
           g_post, w_in, rpb, conv_w, conv_b, dt_bias, a_log, d_skip, g_ssm, w_s, b_s, g_sgu, w_br_a, w_br_b,
           w_br_c, w_out):
    nb, ls, _ = x_sample.shape
    assert 1 + nb <= MOD_ROWS
    cvecs = jnp.concatenate([c_ctx[None, :], c, jnp.zeros((MOD_ROWS - 1 - nb, D_MODEL), _F32)], axis=0)
    mod = _mod_call(cvecs, w_mod, b_mod).reshape(DEPTH, MOD_ROWS, 1, 3 * D_MODEL)
    past = cache_k.shape[2]
    cache_k4 = cache_k.reshape(nb, DEPTH, past, NA_W)
    cache_v4 = cache_v.reshape(nb, DEPTH, past, NA_W)

    def prompt_row(i, tm):
        return 0

    def sample_row(i, tm):
        return 1 + (i * tm) // ls

    y_p, y_s = x_prompt, x_sample
    bp, lp, _ = x_prompt.shape
    caches = (jnp.zeros((bp, DEPTH, lp, NA_W), _F32), jnp.zeros((bp, DEPTH, lp, NA_W), _F32))
    hf_l, hb_l = [], []
    w_main, w_dt = _repack_call(jnp.swapaxes(w_in, 1, 2))
    for l in range(DEPTH):
        wts = _layer_weights(l, g_pre, g_post, w_main, w_dt, conv_w, conv_b, dt_bias, a_log, d_skip, g_ssm, w_s,
                             b_s, g_sgu, w_br_a, w_br_b, w_br_c, w_out)
        y_p, (caches, h_f, h_b) = _layer(y_p, mod[l], wts, prompt_row, None, l, caches)
        hf_l.append(h_f)
        hb_l.append(h_b)
        ctx = (cache_k4, cache_v4, _na_bias_table(rpb[l]), state_ssm_fwd, state_ssm_bwd)
        y_s, _ = _layer(y_s, mod[l], wts, sample_row, ctx, l)
    new_k = caches[0].reshape(bp, DEPTH, lp, NA_HEADS, NA_HEAD_DIM)
    new_v = caches[1].reshape(bp, DEPTH, lp, NA_HEADS, NA_HEAD_DIM)
    return (y_p, y_s, new_k, new_v, jnp.stack(hf_l, axis=1), jnp.stack(hb_l, axis=1))
```

```python
import functools

import jax
import jax.numpy as jnp
from jax import lax
from jax.experimental import pallas as pl
from jax.experimental.pallas import tpu as pltpu

D_MODEL = 2048
DEPTH = 2
EPS = 1e-6
GRID_W = 64
NA_HEAD_DIM = 64
NA_W = D_MODEL // 2
NA_HEADS = NA_W // NA_HEAD_DIM
NA_KH = 8
NA_KW = 16
SSM_HEADDIM = 64
SSM_W = D_MODEL // 2
SSM_HEADS = SSM_W // SSM_HEADDIM
SSM_GROUPS = 4
SSM_STATE = 128
SSM_CHUNK = 128
CONV_CH = SSM_W + 2 * SSM_GROUPS * SSM_STATE
SGU_W = D_MODEL // 2
SGU_GROUPS = 8
SGU_CHUNK = 128

N_MAIN = 16 * 1024
DT_PAD = 128
COL_Q, COL_K, COL_V, COL_GA, COL_XBC, COL_Z, COL_U, COL_VC, COL_GC, COL_GM = 0, 1, 2, 3, 4, 6, 7, 8, 9, 10

NEG = -1e30
MOD_ROWS = 8
VMEM_LIMIT = 56 * 1024 * 1024

_F32 = jnp.float32
_BF16 = jnp.bfloat16
_NT = (((1,), (1,)), ((), ()))


def _sigmoid(x):
    return 1.0 / (1.0 + jnp.exp(-x))


def _silu(x):
    return x * _sigmoid(x)


def _softplus(x):
    return jnp.maximum(x, 0.0) + jnp.log(1.0 + jnp.exp(-jnp.abs(x)))


def _dot(a, b):
    return jnp.dot(a, b, preferred_element_type=_F32)


def _params(*sem):
    return pltpu.CompilerParams(dimension_semantics=sem, vmem_limit_bytes=VMEM_LIMIT)


def _mod_body(c_ref, w_ref, b_ref, o_ref):
    s = _silu(c_ref[...]).astype(_BF16)
    o_ref[...] = _dot(s, w_ref[...].astype(_BF16)) + b_ref[...]


def _mod_call(cvecs, w_mod, b_mod):
    tn = 1024
    n3 = 3 * D_MODEL
    return pl.pallas_call(
        _mod_body,
        grid=(DEPTH, n3 // tn),
        in_specs=[pl.BlockSpec((MOD_ROWS, D_MODEL), lambda l, j: (0, 0)),
                  pl.BlockSpec((None, D_MODEL, tn), lambda l, j: (l, 0, j)),
                  pl.BlockSpec((None, 1, tn), lambda l, j: (l, 0, j))],
        out_specs=pl.BlockSpec((None, MOD_ROWS, tn), lambda l, j: (l, 0, j)),
        out_shape=jax.ShapeDtypeStruct((DEPTH, MOD_ROWS, n3), _F32),
        compiler_params=_params("arbitrary", "arbitrary"),
        name="modulation",
    )(cvecs, w_mod, b_mod.reshape(DEPTH, 1, n3))


N_DT = 2 * SSM_HEADS
OFF_DT = 4 * NA_W + CONV_CH + SSM_W
REPACK_BLK = 512


def _repack_body(a_ref, b_ref, o_ref, odt_ref):
    r = pl.program_id(1)
    cut = OFF_DT // REPACK_BLK

    @pl.when(r < cut)
    def _():
        o_ref[...] = a_ref[...].T.astype(_BF16)

    @pl.when(r >= cut)
    def _():
        o_ref[...] = jnp.concatenate([a_ref[N_DT:, :], b_ref[...]], axis=0).T.astype(_BF16)

    @pl.when(r == cut)
    def _():
        dt_rows = jnp.concatenate([a_ref[:N_DT, :], jnp.zeros((DT_PAD - N_DT, D_MODEL), _F32)], axis=0)
        odt_ref[...] = dt_rows.T.astype(_BF16)


def _repack_call(w_in_t):
    return pl.pallas_call(
        _repack_body,
        grid=(DEPTH, N_MAIN // REPACK_BLK),
        in_specs=[pl.BlockSpec((None, REPACK_BLK, D_MODEL), lambda l, r: (l, r, 0)),
                  pl.BlockSpec((None, N_DT, D_MODEL), lambda l, r: (l, (r + 1) * (REPACK_BLK // N_DT), 0))],
        out_specs=[pl.BlockSpec((None, D_MODEL, REPACK_BLK), lambda l, r: (l, 0, r)),
                   pl.BlockSpec((None, D_MODEL, DT_PAD), lambda l, r: (l, 0, 0))],
        out_shape=[jax.ShapeDtypeStruct((DEPTH, D_MODEL, N_MAIN), _BF16),
                   jax.ShapeDtypeStruct((DEPTH, D_MODEL, DT_PAD), _BF16)],
        compiler_params=_params("arbitrary", "arbitrary"),
        name="repack_w_in",
    )(w_in_t, w_in_t)


def _inproj_body(*refs, want_kv):
    x_ref, mod_ref, g_ref, w_ref, wdt_ref, o_ref, dt_ref = refs[:7]
    h_scr = refs[-1]
    j = pl.program_id(1)

    @pl.when(j == 0)
    def _():
        x = x_ref[...]
        xn = x * lax.rsqrt(jnp.mean(x * x, axis=-1, keepdims=True) + EPS) * g_ref[...]
        shift = mod_ref[:, 0:D_MODEL]
        scale = mod_ref[:, D_MODEL:2 * D_MODEL]
        h = (xn * (1.0 + scale) + shift).astype(_BF16)
        h_scr[...] = h
        dt_ref[...] = _dot(h, wdt_ref[...])

    res = _dot(h_scr[...], w_ref[...])
    o_ref[...] = res.astype(_BF16)
    if want_kv:
        kv_ref = refs[7]

        @pl.when((j == COL_K) | (j == COL_V))
        def _():
            kv_ref[...] = res


def _inproj_call(x2d, mod_l, g_pre_l, w_main, layer, w_dt, mod_row, want_kv):
    t = x2d.shape[0]
    tm = 1024
    tn = NA_W if want_kv else 2 * NA_W
    out_specs = [pl.BlockSpec((tm, tn), lambda i, j: (i, j)),
                 pl.BlockSpec((tm, DT_PAD), lambda i, j: (i, 0))]
    out_shape = [jax.ShapeDtypeStruct((t, N_MAIN), _BF16),
                 jax.ShapeDtypeStruct((t, DT_PAD), _F32)]
    if want_kv:
        out_specs.append(pl.BlockSpec((tm, NA_W), lambda i, j: (i, jnp.clip(j - COL_K, 0, COL_V - COL_K))))
        out_shape.append(jax.ShapeDtypeStruct((t, 2 * NA_W), _F32))
    return pl.pallas_call(
        functools.partial(_inproj_body, want_kv=want_kv),
        grid=(t // tm, N_MAIN // tn),
        in_specs=[pl.BlockSpec((tm, D_MODEL), lambda i, j: (i, 0)),
                  pl.BlockSpec((None, 1, 3 * D_MODEL), lambda i, j: (mod_row(i, tm), 0, 0)),
                  pl.BlockSpec((1, D_MODEL), lambda i, j: (0, 0)),
                  pl.BlockSpec((None, D_MODEL, tn), lambda i, j: (layer, 0, j)),
                  pl.BlockSpec((None, D_MODEL, DT_PAD), lambda i, j: (layer, 0, 0))],
        out_specs=out_specs,
        out_shape=out_shape,
        scratch_shapes=[pltpu.VMEM((tm, D_MODEL), _BF16)],
        compiler_params=_params("arbitrary", "arbitrary"),
        name="inproj",
    )(x2d, mod_l, g_pre_l, w_main, w_dt)


def _stack_heads(q, head0):
    q = q * (NA_HEAD_DIM ** -0.5)
    zero = jnp.zeros_like(q)
    return jnp.concatenate([jnp.where(head0, q, zero), jnp.where(head0, zero, q)], axis=0)


def _ctx_attn_body(q_ref, k_ref, v_ref, ga_ref, kf_ref, vf_ref, kbuf_ref, vbuf_ref, o_ref, ko_ref, vo_ref):
    del kbuf_ref, vbuf_ref
    l = q_ref.shape[0]
    hp = 2 * NA_HEAD_DIM
    ko_ref[...] = kf_ref[...]
    vo_ref[...] = vf_ref[...]
    head0 = lax.broadcasted_iota(jnp.int32, (l, hp), 1) < NA_HEAD_DIM
    ones = jnp.ones((l, hp), _BF16)
    for j in range(NA_HEADS // 2):
        cols = slice(j * hp, (j + 1) * hp)
        q2 = _stack_heads(q_ref[:, cols], head0)
        s = lax.dot_general(q2, k_ref[:, cols], _NT, preferred_element_type=_F32)
        p = jnp.exp(s - jnp.max(s, axis=-1, keepdims=True)).astype(_BF16)
        v_ext = jnp.concatenate([v_ref[:, cols], ones], axis=1)
        oe = _dot(p, v_ext)
        on = oe[:, :hp] / oe[:, hp:]
        o = jnp.where(head0, on[:l], on[l:])
        o_ref[:, cols] = (o * _silu(ga_ref[:, cols].astype(_F32))).astype(_BF16)


def _ctx_attn_call(p3, kv3, layer, caches):
    b, l, _ = p3.shape

    def spec(col):
        return pl.BlockSpec((None, l, NA_W), lambda i: (i, 0, col))

    cache_spec = pl.BlockSpec((None, None, l, NA_W), lambda i: (i, layer, 0, 0))
    cache_shape = jax.ShapeDtypeStruct((b, DEPTH, l, NA_W), _F32)
    any_spec = pl.BlockSpec(memory_space=pl.ANY)
    o_a, k_all, v_all = pl.pallas_call(
        _ctx_attn_body,
        grid=(b,),
        in_specs=[spec(COL_Q), spec(COL_K), spec(COL_V), spec(COL_GA), spec(0), spec(1), any_spec, any_spec],
        out_specs=[pl.BlockSpec((None, l, NA_W), lambda i: (i, 0, 0)), cache_spec, cache_spec],
        out_shape=[jax.ShapeDtypeStruct((b, l, NA_W), _BF16), cache_shape, cache_shape],
        input_output_aliases={6: 1, 7: 2},
        compiler_params=_params("arbitrary"),
        name="ctx_attn",
    )(p3, p3, p3, p3, kv3, kv3, *caches)
    return o_a, (k_all, v_all)


def _na_bias_table(rpb_l):
    j = jnp.arange(GRID_W)[None, :, None]
    c = jnp.arange(GRID_W)[None, None, :]
    k = jnp.arange(2 * NA_KW - 1)[:, None, None]
    cs = jnp.clip(j - NA_KW // 2, 0, GRID_W - NA_KW)
    valid = (c >= cs) & (c < cs + NA_KW)
    sel = (valid & (c - j + (NA_KW - 1) == k)).astype(_F32)
    tbl = jnp.einsum("hdk,kjc->hdjc", rpb_l.astype(_F32), sel, precision=lax.Precision.HIGHEST)
    tbl = tbl + jnp.where(valid, 0.0, NEG)[None]
    return jnp.concatenate([tbl[:, :-1], tbl[:, 1:]], axis=-1)


def _na_body(q_ref, k_ref, v_ref, ga_ref, kc_ref, vc_ref, bias_ref, o_ref, vb_scr, kcb_scr, vcb_scr, *, rows):
    hp = 2 * NA_HEAD_DIM
    win = NA_KH * GRID_W
    vb_scr[:, :hp] = v_ref[...]
    vb_scr[:, hp:] = jnp.ones((vb_scr.shape[0], hp), _BF16)
    kcb_scr[...] = kc_ref[...].astype(_BF16)
    vcb_scr[:, :hp] = vc_ref[...].astype(_BF16)
    vcb_scr[:, hp:] = jnp.ones((vcb_scr.shape[0], hp), _BF16)
    head0 = lax.broadcasted_iota(jnp.int32, (GRID_W, hp), 1) < NA_HEAD_DIM

    def row(r, carry):
        rs = jnp.clip(r - NA_KH // 2, 0, rows - NA_KH)
        dr0 = rs - r + (NA_KH - 1)
        q0 = pl.multiple_of(r * GRID_W, GRID_W)
        k0 = pl.multiple_of(rs * GRID_W, GRID_W)
        q2 = _stack_heads(q_ref[pl.ds(q0, GRID_W), :], head0)
        bias = jnp.concatenate(
            [jnp.concatenate([bias_ref[hh, dr0 + a] for a in range(0, NA_KH, 2)], axis=1) for hh in range(2)],
            axis=0)
        s_loc = lax.dot_general(q2, k_ref[pl.ds(k0, win), :], _NT, preferred_element_type=_F32) + bias
        s_ctx = lax.dot_general(q2, kcb_scr[...], _NT, preferred_element_type=_F32)
        m = jnp.maximum(jnp.max(s_loc, axis=-1, keepdims=True), jnp.max(s_ctx, axis=-1, keepdims=True))
        p_loc = jnp.exp(s_loc - m).astype(_BF16)
        p_ctx = jnp.exp(s_ctx - m).astype(_BF16)
        oe = _dot(p_loc, vb_scr[pl.ds(k0, win), :]) + _dot(p_ctx, vcb_scr[...])
        on = oe[:, :hp] / oe[:, hp:]
        o = jnp.where(head0, on[:GRID_W], on[GRID_W:])
        o_ref[pl.ds(q0, GRID_W), :] = (o * _silu(ga_ref[pl.ds(q0, GRID_W), :].astype(_F32))).astype(_BF16)
        return carry

    lax.fori_loop(0, rows, row, 0, unroll=16)


def _na_call(p3, cache_k4, cache_v4, bias_tbl, layer):
    b, l, _ = p3.shape
    rows = l // GRID_W
    assert rows >= NA_KH and rows % 2 == 0
    hp = 2 * NA_HEAD_DIM
    nblk = N_MAIN // hp // 16
    lc = cache_k4.shape[2]

    def spec(col):
        return pl.BlockSpec((None, l, hp), lambda i, j: (i, 0, col * nblk + j))

    cspec = pl.BlockSpec((None, None, lc, hp), lambda i, j: (i, layer, 0, j))
    return pl.pallas_call(
        functools.partial(_na_body, rows=rows),
        grid=(b, NA_HEADS // 2),
        in_specs=[spec(COL_Q), spec(COL_K), spec(COL_V), spec(COL_GA), cspec, cspec,
                  pl.BlockSpec((2, 2 * NA_KH - 2, GRID_W, 2 * GRID_W), lambda i, j: (j, 0, 0, 0))],
        out_specs=pl.BlockSpec((None, l, hp), lambda i, j: (i, 0, j)),
        out_shape=jax.ShapeDtypeStruct((b, l, NA_W), _BF16),
        scratch_shapes=[pltpu.VMEM((l, 2 * hp), _BF16),
                        pltpu.VMEM((lc, hp), _BF16), pltpu.VMEM((lc, 2 * hp), _BF16)],
        compiler_params=_params("arbitrary", "arbitrary"),
        name="na_attn",
    )(p3, p3, p3, p3, cache_k4, cache_v4, bias_tbl)


HEADS_PER_GROUP = SSM_HEADS // SSM_GROUPS
GROUP_W = HEADS_PER_GROUP * SSM_HEADDIM
SSD_CHUNKS_PER_STEP = 2
SSD_HALO = 16


def _split3(x):
    hi = x.astype(_BF16)
    r1 = x - hi.astype(_F32)
    mid = r1.astype(_BF16)
    lo = (r1 - mid.astype(_F32)).astype(_BF16)
    return hi, mid, lo


def _silu_tanh(x):
    h = 0.5 * x
    return h + h * jnp.tanh(h)


def _ssd_chunk(xs_b, bm, cm, dtraw, dtb_row_ref, alog_row_ref, dtb_col_ref, alog_col_ref, ht_scr, reverse):
    q = SSM_CHUNK
    d = 1 if reverse else 0
    last = 0 if reverse else q - 1

    dta_col = _softplus(dtraw + dtb_row_ref[...]) * (-jnp.exp(alog_row_ref[...]))
    dt_row_all = _softplus(dtraw.T + dtb_col_ref[...])
    dta_row_all = dt_row_all * (-jnp.exp(alog_col_ref[...]))
    dt_row = dt_row_all[d * SSM_HEADS:(d + 1) * SSM_HEADS, :]
    dta_row = dta_row_all[d * SSM_HEADS:(d + 1) * SSM_HEADS, :]

    ri = lax.broadcasted_iota(jnp.int32, (q, q), 0)
    ci = lax.broadcasted_iota(jnp.int32, (q, q), 1)
    causal = (ci >= ri) if reverse else (ci <= ri)
    t_col = jnp.where(causal, 1.0, 0.0).astype(_BF16)
    t_row = jnp.where((ri >= ci) if reverse else (ri <= ci), 1.0, 0.0).astype(_BF16)
    acum_col = sum(_dot(t_col, part) for part in _split3(dta_col))
    acum_row = sum(_dot(part, t_row) for part in _split3(dta_row))

    a_last = jnp.broadcast_to(acum_row[:, last:last + 1], (SSM_HEADS, q))
    dtde_row = dt_row * jnp.exp(a_last - acum_row)
    cdecay = jnp.exp(a_last)

    lo_half = lax.broadcasted_iota(jnp.int32, (1, q), 1) < SSM_HEADDIM
    lane_head = lax.broadcasted_iota(jnp.int32, (1, GROUP_W), 1) // SSM_HEADDIM
    zero_b = jnp.zeros((q, GROUP_W), _BF16)

    def per_head_lanes(vals):
        return jnp.concatenate([jnp.where(lo_half, vals[0], vals[1]), jnp.where(lo_half, vals[2], vals[3])], axis=1)

    ys = []
    for g in range(SSM_GROUPS):
        nsl = slice(g * SSM_STATE, (g + 1) * SSM_STATE)
        b_g = bm[:, nsl]
        c_g = cm[:, nsl]
        cb = lax.dot_general(c_g, b_g, _NT, preferred_element_type=_F32)
        b_t = b_g.astype(_F32).T
        xs_g = xs_b[:, g * GROUP_W:(g + 1) * GROUP_W]
        m_parts, bt_parts, xbd_parts, bcs = [], [], [], []
        for hh in range(HEADS_PER_GROUP):
            h = g * HEADS_PER_GROUP + hh
            lane = d * SSM_HEADS + h
            bc = jnp.broadcast_to(acum_col[:, lane:lane + 1], (q, q))
            lmat = jnp.exp(jnp.where(causal, bc - acum_row[h:h + 1, :], NEG))
            m_parts.append((cb * lmat * dt_row[h:h + 1, :]).astype(_BF16))
            bt_parts.append((b_t * dtde_row[h:h + 1, :]).astype(_BF16))
            xbd_parts.append(jnp.where(lane_head == hh, xs_g, zero_b))
            bcs.append(bc)
        lhs = jnp.concatenate([jnp.concatenate(m_parts, axis=1), jnp.concatenate(bt_parts, axis=1)], axis=0)
        res = _dot(lhs, jnp.concatenate(xbd_parts, axis=0))
        h_t = ht_scr[g]
        y_off = _dot(c_g, h_t.astype(_BF16)) * jnp.exp(per_head_lanes(bcs))
        ys.append(res[:q] + y_off)
        h0 = g * HEADS_PER_GROUP
        cd = per_head_lanes([cdecay[h0 + hh:h0 + hh + 1, :] for hh in range(HEADS_PER_GROUP)])
        ht_scr[g] = h_t * cd + res[q:]
    return ys


def _ssd_state_io(c, nc, h0_ref, hout_ref, ht_scr):
    @pl.when(c == 0)
    def _():
        for g in range(SSM_GROUPS):
            if h0_ref is None:
                ht_scr[g] = jnp.zeros((SSM_STATE, GROUP_W), _F32)
            else:
                hs = h0_ref[g * HEADS_PER_GROUP:(g + 1) * HEADS_PER_GROUP]
                ht_scr[g] = hs.reshape(GROUP_W, SSM_STATE).T

    def store_final():
        @pl.when(c == nc - 1)
        def _():
            for g in range(SSM_GROUPS):
                hout_ref[g * HEADS_PER_GROUP:(g + 1) * HEADS_PER_GROUP] = ht_scr[g].T.reshape(
                    HEADS_PER_GROUP, SSM_HEADDIM, SSM_STATE)

    return store_final


def _ssd_fwd_body(*refs, has_h0, nc):
    it = iter(refs)
    x_ref, prev_ref, next_ref, dt_ref, cw_ref, cb_ref = (next(it) for _ in range(6))
    dec_refs = [next(it) for _ in range(4)]
    dskip_ref = next(it)
    h0_ref = next(it) if has_h0 else None
    y_ref, act_ref, hout_ref, ht_scr = (next(it) for _ in range(4))
    q = SSM_CHUNK
    qb = x_ref.shape[0]
    c = pl.program_id(1)
    store_final = _ssd_state_io(c, nc, h0_ref, hout_ref, ht_scr)

    x = x_ref[...].astype(_F32)
    xp = jnp.where(c > 0, prev_ref[SSD_HALO - 1:SSD_HALO, :].astype(_F32), 0.0)
    xn = jnp.where(c < nc - 1, next_ref[0:1, :].astype(_F32), 0.0)
    sub = lax.broadcasted_iota(jnp.int32, (8, 1), 0)
    x_m1 = pltpu.roll(x, 1, axis=0)
    x_m1 = jnp.concatenate([jnp.where(sub == 0, xp, x_m1[:8]), x_m1[8:]], axis=0)
    x_p1 = pltpu.roll(x, qb - 1, axis=0)
    x_p1 = jnp.concatenate([x_p1[:qb - 8], jnp.where(sub == 7, xn, x_p1[qb - 8:])], axis=0)
    act = _silu_tanh(cw_ref[0:1, :] * x_m1 + cw_ref[1:2, :] * x + cw_ref[2:3, :] * x_p1 + cb_ref[...])
    act_b = act.astype(_BF16)
    act_ref[...] = act_b

    for s in range(qb // q):
        rows = slice(s * q, (s + 1) * q)
        ys = _ssd_chunk(act_b[rows, :SSM_W], act_b[rows, SSM_W:SSM_W + SSM_GROUPS * SSM_STATE],
                        act_b[rows, SSM_W + SSM_GROUPS * SSM_STATE:], dt_ref[rows, :], *dec_refs, ht_scr, False)
        y_ref[rows, :] = jnp.concatenate(ys, axis=1) + act[rows, :SSM_W] * dskip_ref[...]
    store_final()


def _ssd_bwd_body(*refs, has_h0, nc):
    it = iter(refs)
    act_ref, dt_ref = next(it), next(it)
    dec_refs = [next(it) for _ in range(4)]
    yf_ref, z_ref, gssm_ref = next(it), next(it), next(it)
    h0_ref = next(it) if has_h0 else None
    o_ref, hout_ref, ht_scr = next(it), next(it), next(it)
    c = pl.program_id(1)
    store_final = _ssd_state_io(c, nc, h0_ref, hout_ref, ht_scr)

    q = SSM_CHUNK
    for s in reversed(range(act_ref.shape[0] // q)):
        rows = slice(s * q, (s + 1) * q)
        ys = _ssd_chunk(act_ref[rows, :SSM_W], act_ref[rows, SSM_W:SSM_W + SSM_GROUPS * SSM_STATE],
                        act_ref[rows, SSM_W + SSM_GROUPS * SSM_STATE:], dt_ref[rows, :], *dec_refs, ht_scr, True)
        y = (yf_ref[rows, :] + jnp.concatenate(ys, axis=1)) * _silu_tanh(z_ref[rows, :].astype(_F32))
        y = y * lax.rsqrt(jnp.mean(y * y, axis=-1, keepdims=True) + EPS) * gssm_ref[...]
        o_ref[rows, :] = y.astype(_BF16)
    store_final()


def _ssd_calls(p3, dt3, prm, h0_f, h0_b, layer):
    b, l, _ = p3.shape
    q = SSD_CHUNKS_PER_STEP * SSM_CHUNK
    nc = l // q
    has_h0 = h0_f is not None
    xbc_blk = COL_XBC // 2

    def vec(w):
        return pl.BlockSpec((1, w), lambda i, c: (0, 0))

    dec_specs = [vec(DT_PAD), vec(DT_PAD),
                 pl.BlockSpec((DT_PAD, 1), lambda i, c: (0, 0)), pl.BlockSpec((DT_PAD, 1), lambda i, c: (0, 0))]
    dec_args = [prm["dtb_row"], prm["alog_row"], prm["dtb_col"], prm["alog_col"]]
    h0_spec = pl.BlockSpec((None, None, SSM_HEADS, SSM_HEADDIM, SSM_STATE), lambda i, c: (i, layer, 0, 0, 0))
    state_spec = pl.BlockSpec((None, SSM_HEADS, SSM_HEADDIM, SSM_STATE), lambda i, c: (i, 0, 0, 0))
    state_shape = jax.ShapeDtypeStruct((b, SSM_HEADS, SSM_HEADDIM, SSM_STATE), _F32)
    ht_scratch = pltpu.VMEM((SSM_GROUPS, SSM_STATE, GROUP_W), _F32)

    in_specs = [
        pl.BlockSpec((None, q, CONV_CH), lambda i, c: (i, c, xbc_blk)),
        pl.BlockSpec((None, SSD_HALO, CONV_CH),
                     lambda i, c: (i, jnp.maximum(c * (q // SSD_HALO) - 1, 0), xbc_blk)),
        pl.BlockSpec((None, SSD_HALO, CONV_CH),
                     lambda i, c: (i, jnp.minimum((c + 1) * (q // SSD_HALO), l // SSD_HALO - 1), xbc_blk)),
        pl.BlockSpec((None, q, DT_PAD), lambda i, c: (i, c, 0)),
        pl.BlockSpec((3, CONV_CH), lambda i, c: (0, 0)), vec(CONV_CH)] + dec_specs + [vec(SSM_W)]
    args = [p3, p3, p3, dt3, prm["conv_wt"], prm["conv_b"]] + dec_args + [prm["dskip"]]
    if has_h0:
        in_specs.append(h0_spec)
        args.append(h0_f)
    y_f, act, h_f = pl.pallas_call(
        functools.partial(_ssd_fwd_body, has_h0=has_h0, nc=nc),
        grid=(b, nc),
        in_specs=in_specs,
        out_specs=[pl.BlockSpec((None, q, SSM_W), lambda i, c: (i, c, 0)),
                   pl.BlockSpec((None, q, CONV_CH), lambda i, c: (i, c, 0)), state_spec],
        out_shape=[jax.ShapeDtypeStruct((b, l, SSM_W), _F32),
                   jax.ShapeDtypeStruct((b, l, CONV_CH), _BF16), state_shape],
        scratch_shapes=[ht_scratch],
        compiler_params=_params("arbitrary", "arbitrary"),
        name="ssd_fwd",
    )(*args)

    def rc(c):
        return nc - 1 - c

    in_specs = [pl.BlockSpec((None, q, CONV_CH), lambda i, c: (i, rc(c), 0)),
                pl.BlockSpec((None, q, DT_PAD), lambda i, c: (i, rc(c), 0))] + dec_specs + [
        pl.BlockSpec((None, q, SSM_W), lambda i, c: (i, rc(c), 0)),
        pl.BlockSpec((None, q, SSM_W), lambda i, c: (i, rc(c), COL_Z)), vec(SSM_W)]
    args = [act, dt3] + dec_args + [y_f, p3, prm["g_ssm"]]
    if has_h0:
        in_specs.append(h0_spec)
        args.append(h0_b)
    o_b, h_b = pl.pallas_call(
        functools.partial(_ssd_bwd_body, has_h0=has_h0, nc=nc),
        grid=(b, nc),
        in_specs=in_specs,
        out_specs=[pl.BlockSpec((None, q, SSM_W), lambda i, c: (i, rc(c), 0)), state_spec],
        out_shape=[jax.ShapeDtypeStruct((b, l, SSM_W), _BF16), state_shape],
        scratch_shapes=[ht_scratch],
        compiler_params=_params("arbitrary", "arbitrary"),
        name="ssd_bwd",
    )(*args)
    return o_b, h_f, h_b


def _sgu_tile(u_ref, v_ref, g_ref, gs_ref, ws_ref, bs_ref, o_ref):
    v = v_ref[...].astype(_F32)
    mu = jnp.mean(v, axis=-1, keepdims=True)
    vc = v - mu
    var = jnp.mean(vc * vc, axis=-1, keepdims=True)
    vn = (vc * lax.rsqrt(var + EPS) * gs_ref[...]).astype(_BF16)
    ge = SGU_W // SGU_GROUPS
    for ch in range(v.shape[0] // SGU_CHUNK):
        rsl = slice(ch * SGU_CHUNK, (ch + 1) * SGU_CHUNK)
        for g in range(SGU_GROUPS):
            csl = slice(g * ge, (g + 1) * ge)
            vs = _dot(ws_ref[g], vn[rsl, csl]) + bs_ref[:, g:g + 1]
            y = u_ref[rsl, csl].astype(_F32) * vs * _silu(g_ref[rsl, csl].astype(_F32))
            o_ref[rsl, csl] = y.astype(_BF16)


def _merge_body(oa_ref, ob_ref, u_ref, vc_ref, gc_ref, ga_ref, gb_ref, gcm_ref, x_ref, mod_ref, gpost_ref,
                gs_ref, ws_ref, bs_ref, wa_ref, wb_ref, wc_ref, wo_ref, out_ref, oc_scr):
    _sgu_tile(u_ref, vc_ref, gc_ref, gs_ref, ws_ref, bs_ref, oc_scr)
    merged = _sigmoid(ga_ref[...].astype(_F32)) * _dot(oa_ref[...], wa_ref[...])
    merged = merged + _sigmoid(gb_ref[...].astype(_F32)) * _dot(ob_ref[...], wb_ref[...])
    merged = merged + _sigmoid(gcm_ref[...].astype(_F32)) * _dot(oc_scr[...], wc_ref[...])
    y = _dot(merged.astype(_BF16), wo_ref[...])
    y = y * lax.rsqrt(jnp.mean(y * y, axis=-1, keepdims=True) + EPS) * gpost_ref[...]
    out_ref[...] = x_ref[...] + mod_ref[:, 2 * D_MODEL:] * y


def _merge_call(o_a, o_b, p2, x2d, mod_l, prm, mod_row):
    t = x2d.shape[0]
    tm = 256
    gm_blk = COL_GM // 2
    once = pl.Buffered(1)

    def row_spec(w, col=0):
        return pl.BlockSpec((tm, w), lambda i: (i, col))

    def const_spec(shape):
        return pl.BlockSpec(shape, lambda i: (0,) * len(shape))

    def w_spec(kdim):
        return pl.BlockSpec((kdim, D_MODEL), lambda i: (0, 0), pipeline_mode=once)

    return pl.pallas_call(
        _merge_body,
        grid=(t // tm,),
        in_specs=[row_spec(NA_W), row_spec(SSM_W),
                  row_spec(SGU_W, COL_U), row_spec(SGU_W, COL_VC), row_spec(SGU_W, COL_GC),
                  row_spec(D_MODEL, gm_blk), row_spec(D_MODEL, gm_blk + 1), row_spec(D_MODEL, gm_blk + 2),
                  row_spec(D_MODEL),
                  pl.BlockSpec((None, 1, 3 * D_MODEL), lambda i: (mod_row(i, tm), 0, 0)),
                  const_spec((1, D_MODEL)), const_spec((1, SGU_W)),
                  const_spec((SGU_GROUPS, SGU_CHUNK, SGU_CHUNK)), const_spec((SGU_CHUNK, SGU_GROUPS)),
                  w_spec(NA_W), w_spec(SSM_W), w_spec(SGU_W), w_spec(D_MODEL)],
        out_specs=row_spec(D_MODEL),
        out_shape=jax.ShapeDtypeStruct((t, D_MODEL), _F32),
        scratch_shapes=[pltpu.VMEM((tm, SGU_W), _BF16)],
        compiler_params=_params("arbitrary"),
        name="merge_out",
    )(o_a, o_b, p2, p2, p2, p2, p2, p2, x2d, mod_l, prm["g_post"], prm["g_sgu"], prm["w_s"], prm["b_s_t"],
      prm["w_br_a"], prm["w_br_b"], prm["w_br_c"], prm["w_out"])


def _layer(x3, mod_l, wts, mod_row, ctx, layer, caches=None):
    b, l, _ = x3.shape
    t = b * l
    x2d = x3.reshape(t, D_MODEL)
    p2, dt2, *kv = _inproj_call(x2d, mod_l, wts["g_pre"], wts["w_main"], layer, wts["w_dt"], mod_row,
                                want_kv=ctx is None)
    p3 = p2.reshape(b, l, N_MAIN)
    dt3 = dt2.reshape(b, l, DT_PAD)
    if ctx is None:
        o_a, caches = _ctx_attn_call(p3, kv[0].reshape(b, l, 2 * NA_W), layer, caches)
        h0_f = h0_b = None
    else:
        cache_k4, cache_v4, bias_tbl, h0_f, h0_b = ctx
        o_a = _na_call(p3, cache_k4, cache_v4, bias_tbl, layer)
    o_b, h_f, h_b = _ssd_calls(p3, dt3, wts, h0_f, h0_b, layer)
    y2d = _merge_call(o_a.reshape(t, NA_W), o_b.reshape(t, SSM_W), p2, x2d, mod_l, wts, mod_row)
    return y2d.reshape(b, l, D_MODEL), (caches, h_f, h_b)


def _layer_weights(l, g_pre, g_post, w_main, w_dt, conv_w, conv_b, dt_bias, a_log, d_skip, g_ssm, w_s, b_s, g_sgu,
                   w_br_a, w_br_b, w_br_c, w_out):
    dtb = jnp.pad(dt_bias[l].reshape(N_DT).astype(_F32), (0, DT_PAD - N_DT))
    alog = jnp.pad(a_log[l].reshape(N_DT).astype(_F32), (0, DT_PAD - N_DT))
    return {
        "g_pre": g_pre[l].reshape(1, D_MODEL), "g_post": g_post[l].reshape(1, D_MODEL),
        "w_main": w_main, "w_dt": w_dt,
        "conv_wt": conv_w[l].T, "conv_b": conv_b[l].reshape(1, CONV_CH),
        "dtb_row": dtb.reshape(1, DT_PAD), "alog_row": alog.reshape(1, DT_PAD),
        "dtb_col": dtb.reshape(DT_PAD, 1), "alog_col": alog.reshape(DT_PAD, 1),
        "dskip": jnp.repeat(d_skip[l].astype(_F32), SSM_HEADDIM).reshape(1, SSM_W),
        "g_ssm": g_ssm[l].reshape(1, SSM_W),
        "g_sgu": g_sgu[l].reshape(1, SGU_W), "w_s": w_s[l].astype(_BF16), "b_s_t": b_s[l].T,
        "w_br_a": w_br_a[l].astype(_BF16), "w_br_b": w_br_b[l].astype(_BF16),
        "w_br_c": w_br_c[l].astype(_BF16), "w_out": w_out[l].astype(_BF16),
    }


def kernel(x_prompt, x_sample, c, cache_k, cache_v, state_ssm_fwd, state_ssm_bwd, c_ctx, w_mod, b_mod, g_pre,
           g_post, w_in, rpb, conv_w, conv_b, dt_bias, a_log, d_skip, g_ssm, w_s, b_s, g_sgu, w_br_a, w_br_b,
           w_br_c, w_out):
    nb, ls, _ = x_sample.shape
    assert 1 + nb <= MOD_ROWS
    cvecs = jnp.concatenate([c_ctx[None, :], c, jnp.zeros((MOD_ROWS - 1 - nb, D_MODEL), _F32)], axis=0)
    mod = _mod_call(cvecs, w_mod, b_mod).reshape(DEPTH, MOD_ROWS, 1, 3 * D_MODEL)
    past = cache_k.shape[2]
    cache_k4 = cache_k.reshape(nb, DEPTH, past, NA_W)
    cache_v4 = cache_v.reshape(nb, DEPTH, past, NA_W)

    def prompt_row(i, tm):
        return 0

    def sample_row(i, tm):
        return 1 + (i * tm) // ls

    y_p, y_s = x_prompt, x_sample
    bp, lp, _ = x_prompt.shape
    caches = (jnp.zeros((bp, DEPTH, lp, NA_W), _F32), jnp.zeros((bp, DEPTH, lp, NA_W), _F32))
    hf_l, hb_l = [], []
    w_main, w_dt = _repack_call(jnp.swapaxes(w_in, 1, 2))
    for l in range(DEPTH):
        wts = _layer_weights(l, g_pre, g_post, w_main, w_dt, conv_w, conv_b, dt_bias, a_log, d_skip, g_ssm, w_s,
                             b_s, g_sgu, w_br_a, w_br_b, w_br_c, w_out)
        y_p, (caches, h_f, h_b) = _layer(y_p, mod[l], wts, prompt_row, None, l, caches)
        hf_l.append(h_f)
        hb_l.append(h_b)
        ctx = (cache_k4, cache_v4, _na_bias_table(rpb[l]), state_ssm_fwd, state_ssm_bwd)
        y_s, _ = _layer(y_s, mod[l], wts, sample_row, ctx, l)
    new_k = caches[0].reshape(bp, DEPTH, lp, NA_HEADS, NA_HEAD_DIM)
    new_v = caches[1].reshape(bp, DEPTH, lp, NA_HEADS, NA_HEAD_DIM)
    return (y_p, y_s, new_k, new_v, jnp.stack(hf_l, axis=1), jnp.stack(hb_l, axis=1))
```

```python
import functools

import jax
import jax.numpy as jnp
from jax import lax
from jax.experimental import pallas as pl
from jax.experimental.pallas import tpu as pltpu

D_MODEL = 2048
DEPTH = 2
EPS = 1e-6
GRID_W = 64
NA_HEAD_DIM = 64
NA_W = D_MODEL // 2
NA_HEADS = NA_W // NA_HEAD_DIM
NA_KH = 8
NA_KW = 16
SSM_HEADDIM = 64
SSM_W = D_MODEL // 2
SSM_HEADS = SSM_W // SSM_HEADDIM
SSM_GROUPS = 4
SSM_STATE = 128
SSM_CHUNK = 128
CONV_CH = SSM_W + 2 * SSM_GROUPS * SSM_STATE
SGU_W = D_MODEL // 2
SGU_GROUPS = 8
SGU_CHUNK = 128

N_MAIN = 16 * 1024
DT_PAD = 128
COL_Q, COL_K, COL_V, COL_GA, COL_XBC, COL_Z, COL_U, COL_VC, COL_GC, COL_GM = 0, 1, 2, 3, 4, 6, 7, 8, 9, 10

NEG = -1e30
LOG2E = 1.4426950408889634
MOD_ROWS = 8
VMEM_LIMIT = 56 * 1024 * 1024

_F32 = jnp.float32
_BF16 = jnp.bfloat16
_NT = (((1,), (1,)), ((), ()))


def _sigmoid(x):
    return 1.0 / (1.0 + jnp.exp(-x))


def _silu(x):
    return x * _sigmoid(x)


def _softplus(x):
    return jnp.maximum(x, 0.0) + jnp.log(1.0 + jnp.exp(-jnp.abs(x)))


def _dot(a, b):
    return jnp.dot(a, b, preferred_element_type=_F32)


def _params(*sem):
    return pltpu.CompilerParams(dimension_semantics=sem, vmem_limit_bytes=VMEM_LIMIT)


def _mod_body(c_ref, w_ref, b_ref, o_ref):
    s = _silu(c_ref[...]).astype(_BF16)
    o_ref[...] = _dot(s, w_ref[...].astype(_BF16)) + b_ref[...]


def _mod_call(cvecs, w_mod, b_mod):
    tn = 1024
    n3 = 3 * D_MODEL
    return pl.pallas_call(
        _mod_body,
        grid=(DEPTH, n3 // tn),
        in_specs=[pl.BlockSpec((MOD_ROWS, D_MODEL), lambda l, j: (0, 0)),
                  pl.BlockSpec((None, D_MODEL, tn), lambda l, j: (l, 0, j)),
                  pl.BlockSpec((None, 1, tn), lambda l, j: (l, 0, j))],
        out_specs=pl.BlockSpec((None, MOD_ROWS, tn), lambda l, j: (l, 0, j)),
        out_shape=jax.ShapeDtypeStruct((DEPTH, MOD_ROWS, n3), _F32),
        compiler_params=_params("arbitrary", "arbitrary"),
        name="modulation",
    )(cvecs, w_mod, b_mod.reshape(DEPTH, 1, n3))


N_DT = 2 * SSM_HEADS
OFF_DT = 4 * NA_W + CONV_CH + SSM_W
REPACK_BLK = 512


def _repack_body(a_ref, b_ref, o_ref, odt_ref):
    r = pl.program_id(1)
    cut = OFF_DT // REPACK_BLK

    @pl.when(r < cut)
    def _():
        o_ref[...] = a_ref[...].T.astype(_BF16)

    @pl.when(r >= cut)
    def _():
        o_ref[...] = jnp.concatenate([a_ref[N_DT:, :], b_ref[...]], axis=0).T.astype(_BF16)

    @pl.when(r == cut)
    def _():
        dt_rows = jnp.concatenate([a_ref[:N_DT, :], jnp.zeros((DT_PAD - N_DT, D_MODEL), _F32)], axis=0)
        odt_ref[...] = dt_rows.T.astype(_BF16)


def _repack_call(w_in_t):
    return pl.pallas_call(
        _repack_body,
        grid=(DEPTH, N_MAIN // REPACK_BLK),
        in_specs=[pl.BlockSpec((None, REPACK_BLK, D_MODEL), lambda l, r: (l, r, 0)),
                  pl.BlockSpec((None, N_DT, D_MODEL), lambda l, r: (l, (r + 1) * (REPACK_BLK // N_DT), 0))],
        out_specs=[pl.BlockSpec((None, D_MODEL, REPACK_BLK), lambda l, r: (l, 0, r)),
                   pl.BlockSpec((None, D_MODEL, DT_PAD), lambda l, r: (l, 0, 0))],
        out_shape=[jax.ShapeDtypeStruct((DEPTH, D_MODEL, N_MAIN), _BF16),
                   jax.ShapeDtypeStruct((DEPTH, D_MODEL, DT_PAD), _BF16)],
        compiler_params=_params("arbitrary", "arbitrary"),
        name="repack_w_in",
    )(w_in_t, w_in_t)


def _inproj_body(*refs, want_kv):
    x_ref, mod_ref, g_ref, w_ref, wdt_ref, o_ref, dt_ref = refs[:7]
    h_scr = refs[-1]
    j = pl.program_id(1)

    @pl.when(j == 0)
    def _():
        x = x_ref[...]
        xn = x * lax.rsqrt(jnp.mean(x * x, axis=-1, keepdims=True) + EPS) * g_ref[...]
        shift = mod_ref[:, 0:D_MODEL]
        scale = mod_ref[:, D_MODEL:2 * D_MODEL]
        h = (xn * (1.0 + scale) + shift).astype(_BF16)
        h_scr[...] = h
        dt_ref[...] = _dot(h, wdt_ref[...])

    res = _dot(h_scr[...], w_ref[...])
    o_ref[...] = res.astype(_BF16)
    if want_kv:
        kv_ref = refs[7]

        @pl.when((j == COL_K) | (j == COL_V))
        def _():
            kv_ref[...] = res


def _inproj_call(x2d, mod_l, g_pre_l, w_main, layer, w_dt, mod_row, want_kv):
    t = x2d.shape[0]
    tm = 1024
    tn = NA_W if want_kv else 2 * NA_W
    out_specs = [pl.BlockSpec((tm, tn), lambda i, j: (i, j)),
                 pl.BlockSpec((tm, DT_PAD), lambda i, j: (i, 0))]
    out_shape = [jax.ShapeDtypeStruct((t, N_MAIN), _BF16),
                 jax.ShapeDtypeStruct((t, DT_PAD), _F32)]
    if want_kv:
        out_specs.append(pl.BlockSpec((tm, NA_W), lambda i, j: (i, jnp.clip(j - COL_K, 0, COL_V - COL_K))))
        out_shape.append(jax.ShapeDtypeStruct((t, 2 * NA_W), _F32))
    return pl.pallas_call(
        functools.partial(_inproj_body, want_kv=want_kv),
        grid=(t // tm, N_MAIN // tn),
        in_specs=[pl.BlockSpec((tm, D_MODEL), lambda i, j: (i, 0)),
                  pl.BlockSpec((None, 1, 3 * D_MODEL), lambda i, j: (mod_row(i, tm), 0, 0)),
                  pl.BlockSpec((1, D_MODEL), lambda i, j: (0, 0)),
                  pl.BlockSpec((None, D_MODEL, tn), lambda i, j: (layer, 0, j)),
                  pl.BlockSpec((None, D_MODEL, DT_PAD), lambda i, j: (layer, 0, 0))],
        out_specs=out_specs,
        out_shape=out_shape,
        scratch_shapes=[pltpu.VMEM((tm, D_MODEL), _BF16)],
        compiler_params=_params("arbitrary", "arbitrary"),
        name="inproj",
    )(x2d, mod_l, g_pre_l, w_main, w_dt)


def _stack_heads(q, head0):
    q = q * (NA_HEAD_DIM ** -0.5)
    zero = jnp.zeros_like(q)
    return jnp.concatenate([jnp.where(head0, q, zero), jnp.where(head0, zero, q)], axis=0)


def _ctx_attn_body(q_ref, k_ref, v_ref, ga_ref, kf_ref, vf_ref, kbuf_ref, vbuf_ref, o_ref, ko_ref, vo_ref):
    del kbuf_ref, vbuf_ref
    l = q_ref.shape[0]
    hp = 2 * NA_HEAD_DIM
    ko_ref[...] = kf_ref[...]
    vo_ref[...] = vf_ref[...]
    head0 = lax.broadcasted_iota(jnp.int32, (l, hp), 1) < NA_HEAD_DIM
    ones = jnp.ones((l, hp), _BF16)
    for j in range(NA_HEADS // 2):
        cols = slice(j * hp, (j + 1) * hp)
        q2 = _stack_heads(q_ref[:, cols], head0)
        s = lax.dot_general(q2, k_ref[:, cols], _NT, preferred_element_type=_F32)
        p = jnp.exp(s - jnp.max(s, axis=-1, keepdims=True)).astype(_BF16)
        v_ext = jnp.concatenate([v_ref[:, cols], ones], axis=1)
        oe = _dot(p, v_ext)
        on = oe[:, :hp] / oe[:, hp:]
        o = jnp.where(head0, on[:l], on[l:])
        o_ref[:, cols] = (o * _silu(ga_ref[:, cols].astype(_F32))).astype(_BF16)


def _ctx_attn_call(p3, kv3, layer, caches):
    b, l, _ = p3.shape

    def spec(col):
        return pl.BlockSpec((None, l, NA_W), lambda i: (i, 0, col))

    cache_spec = pl.BlockSpec((None, None, l, NA_W), lambda i: (i, layer, 0, 0))
    cache_shape = jax.ShapeDtypeStruct((b, DEPTH, l, NA_W), _F32)
    any_spec = pl.BlockSpec(memory_space=pl.ANY)
    o_a, k_all, v_all = pl.pallas_call(
        _ctx_attn_body,
        grid=(b,),
        in_specs=[spec(COL_Q), spec(COL_K), spec(COL_V), spec(COL_GA), spec(0), spec(1), any_spec, any_spec],
        out_specs=[pl.BlockSpec((None, l, NA_W), lambda i: (i, 0, 0)), cache_spec, cache_spec],
        out_shape=[jax.ShapeDtypeStruct((b, l, NA_W), _BF16), cache_shape, cache_shape],
        input_output_aliases={6: 1, 7: 2},
        compiler_params=_params("arbitrary"),
        name="ctx_attn",
    )(p3, p3, p3, p3, kv3, kv3, *caches)
    return o_a, (k_all, v_all)


def _na_bias_table(rpb_l):
    nk = 2 * NA_KW - 1
    j = jnp.arange(GRID_W)[None, :, None]
    c = jnp.arange(GRID_W)[None, None, :]
    k = jnp.arange(nk)[:, None, None]
    cs = jnp.clip(j - NA_KW // 2, 0, GRID_W - NA_KW)
    valid = (c >= cs) & (c < cs + NA_KW)
    sel = (valid & (c - j + (NA_KW - 1) == k)).astype(_F32)
    zeros = jnp.zeros_like(sel)
    sel2 = jnp.concatenate([jnp.concatenate([sel, zeros], axis=-1), jnp.concatenate([zeros, sel], axis=-1)], axis=0)
    r = rpb_l.astype(_F32)
    r2 = jnp.concatenate([r[:, :-1], r[:, 1:]], axis=-1)
    tbl = jnp.einsum("hdm,mjn->hdjn", r2, sel2, precision=lax.Precision.HIGHEST)
    mask = jnp.where(valid, 0.0, NEG)[0]
    return tbl + jnp.concatenate([mask, mask], axis=-1)[None, None]


def _na_body(q_ref, k_ref, v_ref, ga_ref, kc_ref, vc_ref, bias_ref, o_ref, vb_scr, kcb_scr, vcb_scr, *, rows):
    hp = 2 * NA_HEAD_DIM
    win = NA_KH * GRID_W
    vb_scr[:, :hp] = v_ref[...]
    vb_scr[:, hp:] = jnp.ones((vb_scr.shape[0], hp), _BF16)
    kcb_scr[...] = kc_ref[...].astype(_BF16)
    vcb_scr[:, :hp] = vc_ref[...].astype(_BF16)
    vcb_scr[:, hp:] = jnp.ones((vcb_scr.shape[0], hp), _BF16)
    head0 = lax.broadcasted_iota(jnp.int32, (GRID_W, hp), 1) < NA_HEAD_DIM

    def row(r, carry):
        rs = jnp.clip(r - NA_KH // 2, 0, rows - NA_KH)
        dr0 = rs - r + (NA_KH - 1)
        q0 = pl.multiple_of(r * GRID_W, GRID_W)
        k0 = pl.multiple_of(rs * GRID_W, GRID_W)
        q2 = _stack_heads(q_ref[pl.ds(q0, GRID_W), :], head0)
        bias = jnp.concatenate(
            [jnp.concatenate([bias_ref[hh, dr0 + a] for a in range(0, NA_KH, 2)], axis=1) for hh in range(2)],
            axis=0)
        s_loc = lax.dot_general(q2, k_ref[pl.ds(k0, win), :], _NT, preferred_element_type=_F32) + bias
        s_ctx = lax.dot_general(q2, kcb_scr[...], _NT, preferred_element_type=_F32)
        m = jnp.maximum(jnp.max(s_loc, axis=-1, keepdims=True), jnp.max(s_ctx, axis=-1, keepdims=True))
        p_loc = jnp.exp(s_loc - m).astype(_BF16)
        p_ctx = jnp.exp(s_ctx - m).astype(_BF16)
        oe = _dot(p_loc, vb_scr[pl.ds(k0, win), :]) + _dot(p_ctx, vcb_scr[...])
        on = oe[:, :hp] / oe[:, hp:]
        o = jnp.where(head0, on[:GRID_W], on[GRID_W:])
        o_ref[pl.ds(q0, GRID_W), :] = (o * _silu(ga_ref[pl.ds(q0, GRID_W), :].astype(_F32))).astype(_BF16)
        return carry

    lax.fori_loop(0, rows, row, 0, unroll=32)


def _na_call(p3, cache_k4, cache_v4, bias_tbl, layer):
    b, l, _ = p3.shape
    rows = l // GRID_W
    assert rows >= NA_KH and rows % 2 == 0
    hp = 2 * NA_HEAD_DIM
    nblk = N_MAIN // hp // 16
    lc = cache_k4.shape[2]

    def spec(col):
        return pl.BlockSpec((None, l, hp), lambda i, j: (i, 0, col * nblk + j))

    cspec = pl.BlockSpec((None, None, lc, hp), lambda i, j: (i, layer, 0, j))
    return pl.pallas_call(
        functools.partial(_na_body, rows=rows),
        grid=(b, NA_HEADS // 2),
        in_specs=[spec(COL_Q), spec(COL_K), spec(COL_V), spec(COL_GA), cspec, cspec,
                  pl.BlockSpec((2, 2 * NA_KH - 2, GRID_W, 2 * GRID_W), lambda i, j: (j, 0, 0, 0))],
        out_specs=pl.BlockSpec((None, l, hp), lambda i, j: (i, 0, j)),
        out_shape=jax.ShapeDtypeStruct((b, l, NA_W), _BF16),
        scratch_shapes=[pltpu.VMEM((l, 2 * hp), _BF16),
                        pltpu.VMEM((lc, hp), _BF16), pltpu.VMEM((lc, 2 * hp), _BF16)],
        compiler_params=_params("arbitrary", "arbitrary"),
        name="na_attn",
    )(p3, p3, p3, p3, cache_k4, cache_v4, bias_tbl)


HEADS_PER_GROUP = SSM_HEADS // SSM_GROUPS
GROUP_W = HEADS_PER_GROUP * SSM_HEADDIM
SSD_CHUNKS_PER_STEP = 2
SSD_HALO = 16


def _split3(x):
    hi = x.astype(_BF16)
    r1 = x - hi.astype(_F32)
    mid = r1.astype(_BF16)
    lo = (r1 - mid.astype(_F32)).astype(_BF16)
    return hi, mid, lo


def _silu_tanh(x):
    h = 0.5 * x
    return h + h * jnp.tanh(h)


def _ssd_chunk(xs_b, bm, cm, dtraw, dtb_row_ref, alog_row_ref, dtb_col_ref, alog_col_ref, ht_scr, reverse):
    q = SSM_CHUNK
    d = 1 if reverse else 0
    last = 0 if reverse else q - 1

    dta_col = _softplus(dtraw + dtb_row_ref[...]) * (-jnp.exp(alog_row_ref[...]))
    dt_row_all = _softplus(dtraw.T + dtb_col_ref[...])
    dta_row_all = dt_row_all * (-jnp.exp(alog_col_ref[...]))
    dt_row = dt_row_all[d * SSM_HEADS:(d + 1) * SSM_HEADS, :]
    dta_row = dta_row_all[d * SSM_HEADS:(d + 1) * SSM_HEADS, :]

    ri = lax.broadcasted_iota(jnp.int32, (q, q), 0)
    ci = lax.broadcasted_iota(jnp.int32, (q, q), 1)
    causal = (ci >= ri) if reverse else (ci <= ri)
    t_col = jnp.where(causal, 1.0, 0.0).astype(_BF16)
    t_row = jnp.where((ri >= ci) if reverse else (ri <= ci), 1.0, 0.0).astype(_BF16)
    acum_col = sum(_dot(t_col, part) for part in _split3(dta_col)) * LOG2E
    acum_row = sum(_dot(part, t_row) for part in _split3(dta_row)) * LOG2E

    a_last = jnp.broadcast_to(acum_row[:, last:last + 1], (SSM_HEADS, q))
    dtde_row = dt_row * jnp.exp2(a_last - acum_row)
    cdecay = jnp.exp2(a_last)

    lo_half = lax.broadcasted_iota(jnp.int32, (1, q), 1) < SSM_HEADDIM
    lane_head = lax.broadcasted_iota(jnp.int32, (1, GROUP_W), 1) // SSM_HEADDIM
    zero_b = jnp.zeros((q, GROUP_W), _BF16)

    def per_head_lanes(vals):
        return jnp.concatenate([jnp.where(lo_half, vals[0], vals[1]), jnp.where(lo_half, vals[2], vals[3])], axis=1)

    ys = []
    for g in range(SSM_GROUPS):
        nsl = slice(g * SSM_STATE, (g + 1) * SSM_STATE)
        b_g = bm[:, nsl]
        c_g = cm[:, nsl]
        cb = lax.dot_general(c_g, b_g, _NT, preferred_element_type=_F32)
        b_t = b_g.astype(_F32).T
        xs_g = xs_b[:, g * GROUP_W:(g + 1) * GROUP_W]
        m_parts, bt_parts, xbd_parts, bcs = [], [], [], []
        for hh in range(HEADS_PER_GROUP):
            h = g * HEADS_PER_GROUP + hh
            lane = d * SSM_HEADS + h
            bc = jnp.broadcast_to(acum_col[:, lane:lane + 1], (q, q))
            lmat = jnp.exp2(jnp.where(causal, bc - acum_row[h:h + 1, :], NEG))
            m_parts.append((cb * lmat * dt_row[h:h + 1, :]).astype(_BF16))
            bt_parts.append((b_t * dtde_row[h:h + 1, :]).astype(_BF16))
            xbd_parts.append(jnp.where(lane_head == hh, xs_g, zero_b))
            bcs.append(bc)
        lhs = jnp.concatenate([jnp.concatenate(m_parts, axis=1), jnp.concatenate(bt_parts, axis=1)], axis=0)
        res = _dot(lhs, jnp.concatenate(xbd_parts, axis=0))
        h_t = ht_scr[g]
        y_off = _dot(c_g, h_t.astype(_BF16)) * jnp.exp2(per_head_lanes(bcs))
        ys.append(res[:q] + y_off)
        h0 = g * HEADS_PER_GROUP
        cd = per_head_lanes([cdecay[h0 + hh:h0 + hh + 1, :] for hh in range(HEADS_PER_GROUP)])
        ht_scr[g] = h_t * cd + res[q:]
    return ys


def _ssd_state_io(c, nc, h0_ref, hout_ref, ht_scr):
    @pl.when(c == 0)
    def _():
        for g in range(SSM_GROUPS):
            if h0_ref is None:
                ht_scr[g] = jnp.zeros((SSM_STATE, GROUP_W), _F32)
            else:
                hs = h0_ref[g * HEADS_PER_GROUP:(g + 1) * HEADS_PER_GROUP]
                ht_scr[g] = hs.reshape(GROUP_W, SSM_STATE).T

    def store_final():
        @pl.when(c == nc - 1)
        def _():
            for g in range(SSM_GROUPS):
                hout_ref[g * HEADS_PER_GROUP:(g + 1) * HEADS_PER_GROUP] = ht_scr[g].T.reshape(
                    HEADS_PER_GROUP, SSM_HEADDIM, SSM_STATE)

    return store_final


def _ssd_fwd_body(*refs, has_h0, nc):
    it = iter(refs)
    x_ref, prev_ref, next_ref, dt_ref, cw_ref, cb_ref = (next(it) for _ in range(6))
    dec_refs = [next(it) for _ in range(4)]
    dskip_ref = next(it)
    h0_ref = next(it) if has_h0 else None
    y_ref, act_ref, hout_ref, ht_scr = (next(it) for _ in range(4))
    q = SSM_CHUNK
    qb = x_ref.shape[0]
    c = pl.program_id(1)
    store_final = _ssd_state_io(c, nc, h0_ref, hout_ref, ht_scr)

    x = x_ref[...].astype(_F32)
    xp = jnp.where(c > 0, prev_ref[SSD_HALO - 1:SSD_HALO, :].astype(_F32), 0.0)
    xn = jnp.where(c < nc - 1, next_ref[0:1, :].astype(_F32), 0.0)
    sub = lax.broadcasted_iota(jnp.int32, (8, 1), 0)
    x_m1 = pltpu.roll(x, 1, axis=0)
    x_m1 = jnp.concatenate([jnp.where(sub == 0, xp, x_m1[:8]), x_m1[8:]], axis=0)
    x_p1 = pltpu.roll(x, qb - 1, axis=0)
    x_p1 = jnp.concatenate([x_p1[:qb - 8], jnp.where(sub == 7, xn, x_p1[qb - 8:])], axis=0)
    act = _silu_tanh(cw_ref[0:1, :] * x_m1 + cw_ref[1:2, :] * x + cw_ref[2:3, :] * x_p1 + cb_ref[...])
    act_b = act.astype(_BF16)
    act_ref[...] = act_b

    for s in range(qb // q):
        rows = slice(s * q, (s + 1) * q)
        ys = _ssd_chunk(act_b[rows, :SSM_W], act_b[rows, SSM_W:SSM_W + SSM_GROUPS * SSM_STATE],
                        act_b[rows, SSM_W + SSM_GROUPS * SSM_STATE:], dt_ref[rows, :], *dec_refs, ht_scr, False)
        y_ref[rows, :] = jnp.concatenate(ys, axis=1) + act[rows, :SSM_W] * dskip_ref[...]
    store_final()


def _ssd_bwd_body(*refs, has_h0, nc):
    it = iter(refs)
    act_ref, dt_ref = next(it), next(it)
    dec_refs = [next(it) for _ in range(4)]
    yf_ref, z_ref, gssm_ref = next(it), next(it), next(it)
    h0_ref = next(it) if has_h0 else None
    o_ref, hout_ref, ht_scr = next(it), next(it), next(it)
    c = pl.program_id(1)
    store_final = _ssd_state_io(c, nc, h0_ref, hout_ref, ht_scr)

    q = SSM_CHUNK
    for s in reversed(range(act_ref.shape[0] // q)):
        rows = slice(s * q, (s + 1) * q)
        ys = _ssd_chunk(act_ref[rows, :SSM_W], act_ref[rows, SSM_W:SSM_W + SSM_GROUPS * SSM_STATE],
                        act_ref[rows, SSM_W + SSM_GROUPS * SSM_STATE:], dt_ref[rows, :], *dec_refs, ht_scr, True)
        y = (yf_ref[rows, :] + jnp.concatenate(ys, axis=1)) * _silu_tanh(z_ref[rows, :].astype(_F32))
        y = y * lax.rsqrt(jnp.mean(y * y, axis=-1, keepdims=True) + EPS) * gssm_ref[...]
        o_ref[rows, :] = y.astype(_BF16)
    store_final()


def _ssd_calls(p3, dt3, prm, h0_f, h0_b, layer):
    b, l, _ = p3.shape
    q = SSD_CHUNKS_PER_STEP * SSM_CHUNK
    nc = l // q
    has_h0 = h0_f is not None
    xbc_blk = COL_XBC // 2

    def vec(w):
        return pl.BlockSpec((1, w), lambda i, c: (0, 0))

    dec_specs = [vec(DT_PAD), vec(DT_PAD),
                 pl.BlockSpec((DT_PAD, 1), lambda i, c: (0, 0)), pl.BlockSpec((DT_PAD, 1), lambda i, c: (0, 0))]
    dec_args = [prm["dtb_row"], prm["alog_row"], prm["dtb_col"], prm["alog_col"]]
    h0_spec = pl.BlockSpec((None, None, SSM_HEADS, SSM_HEADDIM, SSM_STATE), lambda i, c: (i, layer, 0, 0, 0))
    state_spec = pl.BlockSpec((None, SSM_HEADS, SSM_HEADDIM, SSM_STATE), lambda i, c: (i, 0, 0, 0))
    state_shape = jax.ShapeDtypeStruct((b, SSM_HEADS, SSM_HEADDIM, SSM_STATE), _F32)
    ht_scratch = pltpu.VMEM((SSM_GROUPS, SSM_STATE, GROUP_W), _F32)

    in_specs = [
        pl.BlockSpec((None, q, CONV_CH), lambda i, c: (i, c, xbc_blk)),
        pl.BlockSpec((None, SSD_HALO, CONV_CH),
                     lambda i, c: (i, jnp.maximum(c * (q // SSD_HALO) - 1, 0), xbc_blk)),
        pl.BlockSpec((None, SSD_HALO, CONV_CH),
                     lambda i, c: (i, jnp.minimum((c + 1) * (q // SSD_HALO), l // SSD_HALO - 1), xbc_blk)),
        pl.BlockSpec((None, q, DT_PAD), lambda i, c: (i, c, 0)),
        pl.BlockSpec((3, CONV_CH), lambda i, c: (0, 0)), vec(CONV_CH)] + dec_specs + [vec(SSM_W)]
    args = [p3, p3, p3, dt3, prm["conv_wt"], prm["conv_b"]] + dec_args + [prm["dskip"]]
    if has_h0:
        in_specs.append(h0_spec)
        args.append(h0_f)
    y_f, act, h_f = pl.pallas_call(
        functools.partial(_ssd_fwd_body, has_h0=has_h0, nc=nc),
        grid=(b, nc),
        in_specs=in_specs,
        out_specs=[pl.BlockSpec((None, q, SSM_W), lambda i, c: (i, c, 0)),
                   pl.BlockSpec((None, q, CONV_CH), lambda i, c: (i, c, 0)), state_spec],
        out_shape=[jax.ShapeDtypeStruct((b, l, SSM_W), _F32),
                   jax.ShapeDtypeStruct((b, l, CONV_CH), _BF16), state_shape],
        scratch_shapes=[ht_scratch],
        compiler_params=_params("arbitrary", "arbitrary"),
        name="ssd_fwd",
    )(*args)

    def rc(c):
        return nc - 1 - c

    in_specs = [pl.BlockSpec((None, q, CONV_CH), lambda i, c: (i, rc(c), 0)),
                pl.BlockSpec((None, q, DT_PAD), lambda i, c: (i, rc(c), 0))] + dec_specs + [
        pl.BlockSpec((None, q, SSM_W), lambda i, c: (i, rc(c), 0)),
        pl.BlockSpec((None, q, SSM_W), lambda i, c: (i, rc(c), COL_Z)), vec(SSM_W)]
    args = [act, dt3] + dec_args + [y_f, p3, prm["g_ssm"]]
    if has_h0:
        in_specs.append(h0_spec)
        args.append(h0_b)
    o_b, h_b = pl.pallas_call(
        functools.partial(_ssd_bwd_body, has_h0=has_h0, nc=nc),
        grid=(b, nc),
        in_specs=in_specs,
        out_specs=[pl.BlockSpec((None, q, SSM_W), lambda i, c: (i, rc(c), 0)), state_spec],
        out_shape=[jax.ShapeDtypeStruct((b, l, SSM_W), _BF16), state_shape],
        scratch_shapes=[ht_scratch],
        compiler_params=_params("arbitrary", "arbitrary"),
        name="ssd_bwd",
    )(*args)
    return o_b, h_f, h_b


def _sgu_tile(u_ref, v_ref, g_ref, gs_ref, ws_ref, bs_ref, o_ref):
    v = v_ref[...].astype(_F32)
    mu = jnp.mean(v, axis=-1, keepdims=True)
    vc = v - mu
    var = jnp.mean(vc * vc, axis=-1, keepdims=True)
    vn = (vc * lax.rsqrt(var + EPS) * gs_ref[...]).astype(_BF16)
    ge = SGU_W // SGU_GROUPS
    for ch in range(v.shape[0] // SGU_CHUNK):
        rsl = slice(ch * SGU_CHUNK, (ch + 1) * SGU_CHUNK)
        for g in range(SGU_GROUPS):
            csl = slice(g * ge, (g + 1) * ge)
            vs = _dot(ws_ref[g], vn[rsl, csl]) + bs_ref[:, g:g + 1]
            y = u_ref[rsl, csl].astype(_F32) * vs * _silu(g_ref[rsl, csl].astype(_F32))
            o_ref[rsl, csl] = y.astype(_BF16)


def _merge_body(oa_ref, ob_ref, u_ref, vc_ref, gc_ref, ga_ref, gb_ref, gcm_ref, x_ref, mod_ref, gpost_ref,
                gs_ref, ws_ref, bs_ref, wa_ref, wb_ref, wc_ref, wo_ref, out_ref, oc_scr):
    _sgu_tile(u_ref, vc_ref, gc_ref, gs_ref, ws_ref, bs_ref, oc_scr)
    merged = _sigmoid(ga_ref[...].astype(_F32)) * _dot(oa_ref[...], wa_ref[...])
    merged = merged + _sigmoid(gb_ref[...].astype(_F32)) * _dot(ob_ref[...], wb_ref[...])
    merged = merged + _sigmoid(gcm_ref[...].astype(_F32)) * _dot(oc_scr[...], wc_ref[...])
    y = _dot(merged.astype(_BF16), wo_ref[...])
    y = y * lax.rsqrt(jnp.mean(y * y, axis=-1, keepdims=True) + EPS) * gpost_ref[...]
    out_ref[...] = x_ref[...] + mod_ref[:, 2 * D_MODEL:] * y


def _merge_call(o_a, o_b, p2, x2d, mod_l, prm, mod_row):
    t = x2d.shape[0]
    tm = 256
    gm_blk = COL_GM // 2
    once = pl.Buffered(1)

    def row_spec(w, col=0):
        return pl.BlockSpec((tm, w), lambda i: (i, col))

    def const_spec(shape):
        return pl.BlockSpec(shape, lambda i: (0,) * len(shape))

    def w_spec(kdim):
        return pl.BlockSpec((kdim, D_MODEL), lambda i: (0, 0), pipeline_mode=once)

    return pl.pallas_call(
        _merge_body,
        grid=(t // tm,),
        in_specs=[row_spec(NA_W), row_spec(SSM_W),
                  row_spec(SGU_W, COL_U), row_spec(SGU_W, COL_VC), row_spec(SGU_W, COL_GC),
                  row_spec(D_MODEL, gm_blk), row_spec(D_MODEL, gm_blk + 1), row_spec(D_MODEL, gm_blk + 2),
                  row_spec(D_MODEL),
                  pl.BlockSpec((None, 1, 3 * D_MODEL), lambda i: (mod_row(i, tm), 0, 0)),
                  const_spec((1, D_MODEL)), const_spec((1, SGU_W)),
                  const_spec((SGU_GROUPS, SGU_CHUNK, SGU_CHUNK)), const_spec((SGU_CHUNK, SGU_GROUPS)),
                  w_spec(NA_W), w_spec(SSM_W), w_spec(SGU_W), w_spec(D_MODEL)],
        out_specs=row_spec(D_MODEL),
        out_shape=jax.ShapeDtypeStruct((t, D_MODEL), _F32),
        scratch_shapes=[pltpu.VMEM((tm, SGU_W), _BF16)],
        compiler_params=_params("arbitrary"),
        name="merge_out",
    )(o_a, o_b, p2, p2, p2, p2, p2, p2, x2d, mod_l, prm["g_post"], prm["g_sgu"], prm["w_s"], prm["b_s_t"],
      prm["w_br_a"], prm["w_br_b"], prm["w_br_c"], prm["w_out"])


def _layer(x3, mod_l, wts, mod_row, ctx, layer, caches=None):
    b, l, _ = x3.shape
    t = b * l
    x2d = x3.reshape(t, D_MODEL)
    p2, dt2, *kv = _inproj_call(x2d, mod_l, wts["g_pre"], wts["w_main"], layer, wts["w_dt"], mod_row,
                                want_kv=ctx is None)
    p3 = p2.reshape(b, l, N_MAIN)
    dt3 = dt2.reshape(b, l, DT_PAD)
    if ctx is None:
        o_a, caches = _ctx_attn_call(p3, kv[0].reshape(b, l, 2 * NA_W), layer, caches)
        h0_f = h0_b = None
    else:
        cache_k4, cache_v4, bias_tbl, h0_f, h0_b = ctx
        o_a = _na_call(p3, cache_k4, cache_v4, bias_tbl, layer)
    o_b, h_f, h_b = _ssd_calls(p3, dt3, wts, h0_f, h0_b, layer)
    y2d = _merge_call(o_a.reshape(t, NA_W), o_b.reshape(t, SSM_W), p2, x2d, mod_l, wts, mod_row)
    return y2d.reshape(b, l, D_MODEL), (caches, h_f, h_b)


def _layer_weights(l, g_pre, g_post, w_main, w_dt, conv_w, conv_b, dt_bias, a_log, d_skip, g_ssm, w_s, b_s, g_sgu,
                   w_br_a, w_br_b, w_br_c, w_out):
    dtb = jnp.pad(dt_bias[l].reshape(N_DT).astype(_F32), (0, DT_PAD - N_DT))
    alog = jnp.pad(a_log[l].reshape(N_DT).astype(_F32), (0, DT_PAD - N_DT))
    return {
        "g_pre": g_pre[l].reshape(1, D_MODEL), "g_post": g_post[l].reshape(1, D_MODEL),
        "w_main": w_main, "w_dt": w_dt,
        "conv_wt": conv_w[l].T, "conv_b": conv_b[l].reshape(1, CONV_CH),
        "dtb_row": dtb.reshape(1, DT_PAD), "alog_row": alog.reshape(1, DT_PAD),
        "dtb_col": dtb.reshape(DT_PAD, 1), "alog_col": alog.reshape(DT_PAD, 1),
        "dskip": jnp.repeat(d_skip[l].astype(_F32), SSM_HEADDIM).reshape(1, SSM_W),
        "g_ssm": g_ssm[l].reshape(1, SSM_W),
        "g_sgu": g_sgu[l].reshape(1, SGU_W), "w_s": w_s[l].astype(_BF16), "b_s_t": b_s[l].T,
        "w_br_a": w_br_a[l].astype(_BF16), "w_br_b": w_br_b[l].astype(_BF16),
        "w_br_c": w_br_c[l].astype(_BF16), "w_out": w_out[l].astype(_BF16),
    }


def kernel(x_prompt, x_sample, c, cache_k, cache_v, state_ssm_fwd, state_ssm_bwd, c_ctx, w_mod, b_mod, g_pre,
           g_post, w_in, rpb, conv_w, conv_b, dt_bias, a_log, d_skip, g_ssm, w_s, b_s, g_sgu, w_br_a, w_br_b,
           w_br_c, w_out):
    nb, ls, _ = x_sample.shape
    assert 1 + nb <= MOD_ROWS
    cvecs = jnp.concatenate([c_ctx[None, :], c, jnp.zeros((MOD_ROWS - 1 - nb, D_MODEL), _F32)], axis=0)
    mod = _mod_call(cvecs, w_mod, b_mod).reshape(DEPTH, MOD_ROWS, 1, 3 * D_MODEL)
    past = cache_k.shape[2]
    cache_k4 = cache_k.reshape(nb, DEPTH, past, NA_W)
    cache_v4 = cache_v.reshape(nb, DEPTH, past, NA_W)

    def prompt_row(i, tm):
        return 0

    def sample_row(i, tm):
        return 1 + (i * tm) // ls

    y_p, y_s = x_prompt, x_sample
    bp, lp, _ = x_prompt.shape
    caches = (jnp.zeros((bp, DEPTH, lp, NA_W), _F32), jnp.zeros((bp, DEPTH, lp, NA_W), _F32))
    hf_l, hb_l = [], []
    w_main, w_dt = _repack_call(jnp.swapaxes(w_in, 1, 2))
    for l in range(DEPTH):
        wts = _layer_weights(l, g_pre, g_post, w_main, w_dt, conv_w, conv_b, dt_bias, a_log, d_skip, g_ssm, w_s,
                             b_s, g_sgu, w_br_a, w_br_b, w_br_c, w_out)
        y_p, (caches, h_f, h_b) = _layer(y_p, mod[l], wts, prompt_row, None, l, caches)
        hf_l.append(h_f)
        hb_l.append(h_b)
        ctx = (cache_k4, cache_v4, _na_bias_table(rpb[l]), state_ssm_fwd, state_ssm_bwd)
        y_s, _ = _layer(y_s, mod[l], wts, sample_row, ctx, l)
    new_k = caches[0].reshape(bp, DEPTH, lp, NA_HEADS, NA_HEAD_DIM)
    new_v = caches[1].reshape(bp, DEPTH, lp, NA_HEADS, NA_HEAD_DIM)
    return (y_p, y_s, new_k, new_v, jnp.stack(hf_l, axis=1), jnp.stack(hb_l, axis=1))
```

```python
import functools

import jax
import jax.numpy as jnp
from jax import lax
from jax.experimental import pallas as pl
from jax.experimental.pallas import tpu as pltpu

D_MODEL = 2048
DEPTH = 2
EPS = 1e-6
GRID_W = 64
NA_HEAD_DIM = 64
NA_W = D_MODEL // 2
NA_HEADS = NA_W // NA_HEAD_DIM
NA_KH = 8
NA_KW = 16
SSM_HEADDIM = 64
SSM_W = D_MODEL // 2
SSM_HEADS = SSM_W // SSM_HEADDIM
SSM_GROUPS = 4
SSM_STATE = 128
SSM_CHUNK = 128
CONV_CH = SSM_W + 2 * SSM_GROUPS * SSM_STATE
SGU_W = D_MODEL // 2
SGU_GROUPS = 8
SGU_CHUNK = 128

SEG_W = 1024
N_MAIN = 16 * SEG_W
DT_PAD = 128
COL_Q, COL_K, COL_V, COL_GA, COL_XBC, COL_Z, COL_U, COL_VC, COL_GC, COL_GM = 0, 1, 2, 3, 4, 6, 7, 8, 9, 10

NEG = -1e30
LOG2E = 1.4426950408889634
MOD_ROWS = 8

V7X_VMEM_BYTES = 64 * 1024 * 1024
VMEM_LIMIT = V7X_VMEM_BYTES * 7 // 8
INPROJ_TM = 1024
MERGE_TM = 256
MOD_TN = 1024

_F32 = jnp.float32
_BF16 = jnp.bfloat16
_NT = (((1,), (1,)), ((), ()))


def _sigmoid(x):
    return 1.0 / (1.0 + jnp.exp(-x))


def _silu(x):
    return x * _sigmoid(x)


def _softplus(x):
    return jnp.maximum(x, 0.0) + jnp.log(1.0 + jnp.exp(-jnp.abs(x)))


def _dot(a, b):
    return jnp.dot(a, b, preferred_element_type=_F32)


def _params(*sem):
    return pltpu.CompilerParams(dimension_semantics=sem, vmem_limit_bytes=VMEM_LIMIT)


def _mod_body(c_ref, w_ref, b_ref, o_ref):
    s = _silu(c_ref[...]).astype(_BF16)
    o_ref[...] = _dot(s, w_ref[...].astype(_BF16)) + b_ref[...]


def _mod_call(cvecs, w_mod, b_mod):
    tn = MOD_TN
    n3 = 3 * D_MODEL
    return pl.pallas_call(
        _mod_body,
        grid=(DEPTH, n3 // tn),
        in_specs=[pl.BlockSpec((MOD_ROWS, D_MODEL), lambda l, j: (0, 0)),
                  pl.BlockSpec((None, D_MODEL, tn), lambda l, j: (l, 0, j)),
                  pl.BlockSpec((None, 1, tn), lambda l, j: (l, 0, j))],
        out_specs=pl.BlockSpec((None, MOD_ROWS, tn), lambda l, j: (l, 0, j)),
        out_shape=jax.ShapeDtypeStruct((DEPTH, MOD_ROWS, n3), _F32),
        compiler_params=_params("arbitrary", "arbitrary"),
        name="modulation",
    )(cvecs, w_mod, b_mod.reshape(DEPTH, 1, n3))


N_DT = 2 * SSM_HEADS
OFF_DT = 4 * NA_W + CONV_CH + SSM_W
REPACK_BLK = 512


def _repack_body(a_ref, b_ref, o_ref, odt_ref):
    r = pl.program_id(1)
    cut = OFF_DT // REPACK_BLK

    @pl.when(r < cut)
    def _():
        o_ref[...] = a_ref[...].T.astype(_BF16)

    @pl.when(r >= cut)
    def _():
        o_ref[...] = jnp.concatenate([a_ref[N_DT:, :], b_ref[...]], axis=0).T.astype(_BF16)

    @pl.when(r == cut)
    def _():
        dt_rows = jnp.concatenate([a_ref[:N_DT, :], jnp.zeros((DT_PAD - N_DT, D_MODEL), _F32)], axis=0)
        odt_ref[...] = dt_rows.T.astype(_BF16)


def _repack_call(w_in_t):
    return pl.pallas_call(
        _repack_body,
        grid=(DEPTH, N_MAIN // REPACK_BLK),
        in_specs=[pl.BlockSpec((None, REPACK_BLK, D_MODEL), lambda l, r: (l, r, 0)),
                  pl.BlockSpec((None, N_DT, D_MODEL), lambda l, r: (l, (r + 1) * (REPACK_BLK // N_DT), 0))],
        out_specs=[pl.BlockSpec((None, D_MODEL, REPACK_BLK), lambda l, r: (l, 0, r)),
                   pl.BlockSpec((None, D_MODEL, DT_PAD), lambda l, r: (l, 0, 0))],
        out_shape=[jax.ShapeDtypeStruct((DEPTH, D_MODEL, N_MAIN), _BF16),
                   jax.ShapeDtypeStruct((DEPTH, D_MODEL, DT_PAD), _BF16)],
        compiler_params=_params("arbitrary", "arbitrary"),
        name="repack_w_in",
    )(w_in_t, w_in_t)


def _inproj_body(*refs, want_kv):
    x_ref, mod_ref, g_ref, w_ref, wdt_ref, o_ref, dt_ref = refs[:7]
    h_scr = refs[-1]
    j = pl.program_id(1)

    @pl.when(j == 0)
    def _():
        x = x_ref[...]
        xn = x * lax.rsqrt(jnp.mean(x * x, axis=-1, keepdims=True) + EPS) * g_ref[...]
        shift = mod_ref[:, 0:D_MODEL]
        scale = mod_ref[:, D_MODEL:2 * D_MODEL]
        h = (xn * (1.0 + scale) + shift).astype(_BF16)
        h_scr[...] = h
        dt_ref[...] = _dot(h, wdt_ref[...])

    res = _dot(h_scr[...], w_ref[...])
    o_ref[...] = res.astype(_BF16)
    if want_kv:
        kv_ref = refs[7]

        @pl.when((j == COL_K) | (j == COL_V))
        def _():
            kv_ref[...] = res


def _inproj_call(x2d, mod_l, g_pre_l, w_main, layer, w_dt, mod_row, want_kv):
    t = x2d.shape[0]
    tm = INPROJ_TM
    tn = SEG_W if want_kv else 2 * SEG_W
    out_specs = [pl.BlockSpec((tm, tn), lambda i, j: (i, j)),
                 pl.BlockSpec((tm, DT_PAD), lambda i, j: (i, 0))]
    out_shape = [jax.ShapeDtypeStruct((t, N_MAIN), _BF16),
                 jax.ShapeDtypeStruct((t, DT_PAD), _F32)]
    if want_kv:
        out_specs.append(pl.BlockSpec((tm, NA_W), lambda i, j: (i, jnp.clip(j - COL_K, 0, COL_V - COL_K))))
        out_shape.append(jax.ShapeDtypeStruct((t, 2 * NA_W), _F32))
    return pl.pallas_call(
        functools.partial(_inproj_body, want_kv=want_kv),
        grid=(t // tm, N_MAIN // tn),
        in_specs=[pl.BlockSpec((tm, D_MODEL), lambda i, j: (i, 0)),
                  pl.BlockSpec((None, 1, 3 * D_MODEL), lambda i, j: (mod_row(i, tm), 0, 0)),
                  pl.BlockSpec((1, D_MODEL), lambda i, j: (0, 0)),
                  pl.BlockSpec((None, D_MODEL, tn), lambda i, j: (layer, 0, j)),
                  pl.BlockSpec((None, D_MODEL, DT_PAD), lambda i, j: (layer, 0, 0))],
        out_specs=out_specs,
        out_shape=out_shape,
        scratch_shapes=[pltpu.VMEM((tm, D_MODEL), _BF16)],
        compiler_params=_params("arbitrary", "arbitrary"),
        name="inproj",
    )(x2d, mod_l, g_pre_l, w_main, w_dt)


def _stack_heads(q, head0):
    q = q * (NA_HEAD_DIM ** -0.5)
    zero = jnp.zeros_like(q)
    return jnp.concatenate([jnp.where(head0, q, zero), jnp.where(head0, zero, q)], axis=0)


def _ctx_attn_body(q_ref, k_ref, v_ref, ga_ref, kf_ref, vf_ref, kbuf_ref, vbuf_ref, o_ref, ko_ref, vo_ref):
    del kbuf_ref, vbuf_ref
    l = q_ref.shape[0]
    hp = 2 * NA_HEAD_DIM
    ko_ref[...] = kf_ref[...]
    vo_ref[...] = vf_ref[...]
    head0 = lax.broadcasted_iota(jnp.int32, (l, hp), 1) < NA_HEAD_DIM
    ones = jnp.ones((l, hp), _BF16)
    for j in range(NA_HEADS // 2):
        cols = slice(j * hp, (j + 1) * hp)
        q2 = _stack_heads(q_ref[:, cols], head0)
        s = lax.dot_general(q2, k_ref[:, cols], _NT, preferred_element_type=_F32)
        p = jnp.exp(s - jnp.max(s, axis=-1, keepdims=True)).astype(_BF16)
        v_ext = jnp.concatenate([v_ref[:, cols], ones], axis=1)
        oe = _dot(p, v_ext)
        on = oe[:, :hp] / oe[:, hp:]
        o = jnp.where(head0, on[:l], on[l:])
        o_ref[:, cols] = (o * _silu(ga_ref[:, cols].astype(_F32))).astype(_BF16)


def _ctx_attn_call(p3, kv3, layer, caches):
    b, l, _ = p3.shape

    def spec(col):
        return pl.BlockSpec((None, l, NA_W), lambda i: (i, 0, col))

    cache_spec = pl.BlockSpec((None, None, l, NA_W), lambda i: (i, layer, 0, 0))
    cache_shape = jax.ShapeDtypeStruct((b, DEPTH, l, NA_W), _F32)
    any_spec = pl.BlockSpec(memory_space=pl.ANY)
    o_a, k_all, v_all = pl.pallas_call(
        _ctx_attn_body,
        grid=(b,),
        in_specs=[spec(COL_Q), spec(COL_K), spec(COL_V), spec(COL_GA), spec(0), spec(1), any_spec, any_spec],
        out_specs=[pl.BlockSpec((None, l, NA_W), lambda i: (i, 0, 0)), cache_spec, cache_spec],
        out_shape=[jax.ShapeDtypeStruct((b, l, NA_W), _BF16), cache_shape, cache_shape],
        input_output_aliases={6: 1, 7: 2},
        compiler_params=_params("arbitrary"),
        name="ctx_attn",
    )(p3, p3, p3, p3, kv3, kv3, *caches)
    return o_a, (k_all, v_all)


def _na_bias_table(rpb_l):
    nk = 2 * NA_KW - 1
    j = jnp.arange(GRID_W)[None, :, None]
    c = jnp.arange(GRID_W)[None, None, :]
    k = jnp.arange(nk)[:, None, None]
    cs = jnp.clip(j - NA_KW // 2, 0, GRID_W - NA_KW)
    valid = (c >= cs) & (c < cs + NA_KW)
    sel = (valid & (c - j + (NA_KW - 1) == k)).astype(_F32)
    zeros = jnp.zeros_like(sel)
    sel2 = jnp.concatenate([jnp.concatenate([sel, zeros], axis=-1), jnp.concatenate([zeros, sel], axis=-1)], axis=0)
    r = rpb_l.astype(_F32)
    r2 = jnp.concatenate([r[:, :-1], r[:, 1:]], axis=-1)
    tbl = jnp.einsum("hdm,mjn->hdjn", r2, sel2, precision=lax.Precision.HIGHEST)
    mask = jnp.where(valid, 0.0, NEG)[0]
    return tbl + jnp.concatenate([mask, mask], axis=-1)[None, None]


def _na_body(q_ref, k_ref, v_ref, ga_ref, kc_ref, vc_ref, bias_ref, o_ref, vb_scr, kcb_scr, vcb_scr, *, rows):
    hp = 2 * NA_HEAD_DIM
    win = NA_KH * GRID_W
    vb_scr[:, :hp] = v_ref[...]
    vb_scr[:, hp:] = jnp.ones((vb_scr.shape[0], hp), _BF16)
    kcb_scr[...] = kc_ref[...].astype(_BF16)
    vcb_scr[:, :hp] = vc_ref[...].astype(_BF16)
    vcb_scr[:, hp:] = jnp.ones((vcb_scr.shape[0], hp), _BF16)
    head0 = lax.broadcasted_iota(jnp.int32, (GRID_W, hp), 1) < NA_HEAD_DIM

    def row(r, carry):
        rs = jnp.clip(r - NA_KH // 2, 0, rows - NA_KH)
        dr0 = rs - r + (NA_KH - 1)
        q0 = pl.multiple_of(r * GRID_W, GRID_W)
        k0 = pl.multiple_of(rs * GRID_W, GRID_W)
        q2 = _stack_heads(q_ref[pl.ds(q0, GRID_W), :], head0)
        bias = jnp.concatenate(
            [jnp.concatenate([bias_ref[hh, dr0 + a] for a in range(0, NA_KH, 2)], axis=1) for hh in range(2)],
            axis=0)
        s_loc = lax.dot_general(q2, k_ref[pl.ds(k0, win), :], _NT, preferred_element_type=_F32) + bias
        s_ctx = lax.dot_general(q2, kcb_scr[...], _NT, preferred_element_type=_F32)
        m = jnp.maximum(jnp.max(s_loc, axis=-1, keepdims=True), jnp.max(s_ctx, axis=-1, keepdims=True))
        p_loc = jnp.exp(s_loc - m).astype(_BF16)
        p_ctx = jnp.exp(s_ctx - m).astype(_BF16)
        oe = _dot(p_loc, vb_scr[pl.ds(k0, win), :]) + _dot(p_ctx, vcb_scr[...])
        on = oe[:, :hp] / oe[:, hp:]
        o = jnp.where(head0, on[:GRID_W], on[GRID_W:])
        o_ref[pl.ds(q0, GRID_W), :] = (o * _silu(ga_ref[pl.ds(q0, GRID_W), :].astype(_F32))).astype(_BF16)
        return carry

    lax.fori_loop(0, rows, row, 0, unroll=32)


def _na_call(p3, cache_k4, cache_v4, bias_tbl, layer):
    b, l, _ = p3.shape
    rows = l // GRID_W
    assert rows >= NA_KH and rows % 2 == 0
    hp = 2 * NA_HEAD_DIM
    nblk = SEG_W // hp
    lc = cache_k4.shape[2]

    def spec(col):
        return pl.BlockSpec((None, l, hp), lambda i, j: (i, 0, col * nblk + j))

    cspec = pl.BlockSpec((None, None, lc, hp), lambda i, j: (i, layer, 0, j))
    return pl.pallas_call(
        functools.partial(_na_body, rows=rows),
        grid=(b, NA_HEADS // 2),
        in_specs=[spec(COL_Q), spec(COL_K), spec(COL_V), spec(COL_GA), cspec, cspec,
                  pl.BlockSpec((2, 2 * NA_KH - 2, GRID_W, 2 * GRID_W), lambda i, j: (j, 0, 0, 0))],
        out_specs=pl.BlockSpec((None, l, hp), lambda i, j: (i, 0, j)),
        out_shape=jax.ShapeDtypeStruct((b, l, NA_W), _BF16),
        scratch_shapes=[pltpu.VMEM((l, 2 * hp), _BF16),
                        pltpu.VMEM((lc, hp), _BF16), pltpu.VMEM((lc, 2 * hp), _BF16)],
        compiler_params=_params("arbitrary", "arbitrary"),
        name="na_attn",
    )(p3, p3, p3, p3, cache_k4, cache_v4, bias_tbl)


HEADS_PER_GROUP = SSM_HEADS // SSM_GROUPS
GROUP_W = HEADS_PER_GROUP * SSM_HEADDIM
SSD_CHUNKS_PER_STEP = 4
SSD_HALO = 16


def _split3(x):
    hi = x.astype(_BF16)
    r1 = x - hi.astype(_F32)
    mid = r1.astype(_BF16)
    lo = (r1 - mid.astype(_F32)).astype(_BF16)
    return hi, mid, lo


def _silu_tanh(x):
    h = 0.5 * x
    return h + h * jnp.tanh(h)


def _ssd_chunk(xs_b, bm, cm, dtraw, dtb_row_ref, alog_row_ref, dtb_col_ref, alog_col_ref, ht_scr, reverse):
    q = SSM_CHUNK
    d = 1 if reverse else 0
    last = 0 if reverse else q - 1

    hs = slice(d * SSM_HEADS, (d + 1) * SSM_HEADS)
    dta_col = _softplus(dtraw + dtb_row_ref[...]) * (-jnp.exp(alog_row_ref[...]))
    dt_row = _softplus(dtraw.T[hs, :] + dtb_col_ref[hs, :])
    dta_row = dt_row * (-jnp.exp(alog_col_ref[hs, :]))

    ri = lax.broadcasted_iota(jnp.int32, (q, q), 0)
    ci = lax.broadcasted_iota(jnp.int32, (q, q), 1)
    causal = (ci >= ri) if reverse else (ci <= ri)
    t_col = jnp.where(causal, 1.0, 0.0).astype(_BF16)
    t_row = jnp.where((ri >= ci) if reverse else (ri <= ci), 1.0, 0.0).astype(_BF16)
    acum_col = sum(_dot(t_col, part) for part in _split3(dta_col)) * LOG2E
    acum_row = sum(_dot(part, t_row) for part in _split3(dta_row)) * LOG2E

    a_last = jnp.broadcast_to(acum_row[:, last:last + 1], (SSM_HEADS, q))
    dtde_row = dt_row * jnp.exp2(a_last - acum_row)
    cdecay = jnp.exp2(a_last)

    lo_half = lax.broadcasted_iota(jnp.int32, (1, q), 1) < SSM_HEADDIM
    lane_head = lax.broadcasted_iota(jnp.int32, (1, GROUP_W), 1) // SSM_HEADDIM
    zero_b = jnp.zeros((q, GROUP_W), _BF16)

    def per_head_lanes(vals):
        return jnp.concatenate([jnp.where(lo_half, vals[0], vals[1]), jnp.where(lo_half, vals[2], vals[3])], axis=1)

    ys = []
    for g in range(SSM_GROUPS):
        nsl = slice(g * SSM_STATE, (g + 1) * SSM_STATE)
        b_g = bm[:, nsl]
        c_g = cm[:, nsl]
        cb = lax.dot_general(c_g, b_g, _NT, preferred_element_type=_F32)
        b_t = b_g.astype(_F32).T
        xs_g = xs_b[:, g * GROUP_W:(g + 1) * GROUP_W]
        m_parts, bt_parts, xbd_parts, bcs = [], [], [], []
        for hh in range(HEADS_PER_GROUP):
            h = g * HEADS_PER_GROUP + hh
            lane = d * SSM_HEADS + h
            bc = jnp.broadcast_to(acum_col[:, lane:lane + 1], (q, q))
            lmat = jnp.exp2(jnp.where(causal, bc - acum_row[h:h + 1, :], NEG))
            m_parts.append((cb * lmat * dt_row[h:h + 1, :]).astype(_BF16))
            bt_parts.append((b_t * dtde_row[h:h + 1, :]).astype(_BF16))
            xbd_parts.append(jnp.where(lane_head == hh, xs_g, zero_b))
            bcs.append(bc)
        lhs = jnp.concatenate([jnp.concatenate(m_parts, axis=1), jnp.concatenate(bt_parts, axis=1)], axis=0)
        res = _dot(lhs, jnp.concatenate(xbd_parts, axis=0))
        h_t = ht_scr[g]
        y_off = _dot(c_g, h_t.astype(_BF16)) * jnp.exp2(per_head_lanes(bcs))
        ys.append(res[:q] + y_off)
        h0 = g * HEADS_PER_GROUP
        cd = per_head_lanes([cdecay[h0 + hh:h0 + hh + 1, :] for hh in range(HEADS_PER_GROUP)])
        ht_scr[g] = h_t * cd + res[q:]
    return ys


def _ssd_state_io(c, nc, h0_ref, hout_ref, ht_scr):
    @pl.when(c == 0)
    def _():
        for g in range(SSM_GROUPS):
            if h0_ref is None:
                ht_scr[g] = jnp.zeros((SSM_STATE, GROUP_W), _F32)
            else:
                hs = h0_ref[g * HEADS_PER_GROUP:(g + 1) * HEADS_PER_GROUP]
                ht_scr[g] = hs.reshape(GROUP_W, SSM_STATE).T

    def store_final():
        @pl.when(c == nc - 1)
        def _():
            for g in range(SSM_GROUPS):
                hout_ref[g * HEADS_PER_GROUP:(g + 1) * HEADS_PER_GROUP] = ht_scr[g].T.reshape(
                    HEADS_PER_GROUP, SSM_HEADDIM, SSM_STATE)

    return store_final


def _ssd_fwd_body(*refs, has_h0, nc):
    it = iter(refs)
    x_ref, prev_ref, next_ref, dt_ref, cw_ref, cb_ref = (next(it) for _ in range(6))
    dec_refs = [next(it) for _ in range(4)]
    dskip_ref = next(it)
    h0_ref = next(it) if has_h0 else None
    y_ref, act_ref, hout_ref, ht_scr = (next(it) for _ in range(4))
    q = SSM_CHUNK
    qb = x_ref.shape[0]
    c = pl.program_id(1)
    store_final = _ssd_state_io(c, nc, h0_ref, hout_ref, ht_scr)

    x = x_ref[...].astype(_F32)
    xp = jnp.where(c > 0, prev_ref[SSD_HALO - 1:SSD_HALO, :].astype(_F32), 0.0)
    xn = jnp.where(c < nc - 1, next_ref[0:1, :].astype(_F32), 0.0)
    sub = lax.broadcasted_iota(jnp.int32, (8, 1), 0)
    x_m1 = pltpu.roll(x, 1, axis=0)
    x_m1 = jnp.concatenate([jnp.where(sub == 0, xp, x_m1[:8]), x_m1[8:]], axis=0)
    x_p1 = pltpu.roll(x, qb - 1, axis=0)
    x_p1 = jnp.concatenate([x_p1[:qb - 8], jnp.where(sub == 7, xn, x_p1[qb - 8:])], axis=0)
    act = _silu_tanh(cw_ref[0:1, :] * x_m1 + cw_ref[1:2, :] * x + cw_ref[2:3, :] * x_p1 + cb_ref[...])
    act_b = act.astype(_BF16)
    act_ref[...] = act_b

    for s in range(qb // q):
        rows = slice(s * q, (s + 1) * q)
        ys = _ssd_chunk(act_b[rows, :SSM_W], act_b[rows, SSM_W:SSM_W + SSM_GROUPS * SSM_STATE],
                        act_b[rows, SSM_W + SSM_GROUPS * SSM_STATE:], dt_ref[rows, :], *dec_refs, ht_scr, False)
        y_ref[rows, :] = jnp.concatenate(ys, axis=1) + act[rows, :SSM_W] * dskip_ref[...]
    store_final()


def _ssd_bwd_body(*refs, has_h0, nc):
    it = iter(refs)
    act_ref, dt_ref = next(it), next(it)
    dec_refs = [next(it) for _ in range(4)]
    yf_ref, z_ref, gssm_ref = next(it), next(it), next(it)
    h0_ref = next(it) if has_h0 else None
    o_ref, hout_ref, ht_scr = next(it), next(it), next(it)
    c = pl.program_id(1)
    store_final = _ssd_state_io(c, nc, h0_ref, hout_ref, ht_scr)

    q = SSM_CHUNK
    for s in reversed(range(act_ref.shape[0] // q)):
        rows = slice(s * q, (s + 1) * q)
        ys = _ssd_chunk(act_ref[rows, :SSM_W], act_ref[rows, SSM_W:SSM_W + SSM_GROUPS * SSM_STATE],
                        act_ref[rows, SSM_W + SSM_GROUPS * SSM_STATE:], dt_ref[rows, :], *dec_refs, ht_scr, True)
        y = (yf_ref[rows, :] + jnp.concatenate(ys, axis=1)) * _silu_tanh(z_ref[rows, :].astype(_F32))
        y = y * lax.rsqrt(jnp.mean(y * y, axis=-1, keepdims=True) + EPS) * gssm_ref[...]
        o_ref[rows, :] = y.astype(_BF16)
    store_final()


def _ssd_calls(p3, dt3, prm, h0_f, h0_b, layer):
    b, l, _ = p3.shape
    q = min(SSD_CHUNKS_PER_STEP * SSM_CHUNK, l)
    nc = l // q
    has_h0 = h0_f is not None
    xbc_blk = COL_XBC // 2

    def vec(w):
        return pl.BlockSpec((1, w), lambda i, c: (0, 0))

    dec_specs = [vec(DT_PAD), vec(DT_PAD),
                 pl.BlockSpec((DT_PAD, 1), lambda i, c: (0, 0)), pl.BlockSpec((DT_PAD, 1), lambda i, c: (0, 0))]
    dec_args = [prm["dtb_row"], prm["alog_row"], prm["dtb_col"], prm["alog_col"]]
    h0_spec = pl.BlockSpec((None, None, SSM_HEADS, SSM_HEADDIM, SSM_STATE), lambda i, c: (i, layer, 0, 0, 0))
    state_spec = pl.BlockSpec((None, SSM_HEADS, SSM_HEADDIM, SSM_STATE), lambda i, c: (i, 0, 0, 0))
    state_shape = jax.ShapeDtypeStruct((b, SSM_HEADS, SSM_HEADDIM, SSM_STATE), _F32)
    ht_scratch = pltpu.VMEM((SSM_GROUPS, SSM_STATE, GROUP_W), _F32)

    in_specs = [
        pl.BlockSpec((None, q, CONV_CH), lambda i, c: (i, c, xbc_blk)),
        pl.BlockSpec((None, SSD_HALO, CONV_CH),
                     lambda i, c: (i, jnp.maximum(c * (q // SSD_HALO) - 1, 0), xbc_blk)),
        pl.BlockSpec((None, SSD_HALO, CONV_CH),
                     lambda i, c: (i, jnp.minimum((c + 1) * (q // SSD_HALO), l // SSD_HALO - 1), xbc_blk)),
        pl.BlockSpec((None, q, DT_PAD), lambda i, c: (i, c, 0)),
        pl.BlockSpec((3, CONV_CH), lambda i, c: (0, 0)), vec(CONV_CH)] + dec_specs + [vec(SSM_W)]
    args = [p3, p3, p3, dt3, prm["conv_wt"], prm["conv_b"]] + dec_args + [prm["dskip"]]
    if has_h0:
        in_specs.append(h0_spec)
        args.append(h0_f)
    y_f, act, h_f = pl.pallas_call(
        functools.partial(_ssd_fwd_body, has_h0=has_h0, nc=nc),
        grid=(b, nc),
        in_specs=in_specs,
        out_specs=[pl.BlockSpec((None, q, SSM_W), lambda i, c: (i, c, 0)),
                   pl.BlockSpec((None, q, CONV_CH), lambda i, c: (i, c, 0)), state_spec],
        out_shape=[jax.ShapeDtypeStruct((b, l, SSM_W), _F32),
                   jax.ShapeDtypeStruct((b, l, CONV_CH), _BF16), state_shape],
        scratch_shapes=[ht_scratch],
        compiler_params=_params("arbitrary", "arbitrary"),
        name="ssd_fwd",
    )(*args)

    def rc(c):
        return nc - 1 - c

    in_specs = [pl.BlockSpec((None, q, CONV_CH), lambda i, c: (i, rc(c), 0)),
                pl.BlockSpec((None, q, DT_PAD), lambda i, c: (i, rc(c), 0))] + dec_specs + [
        pl.BlockSpec((None, q, SSM_W), lambda i, c: (i, rc(c), 0)),
        pl.BlockSpec((None, q, SSM_W), lambda i, c: (i, rc(c), COL_Z)), vec(SSM_W)]
    args = [act, dt3] + dec_args + [y_f, p3, prm["g_ssm"]]
    if has_h0:
        in_specs.append(h0_spec)
        args.append(h0_b)
    o_b, h_b = pl.pallas_call(
        functools.partial(_ssd_bwd_body, has_h0=has_h0, nc=nc),
        grid=(b, nc),
        in_specs=in_specs,
        out_specs=[pl.BlockSpec((None, q, SSM_W), lambda i, c: (i, rc(c), 0)), state_spec],
        out_shape=[jax.ShapeDtypeStruct((b, l, SSM_W), _BF16), state_shape],
        scratch_shapes=[ht_scratch],
        compiler_params=_params("arbitrary", "arbitrary"),
        name="ssd_bwd",
    )(*args)
    return o_b, h_f, h_b


def _sgu_tile(u_ref, v_ref, g_ref, gs_ref, ws_ref, bs_ref, o_ref):
    v = v_ref[...].astype(_F32)
    mu = jnp.mean(v, axis=-1, keepdims=True)
    vc = v - mu
    var = jnp.mean(vc * vc, axis=-1, keepdims=True)
    vn = (vc * lax.rsqrt(var + EPS) * gs_ref[...]).astype(_BF16)
    ge = SGU_W // SGU_GROUPS
    for ch in range(v.shape[0] // SGU_CHUNK):
        rsl = slice(ch * SGU_CHUNK, (ch + 1) * SGU_CHUNK)
        for g in range(SGU_GROUPS):
            csl = slice(g * ge, (g + 1) * ge)
            vs = _dot(ws_ref[g], vn[rsl, csl]) + bs_ref[:, g:g + 1]
            y = u_ref[rsl, csl].astype(_F32) * vs * _silu(g_ref[rsl, csl].astype(_F32))
            o_ref[rsl, csl] = y.astype(_BF16)


def _merge_body(oa_ref, ob_ref, u_ref, vc_ref, gc_ref, ga_ref, gb_ref, gcm_ref, x_ref, mod_ref, gpost_ref,
                gs_ref, ws_ref, bs_ref, wa_ref, wb_ref, wc_ref, wo_ref, out_ref, oc_scr):
    _sgu_tile(u_ref, vc_ref, gc_ref, gs_ref, ws_ref, bs_ref, oc_scr)
    merged = _sigmoid(ga_ref[...].astype(_F32)) * _dot(oa_ref[...], wa_ref[...])
    merged = merged + _sigmoid(gb_ref[...].astype(_F32)) * _dot(ob_ref[...], wb_ref[...])
    merged = merged + _sigmoid(gcm_ref[...].astype(_F32)) * _dot(oc_scr[...], wc_ref[...])
    y = _dot(merged.astype(_BF16), wo_ref[...])
    y = y * lax.rsqrt(jnp.mean(y * y, axis=-1, keepdims=True) + EPS) * gpost_ref[...]
    out_ref[...] = x_ref[...] + mod_ref[:, 2 * D_MODEL:] * y


def _merge_call(o_a, o_b, p2, x2d, mod_l, prm, mod_row):
    t = x2d.shape[0]
    tm = MERGE_TM
    gm_blk = COL_GM // 2
    once = pl.Buffered(1)

    def row_spec(w, col=0):
        return pl.BlockSpec((tm, w), lambda i: (i, col))

    def const_spec(shape):
        return pl.BlockSpec(shape, lambda i: (0,) * len(shape))

    def w_spec(kdim):
        return pl.BlockSpec((kdim, D_MODEL), lambda i: (0, 0), pipeline_mode=once)

    return pl.pallas_call(
        _merge_body,
        grid=(t // tm,),
        in_specs=[row_spec(NA_W), row_spec(SSM_W),
                  row_spec(SGU_W, COL_U), row_spec(SGU_W, COL_VC), row_spec(SGU_W, COL_GC),
                  row_spec(D_MODEL, gm_blk), row_spec(D_MODEL, gm_blk + 1), row_spec(D_MODEL, gm_blk + 2),
                  row_spec(D_MODEL),
                  pl.BlockSpec((None, 1, 3 * D_MODEL), lambda i: (mod_row(i, tm), 0, 0)),
                  const_spec((1, D_MODEL)), const_spec((1, SGU_W)),
                  const_spec((SGU_GROUPS, SGU_CHUNK, SGU_CHUNK)), const_spec((SGU_CHUNK, SGU_GROUPS)),
                  w_spec(NA_W), w_spec(SSM_W), w_spec(SGU_W), w_spec(D_MODEL)],
        out_specs=row_spec(D_MODEL),
        out_shape=jax.ShapeDtypeStruct((t, D_MODEL), _F32),
        scratch_shapes=[pltpu.VMEM((tm, SGU_W), _BF16)],
        compiler_params=_params("arbitrary"),
        name="merge_out",
    )(o_a, o_b, p2, p2, p2, p2, p2, p2, x2d, mod_l, prm["g_post"], prm["g_sgu"], prm["w_s"], prm["b_s_t"],
      prm["w_br_a"], prm["w_br_b"], prm["w_br_c"], prm["w_out"])


def _layer(x3, mod_l, wts, mod_row, ctx, layer, caches=None):
    b, l, _ = x3.shape
    t = b * l
    x2d = x3.reshape(t, D_MODEL)
    p2, dt2, *kv = _inproj_call(x2d, mod_l, wts["g_pre"], wts["w_main"], layer, wts["w_dt"], mod_row,
                                want_kv=ctx is None)
    p3 = p2.reshape(b, l, N_MAIN)
    dt3 = dt2.reshape(b, l, DT_PAD)
    if ctx is None:
        o_a, caches = _ctx_attn_call(p3, kv[0].reshape(b, l, 2 * NA_W), layer, caches)
        h0_f = h0_b = None
    else:
        cache_k4, cache_v4, bias_tbl, h0_f, h0_b = ctx
        o_a = _na_call(p3, cache_k4, cache_v4, bias_tbl, layer)
    o_b, h_f, h_b = _ssd_calls(p3, dt3, wts, h0_f, h0_b, layer)
    y2d = _merge_call(o_a.reshape(t, NA_W), o_b.reshape(t, SSM_W), p2, x2d, mod_l, wts, mod_row)
    return y2d.reshape(b, l, D_MODEL), (caches, h_f, h_b)


def _layer_weights(l, g_pre, g_post, w_main, w_dt, conv_w, conv_b, dt_bias, a_log, d_skip, g_ssm, w_s, b_s, g_sgu,
                   w_br_a, w_br_b, w_br_c, w_out):
    dtb = jnp.pad(dt_bias[l].reshape(N_DT).astype(_F32), (0, DT_PAD - N_DT))
    alog = jnp.pad(a_log[l].reshape(N_DT).astype(_F32), (0, DT_PAD - N_DT))
    return {
        "g_pre": g_pre[l].reshape(1, D_MODEL), "g_post": g_post[l].reshape(1, D_MODEL),
        "w_main": w_main, "w_dt": w_dt,
        "conv_wt": conv_w[l].T, "conv_b": conv_b[l].reshape(1, CONV_CH),
        "dtb_row": dtb.reshape(1, DT_PAD), "alog_row": alog.reshape(1, DT_PAD),
        "dtb_col": dtb.reshape(DT_PAD, 1), "alog_col": alog.reshape(DT_PAD, 1),
        "dskip": jnp.repeat(d_skip[l].astype(_F32), SSM_HEADDIM).reshape(1, SSM_W),
        "g_ssm": g_ssm[l].reshape(1, SSM_W),
        "g_sgu": g_sgu[l].reshape(1, SGU_W), "w_s": w_s[l].astype(_BF16), "b_s_t": b_s[l].T,
        "w_br_a": w_br_a[l].astype(_BF16), "w_br_b": w_br_b[l].astype(_BF16),
        "w_br_c": w_br_c[l].astype(_BF16), "w_out": w_out[l].astype(_BF16),
    }


def kernel(x_prompt, x_sample, c, cache_k, cache_v, state_ssm_fwd, state_ssm_bwd, c_ctx, w_mod, b_mod, g_pre,
           g_post, w_in, rpb, conv_w, conv_b, dt_bias, a_log, d_skip, g_ssm, w_s, b_s, g_sgu, w_br_a, w_br_b,
           w_br_c, w_out):
    nb, ls, _ = x_sample.shape
    bp, lp, _ = x_prompt.shape
    assert 1 + nb <= MOD_ROWS
    assert ls % INPROJ_TM == 0 and (bp * lp) % INPROJ_TM == 0 and ls % MERGE_TM == 0
    assert lp % SSM_CHUNK == 0 and ls % (SSD_CHUNKS_PER_STEP * SSM_CHUNK) == 0 and MERGE_TM % SGU_CHUNK == 0
    assert ls % GRID_W == 0 and ls // GRID_W >= NA_KH
    cvecs = jnp.concatenate([c_ctx[None, :], c, jnp.zeros((MOD_ROWS - 1 - nb, D_MODEL), _F32)], axis=0)
    mod = _mod_call(cvecs, w_mod, b_mod).reshape(DEPTH, MOD_ROWS, 1, 3 * D_MODEL)
    past = cache_k.shape[2]
    cache_k4 = cache_k.reshape(nb, DEPTH, past, NA_W)
    cache_v4 = cache_v.reshape(nb, DEPTH, past, NA_W)

    def prompt_row(i, tm):
        return 0

    def sample_row(i, tm):
        return 1 + (i * tm) // ls

    y_p, y_s = x_prompt, x_sample
    caches = (jnp.zeros((bp, DEPTH, lp, NA_W), _F32), jnp.zeros((bp, DEPTH, lp, NA_W), _F32))
    hf_l, hb_l = [], []
    w_main, w_dt = _repack_call(jnp.swapaxes(w_in, 1, 2))
    for l in range(DEPTH):
        wts = _layer_weights(l, g_pre, g_post, w_main, w_dt, conv_w, conv_b, dt_bias, a_log, d_skip, g_ssm, w_s,
                             b_s, g_sgu, w_br_a, w_br_b, w_br_c, w_out)
        y_p, (caches, h_f, h_b) = _layer(y_p, mod[l], wts, prompt_row, None, l, caches)
        hf_l.append(h_f)
        hb_l.append(h_b)
        ctx = (cache_k4, cache_v4, _na_bias_table(rpb[l]), state_ssm_fwd, state_ssm_bwd)
        y_s, _ = _layer(y_s, mod[l], wts, sample_row, ctx, l)
    new_k = caches[0].reshape(bp, DEPTH, lp, NA_HEADS, NA_HEAD_DIM)
    new_v = caches[1].reshape(bp, DEPTH, lp, NA_HEADS, NA_HEAD_DIM)
    return (y_p, y_s, new_k, new_v, jnp.stack(hf_l, axis=1), jnp.stack(hb_l, axis=1))
```

```python
import functools

import jax
import jax.numpy as jnp
from jax import lax
from jax.experimental import pallas as pl
from jax.experimental.pallas import tpu as pltpu

D_MODEL = 2048
DEPTH = 2
EPS = 1e-6
GRID_W = 64
NA_HEAD_DIM = 64
NA_W = D_MODEL // 2
NA_HEADS = NA_W // NA_HEAD_DIM
NA_KH = 8
NA_KW = 16
SSM_HEADDIM = 64
SSM_W = D_MODEL // 2
SSM_HEADS = SSM_W // SSM_HEADDIM
SSM_GROUPS = 4
SSM_STATE = 128
SSM_CHUNK = 128
CONV_CH = SSM_W + 2 * SSM_GROUPS * SSM_STATE
SGU_W = D_MODEL // 2
SGU_GROUPS = 8
SGU_CHUNK = 128

SEG_W = 1024
N_MAIN = 16 * SEG_W
DT_PAD = 128
COL_Q, COL_K, COL_V, COL_GA, COL_XBC, COL_Z, COL_U, COL_VC, COL_GC, COL_GM = 0, 1, 2, 3, 4, 6, 7, 8, 9, 10

NEG = -1e30
LOG2E = 1.4426950408889634
MOD_ROWS = 8

V7X_VMEM_BYTES = 64 * 1024 * 1024
VMEM_LIMIT = V7X_VMEM_BYTES * 7 // 8
INPROJ_TM = 1024
MERGE_TM = 256
MOD_TN = 1024

_F32 = jnp.float32
_BF16 = jnp.bfloat16
_NT = (((1,), (1,)), ((), ()))


def _sigmoid(x):
    return 1.0 / (1.0 + jnp.exp(-x))


def _silu(x):
    return x * _sigmoid(x)


def _softplus(x):
    return jnp.maximum(x, 0.0) + jnp.log(1.0 + jnp.exp(-jnp.abs(x)))


def _dot(a, b):
    return jnp.dot(a, b, preferred_element_type=_F32)


def _params(*sem):
    return pltpu.CompilerParams(dimension_semantics=sem, vmem_limit_bytes=VMEM_LIMIT)


def _mod_body(c_ref, w_ref, b_ref, o_ref):
    s = _silu(c_ref[...]).astype(_BF16)
    o_ref[...] = _dot(s, w_ref[...].astype(_BF16)) + b_ref[...]


def _mod_call(cvecs, w_mod, b_mod):
    tn = MOD_TN
    n3 = 3 * D_MODEL
    return pl.pallas_call(
        _mod_body,
        grid=(DEPTH, n3 // tn),
        in_specs=[pl.BlockSpec((MOD_ROWS, D_MODEL), lambda l, j: (0, 0)),
                  pl.BlockSpec((None, D_MODEL, tn), lambda l, j: (l, 0, j)),
                  pl.BlockSpec((None, 1, tn), lambda l, j: (l, 0, j))],
        out_specs=pl.BlockSpec((None, MOD_ROWS, tn), lambda l, j: (l, 0, j)),
        out_shape=jax.ShapeDtypeStruct((DEPTH, MOD_ROWS, n3), _F32),
        compiler_params=_params("arbitrary", "arbitrary"),
        name="modulation",
    )(cvecs, w_mod, b_mod.reshape(DEPTH, 1, n3))


N_DT = 2 * SSM_HEADS
OFF_DT = 4 * NA_W + CONV_CH + SSM_W
REPACK_BLK = 512


def _repack_body(a_ref, b_ref, o_ref, odt_ref):
    r = pl.program_id(1)
    cut = OFF_DT // REPACK_BLK

    @pl.when(r < cut)
    def _():
        o_ref[...] = a_ref[...].T.astype(_BF16)

    @pl.when(r >= cut)
    def _():
        o_ref[...] = jnp.concatenate([a_ref[N_DT:, :], b_ref[...]], axis=0).T.astype(_BF16)

    @pl.when(r == cut)
    def _():
        dt_rows = jnp.concatenate([a_ref[:N_DT, :], jnp.zeros((DT_PAD - N_DT, D_MODEL), _F32)], axis=0)
        odt_ref[...] = dt_rows.T.astype(_BF16)


def _repack_call(w_in_t):
    return pl.pallas_call(
        _repack_body,
        grid=(DEPTH, N_MAIN // REPACK_BLK),
        in_specs=[pl.BlockSpec((None, REPACK_BLK, D_MODEL), lambda l, r: (l, r, 0)),
                  pl.BlockSpec((None, N_DT, D_MODEL), lambda l, r: (l, (r + 1) * (REPACK_BLK // N_DT), 0))],
        out_specs=[pl.BlockSpec((None, D_MODEL, REPACK_BLK), lambda l, r: (l, 0, r)),
                   pl.BlockSpec((None, D_MODEL, DT_PAD), lambda l, r: (l, 0, 0))],
        out_shape=[jax.ShapeDtypeStruct((DEPTH, D_MODEL, N_MAIN), _BF16),
                   jax.ShapeDtypeStruct((DEPTH, D_MODEL, DT_PAD), _BF16)],
        compiler_params=_params("arbitrary", "arbitrary"),
        name="repack_w_in",
    )(w_in_t, w_in_t)


def _inproj_body(*refs, want_kv):
    x_ref, mod_ref, g_ref, w_ref, wdt_ref, o_ref, dt_ref = refs[:7]
    h_scr = refs[-1]
    j = pl.program_id(1)

    @pl.when(j == 0)
    def _():
        x = x_ref[...]
        xn = x * lax.rsqrt(jnp.mean(x * x, axis=-1, keepdims=True) + EPS) * g_ref[...]
        shift = mod_ref[:, 0:D_MODEL]
        scale = mod_ref[:, D_MODEL:2 * D_MODEL]
        h = (xn * (1.0 + scale) + shift).astype(_BF16)
        h_scr[...] = h
        dt_ref[...] = _dot(h, wdt_ref[...])

    res = _dot(h_scr[...], w_ref[...])
    o_ref[...] = res.astype(_BF16)
    if want_kv:
        kv_ref = refs[7]

        @pl.when((j == COL_K) | (j == COL_V))
        def _():
            kv_ref[...] = res


def _inproj_call(x2d, mod_l, g_pre_l, w_main, layer, w_dt, mod_row, want_kv):
    t = x2d.shape[0]
    tm = INPROJ_TM
    tn = SEG_W if want_kv else 2 * SEG_W
    out_specs = [pl.BlockSpec((tm, tn), lambda i, j: (i, j)),
                 pl.BlockSpec((tm, DT_PAD), lambda i, j: (i, 0))]
    out_shape = [jax.ShapeDtypeStruct((t, N_MAIN), _BF16),
                 jax.ShapeDtypeStruct((t, DT_PAD), _F32)]
    if want_kv:
        out_specs.append(pl.BlockSpec((tm, NA_W), lambda i, j: (i, jnp.clip(j - COL_K, 0, COL_V - COL_K))))
        out_shape.append(jax.ShapeDtypeStruct((t, 2 * NA_W), _F32))
    return pl.pallas_call(
        functools.partial(_inproj_body, want_kv=want_kv),
        grid=(t // tm, N_MAIN // tn),
        in_specs=[pl.BlockSpec((tm, D_MODEL), lambda i, j: (i, 0)),
                  pl.BlockSpec((None, 1, 3 * D_MODEL), lambda i, j: (mod_row(i, tm), 0, 0)),
                  pl.BlockSpec((1, D_MODEL), lambda i, j: (0, 0)),
                  pl.BlockSpec((None, D_MODEL, tn), lambda i, j: (layer, 0, j)),
                  pl.BlockSpec((None, D_MODEL, DT_PAD), lambda i, j: (layer, 0, 0))],
        out_specs=out_specs,
        out_shape=out_shape,
        scratch_shapes=[pltpu.VMEM((tm, D_MODEL), _BF16)],
        compiler_params=_params("arbitrary", "arbitrary"),
        name="inproj",
    )(x2d, mod_l, g_pre_l, w_main, w_dt)


def _stack_heads(q, head0):
    q = q * (NA_HEAD_DIM ** -0.5)
    zero = jnp.zeros_like(q)
    return jnp.concatenate([jnp.where(head0, q, zero), jnp.where(head0, zero, q)], axis=0)


def _ctx_attn_body(q_ref, k_ref, v_ref, ga_ref, kf_ref, vf_ref, kbuf_ref, vbuf_ref, o_ref, ko_ref, vo_ref):
    del kbuf_ref, vbuf_ref
    l = q_ref.shape[0]
    hp = 2 * NA_HEAD_DIM
    ko_ref[...] = kf_ref[...]
    vo_ref[...] = vf_ref[...]
    head0 = lax.broadcasted_iota(jnp.int32, (l, hp), 1) < NA_HEAD_DIM
    ones = jnp.ones((l, hp), _BF16)
    for j in range(NA_HEADS // 2):
        cols = slice(j * hp, (j + 1) * hp)
        q2 = _stack_heads(q_ref[:, cols], head0)
        s = lax.dot_general(q2, k_ref[:, cols], _NT, preferred_element_type=_F32)
        p = jnp.exp(s - jnp.max(s, axis=-1, keepdims=True)).astype(_BF16)
        v_ext = jnp.concatenate([v_ref[:, cols], ones], axis=1)
        oe = _dot(p, v_ext)
        on = oe[:, :hp] / oe[:, hp:]
        o = jnp.where(head0, on[:l], on[l:])
        o_ref[:, cols] = (o * _silu(ga_ref[:, cols].astype(_F32))).astype(_BF16)


def _ctx_attn_call(p3, kv3, layer, caches):
    b, l, _ = p3.shape

    def spec(col):
        return pl.BlockSpec((None, l, NA_W), lambda i: (i, 0, col))

    cache_spec = pl.BlockSpec((None, None, l, NA_W), lambda i: (i, layer, 0, 0))
    cache_shape = jax.ShapeDtypeStruct((b, DEPTH, l, NA_W), _F32)
    any_spec = pl.BlockSpec(memory_space=pl.ANY)
    o_a, k_all, v_all = pl.pallas_call(
        _ctx_attn_body,
        grid=(b,),
        in_specs=[spec(COL_Q), spec(COL_K), spec(COL_V), spec(COL_GA), spec(0), spec(1), any_spec, any_spec],
        out_specs=[pl.BlockSpec((None, l, NA_W), lambda i: (i, 0, 0)), cache_spec, cache_spec],
        out_shape=[jax.ShapeDtypeStruct((b, l, NA_W), _BF16), cache_shape, cache_shape],
        input_output_aliases={6: 1, 7: 2},
        compiler_params=_params("arbitrary"),
        name="ctx_attn",
    )(p3, p3, p3, p3, kv3, kv3, *caches)
    return o_a, (k_all, v_all)


def _na_bias_table(rpb_l):
    nk = 2 * NA_KW - 1
    j = jnp.arange(GRID_W)[None, :, None]
    c = jnp.arange(GRID_W)[None, None, :]
    k = jnp.arange(nk)[:, None, None]
    cs = jnp.clip(j - NA_KW // 2, 0, GRID_W - NA_KW)
    valid = (c >= cs) & (c < cs + NA_KW)
    sel = (valid & (c - j + (NA_KW - 1) == k)).astype(_F32)
    zeros = jnp.zeros_like(sel)
    sel2 = jnp.concatenate([jnp.concatenate([sel, zeros], axis=-1), jnp.concatenate([zeros, sel], axis=-1)], axis=0)
    r = rpb_l.astype(_F32)
    r2 = jnp.concatenate([r[:, :-1], r[:, 1:]], axis=-1)
    tbl = jnp.einsum("hdm,mjn->hdjn", r2, sel2, precision=lax.Precision.HIGHEST)
    mask = jnp.where(valid, 0.0, NEG)[0]
    return tbl + jnp.concatenate([mask, mask], axis=-1)[None, None]


def _na_body(q_ref, k_ref, v_ref, ga_ref, kc_ref, vc_ref, bias_ref, o_ref, vb_scr, kcb_scr, vcb_scr, *, rows):
    hp = 2 * NA_HEAD_DIM
    win = NA_KH * GRID_W
    vb_scr[:, :hp] = v_ref[...]
    vb_scr[:, hp:] = jnp.ones((vb_scr.shape[0], hp), _BF16)
    kcb_scr[...] = kc_ref[...].astype(_BF16)
    vcb_scr[:, :hp] = vc_ref[...].astype(_BF16)
    vcb_scr[:, hp:] = jnp.ones((vcb_scr.shape[0], hp), _BF16)
    head0 = lax.broadcasted_iota(jnp.int32, (GRID_W, hp), 1) < NA_HEAD_DIM

    def row(r, carry):
        rs = jnp.clip(r - NA_KH // 2, 0, rows - NA_KH)
        dr0 = rs - r + (NA_KH - 1)
        q0 = pl.multiple_of(r * GRID_W, GRID_W)
        k0 = pl.multiple_of(rs * GRID_W, GRID_W)
        q2 = _stack_heads(q_ref[pl.ds(q0, GRID_W), :], head0)
        bias = jnp.concatenate(
            [jnp.concatenate([bias_ref[hh, dr0 + a] for a in range(0, NA_KH, 2)], axis=1) for hh in range(2)],
            axis=0)
        s_loc = lax.dot_general(q2, k_ref[pl.ds(k0, win), :], _NT, preferred_element_type=_F32) + bias
        s_ctx = lax.dot_general(q2, kcb_scr[...], _NT, preferred_element_type=_F32)
        m = jnp.maximum(jnp.max(s_loc, axis=-1, keepdims=True), jnp.max(s_ctx, axis=-1, keepdims=True))
        p_loc = jnp.exp(s_loc - m).astype(_BF16)
        p_ctx = jnp.exp(s_ctx - m).astype(_BF16)
        oe = _dot(p_loc, vb_scr[pl.ds(k0, win), :]) + _dot(p_ctx, vcb_scr[...])
        on = oe[:, :hp] / oe[:, hp:]
        o = jnp.where(head0, on[:GRID_W], on[GRID_W:])
        o_ref[pl.ds(q0, GRID_W), :] = (o * _silu(ga_ref[pl.ds(q0, GRID_W), :].astype(_F32))).astype(_BF16)
        return carry

    lax.fori_loop(0, rows, row, 0, unroll=True)


def _na_call(p3, cache_k4, cache_v4, bias_tbl, layer):
    b, l, _ = p3.shape
    rows = l // GRID_W
    assert rows >= NA_KH and rows % 2 == 0
    hp = 2 * NA_HEAD_DIM
    nblk = SEG_W // hp
    lc = cache_k4.shape[2]

    def spec(col):
        return pl.BlockSpec((None, l, hp), lambda i, j: (i, 0, col * nblk + j))

    cspec = pl.BlockSpec((None, None, lc, hp), lambda i, j: (i, layer, 0, j))
    return pl.pallas_call(
        functools.partial(_na_body, rows=rows),
        grid=(b, NA_HEADS // 2),
        in_specs=[spec(COL_Q), spec(COL_K), spec(COL_V), spec(COL_GA), cspec, cspec,
                  pl.BlockSpec((2, 2 * NA_KH - 2, GRID_W, 2 * GRID_W), lambda i, j: (j, 0, 0, 0))],
        out_specs=pl.BlockSpec((None, l, hp), lambda i, j: (i, 0, j)),
        out_shape=jax.ShapeDtypeStruct((b, l, NA_W), _BF16),
        scratch_shapes=[pltpu.VMEM((l, 2 * hp), _BF16),
                        pltpu.VMEM((lc, hp), _BF16), pltpu.VMEM((lc, 2 * hp), _BF16)],
        compiler_params=_params("arbitrary", "arbitrary"),
        name="na_attn",
    )(p3, p3, p3, p3, cache_k4, cache_v4, bias_tbl)


HEADS_PER_GROUP = SSM_HEADS // SSM_GROUPS
GROUP_W = HEADS_PER_GROUP * SSM_HEADDIM
SSD_CHUNKS_PER_STEP = 4
SSD_HALO = 16


def _split3(x):
    hi = x.astype(_BF16)
    r1 = x - hi.astype(_F32)
    mid = r1.astype(_BF16)
    lo = (r1 - mid.astype(_F32)).astype(_BF16)
    return hi, mid, lo


def _silu_tanh(x):
    h = 0.5 * x
    return h + h * jnp.tanh(h)


def _ssd_chunk(xs_b, bm, cm, dtraw, dtb_row_ref, alog_row_ref, dtb_col_ref, alog_col_ref, ht_scr, reverse):
    q = SSM_CHUNK
    d = 1 if reverse else 0
    last = 0 if reverse else q - 1

    hs = slice(d * SSM_HEADS, (d + 1) * SSM_HEADS)
    dta_col = _softplus(dtraw + dtb_row_ref[...]) * (-jnp.exp(alog_row_ref[...]))
    dt_row = _softplus(dtraw.T[hs, :] + dtb_col_ref[hs, :])
    dta_row = dt_row * (-jnp.exp(alog_col_ref[hs, :]))

    ri = lax.broadcasted_iota(jnp.int32, (q, q), 0)
    ci = lax.broadcasted_iota(jnp.int32, (q, q), 1)
    causal = (ci >= ri) if reverse else (ci <= ri)
    t_col = jnp.where(causal, 1.0, 0.0).astype(_BF16)
    t_row = jnp.where((ri >= ci) if reverse else (ri <= ci), 1.0, 0.0).astype(_BF16)
    acum_col = sum(_dot(t_col, part) for part in _split3(dta_col)) * LOG2E
    acum_row = sum(_dot(part, t_row) for part in _split3(dta_row)) * LOG2E

    a_last = jnp.broadcast_to(acum_row[:, last:last + 1], (SSM_HEADS, q))
    dtde_row = dt_row * jnp.exp2(a_last - acum_row)
    cdecay = jnp.exp2(a_last)

    lo_half = lax.broadcasted_iota(jnp.int32, (1, q), 1) < SSM_HEADDIM
    lane_head = lax.broadcasted_iota(jnp.int32, (1, GROUP_W), 1) // SSM_HEADDIM
    zero_b = jnp.zeros((q, GROUP_W), _BF16)

    def per_head_lanes(vals):
        return jnp.concatenate([jnp.where(lo_half, vals[0], vals[1]), jnp.where(lo_half, vals[2], vals[3])], axis=1)

    ys = []
    for g in range(SSM_GROUPS):
        nsl = slice(g * SSM_STATE, (g + 1) * SSM_STATE)
        b_g = bm[:, nsl]
        c_g = cm[:, nsl]
        cb = lax.dot_general(c_g, b_g, _NT, preferred_element_type=_F32)
        b_t = b_g.astype(_F32).T
        xs_g = xs_b[:, g * GROUP_W:(g + 1) * GROUP_W]
        m_parts, bt_parts, xbd_parts, bcs = [], [], [], []
        for hh in range(HEADS_PER_GROUP):
            h = g * HEADS_PER_GROUP + hh
            lane = d * SSM_HEADS + h
            bc = jnp.broadcast_to(acum_col[:, lane:lane + 1], (q, q))
            lmat = jnp.exp2(jnp.where(causal, bc - acum_row[h:h + 1, :], NEG))
            m_parts.append((cb * lmat * dt_row[h:h + 1, :]).astype(_BF16))
            bt_parts.append((b_t * dtde_row[h:h + 1, :]).astype(_BF16))
            xbd_parts.append(jnp.where(lane_head == hh, xs_g, zero_b))
            bcs.append(bc)
        lhs = jnp.concatenate([jnp.concatenate(m_parts, axis=1), jnp.concatenate(bt_parts, axis=1)], axis=0)
        res = _dot(lhs, jnp.concatenate(xbd_parts, axis=0))
        h_t = ht_scr[g]
        y_off = _dot(c_g, h_t.astype(_BF16)) * jnp.exp2(per_head_lanes(bcs))
        ys.append(res[:q] + y_off)
        h0 = g * HEADS_PER_GROUP
        cd = per_head_lanes([cdecay[h0 + hh:h0 + hh + 1, :] for hh in range(HEADS_PER_GROUP)])
        ht_scr[g] = h_t * cd + res[q:]
    return ys


def _ssd_state_io(c, nc, h0_ref, hout_ref, ht_scr):
    @pl.when(c == 0)
    def _():
        for g in range(SSM_GROUPS):
            if h0_ref is None:
                ht_scr[g] = jnp.zeros((SSM_STATE, GROUP_W), _F32)
            else:
                hs = h0_ref[g * HEADS_PER_GROUP:(g + 1) * HEADS_PER_GROUP]
                ht_scr[g] = hs.reshape(GROUP_W, SSM_STATE).T

    def store_final():
        @pl.when(c == nc - 1)
        def _():
            for g in range(SSM_GROUPS):
                hout_ref[g * HEADS_PER_GROUP:(g + 1) * HEADS_PER_GROUP] = ht_scr[g].T.reshape(
                    HEADS_PER_GROUP, SSM_HEADDIM, SSM_STATE)

    return store_final


def _ssd_fwd_body(*refs, has_h0, nc):
    it = iter(refs)
    x_ref, prev_ref, next_ref, dt_ref, cw_ref, cb_ref = (next(it) for _ in range(6))
    dec_refs = [next(it) for _ in range(4)]
    dskip_ref = next(it)
    h0_ref = next(it) if has_h0 else None
    y_ref, act_ref, hout_ref, ht_scr = (next(it) for _ in range(4))
    q = SSM_CHUNK
    qb = x_ref.shape[0]
    c = pl.program_id(1)
    store_final = _ssd_state_io(c, nc, h0_ref, hout_ref, ht_scr)

    x = x_ref[...].astype(_F32)
    xp = jnp.where(c > 0, prev_ref[SSD_HALO - 1:SSD_HALO, :].astype(_F32), 0.0)
    xn = jnp.where(c < nc - 1, next_ref[0:1, :].astype(_F32), 0.0)
    sub = lax.broadcasted_iota(jnp.int32, (8, 1), 0)
    x_m1 = pltpu.roll(x, 1, axis=0)
    x_m1 = jnp.concatenate([jnp.where(sub == 0, xp, x_m1[:8]), x_m1[8:]], axis=0)
    x_p1 = pltpu.roll(x, qb - 1, axis=0)
    x_p1 = jnp.concatenate([x_p1[:qb - 8], jnp.where(sub == 7, xn, x_p1[qb - 8:])], axis=0)
    act = _silu_tanh(cw_ref[0:1, :] * x_m1 + cw_ref[1:2, :] * x + cw_ref[2:3, :] * x_p1 + cb_ref[...])
    act_b = act.astype(_BF16)
    act_ref[...] = act_b

    for s in range(qb // q):
        rows = slice(s * q, (s + 1) * q)
        ys = _ssd_chunk(act_b[rows, :SSM_W], act_b[rows, SSM_W:SSM_W + SSM_GROUPS * SSM_STATE],
                        act_b[rows, SSM_W + SSM_GROUPS * SSM_STATE:], dt_ref[rows, :], *dec_refs, ht_scr, False)
        y_ref[rows, :] = jnp.concatenate(ys, axis=1) + act[rows, :SSM_W] * dskip_ref[...]
    store_final()


def _ssd_bwd_body(*refs, has_h0, nc):
    it = iter(refs)
    act_ref, dt_ref = next(it), next(it)
    dec_refs = [next(it) for _ in range(4)]
    yf_ref, z_ref, gssm_ref = next(it), next(it), next(it)
    h0_ref = next(it) if has_h0 else None
    o_ref, hout_ref, ht_scr = next(it), next(it), next(it)
    c = pl.program_id(1)
    store_final = _ssd_state_io(c, nc, h0_ref, hout_ref, ht_scr)

    q = SSM_CHUNK
    for s in reversed(range(act_ref.shape[0] // q)):
        rows = slice(s * q, (s + 1) * q)
        ys = _ssd_chunk(act_ref[rows, :SSM_W], act_ref[rows, SSM_W:SSM_W + SSM_GROUPS * SSM_STATE],
                        act_ref[rows, SSM_W + SSM_GROUPS * SSM_STATE:], dt_ref[rows, :], *dec_refs, ht_scr, True)
        y = (yf_ref[rows, :] + jnp.concatenate(ys, axis=1)) * _silu_tanh(z_ref[rows, :].astype(_F32))
        y = y * lax.rsqrt(jnp.mean(y * y, axis=-1, keepdims=True) + EPS) * gssm_ref[...]
        o_ref[rows, :] = y.astype(_BF16)
    store_final()


def _ssd_calls(p3, dt3, prm, h0_f, h0_b, layer):
    b, l, _ = p3.shape
    q = min(SSD_CHUNKS_PER_STEP * SSM_CHUNK, l)
    nc = l // q
    has_h0 = h0_f is not None
    xbc_blk = COL_XBC // 2

    def vec(w):
        return pl.BlockSpec((1, w), lambda i, c: (0, 0))

    dec_specs = [vec(DT_PAD), vec(DT_PAD),
                 pl.BlockSpec((DT_PAD, 1), lambda i, c: (0, 0)), pl.BlockSpec((DT_PAD, 1), lambda i, c: (0, 0))]
    dec_args = [prm["dtb_row"], prm["alog_row"], prm["dtb_col"], prm["alog_col"]]
    h0_spec = pl.BlockSpec((None, None, SSM_HEADS, SSM_HEADDIM, SSM_STATE), lambda i, c: (i, layer, 0, 0, 0))
    state_spec = pl.BlockSpec((None, SSM_HEADS, SSM_HEADDIM, SSM_STATE), lambda i, c: (i, 0, 0, 0))
    state_shape = jax.ShapeDtypeStruct((b, SSM_HEADS, SSM_HEADDIM, SSM_STATE), _F32)
    ht_scratch = pltpu.VMEM((SSM_GROUPS, SSM_STATE, GROUP_W), _F32)

    in_specs = [
        pl.BlockSpec((None, q, CONV_CH), lambda i, c: (i, c, xbc_blk)),
        pl.BlockSpec((None, SSD_HALO, CONV_CH),
                     lambda i, c: (i, jnp.maximum(c * (q // SSD_HALO) - 1, 0), xbc_blk)),
        pl.BlockSpec((None, SSD_HALO, CONV_CH),
                     lambda i, c: (i, jnp.minimum((c + 1) * (q // SSD_HALO), l // SSD_HALO - 1), xbc_blk)),
        pl.BlockSpec((None, q, DT_PAD), lambda i, c: (i, c, 0)),
        pl.BlockSpec((3, CONV_CH), lambda i, c: (0, 0)), vec(CONV_CH)] + dec_specs + [vec(SSM_W)]
    args = [p3, p3, p3, dt3, prm["conv_wt"], prm["conv_b"]] + dec_args + [prm["dskip"]]
    if has_h0:
        in_specs.append(h0_spec)
        args.append(h0_f)
    y_f, act, h_f = pl.pallas_call(
        functools.partial(_ssd_fwd_body, has_h0=has_h0, nc=nc),
        grid=(b, nc),
        in_specs=in_specs,
        out_specs=[pl.BlockSpec((None, q, SSM_W), lambda i, c: (i, c, 0)),
                   pl.BlockSpec((None, q, CONV_CH), lambda i, c: (i, c, 0)), state_spec],
        out_shape=[jax.ShapeDtypeStruct((b, l, SSM_W), _F32),
                   jax.ShapeDtypeStruct((b, l, CONV_CH), _BF16), state_shape],
        scratch_shapes=[ht_scratch],
        compiler_params=_params("arbitrary", "arbitrary"),
        name="ssd_fwd",
    )(*args)

    def rc(c):
        return nc - 1 - c

    in_specs = [pl.BlockSpec((None, q, CONV_CH), lambda i, c: (i, rc(c), 0)),
                pl.BlockSpec((None, q, DT_PAD), lambda i, c: (i, rc(c), 0))] + dec_specs + [
        pl.BlockSpec((None, q, SSM_W), lambda i, c: (i, rc(c), 0)),
        pl.BlockSpec((None, q, SSM_W), lambda i, c: (i, rc(c), COL_Z)), vec(SSM_W)]
    args = [act, dt3] + dec_args + [y_f, p3, prm["g_ssm"]]
    if has_h0:
        in_specs.append(h0_spec)
        args.append(h0_b)
    o_b, h_b = pl.pallas_call(
        functools.partial(_ssd_bwd_body, has_h0=has_h0, nc=nc),
        grid=(b, nc),
        in_specs=in_specs,
        out_specs=[pl.BlockSpec((None, q, SSM_W), lambda i, c: (i, rc(c), 0)), state_spec],
        out_shape=[jax.ShapeDtypeStruct((b, l, SSM_W), _BF16), state_shape],
        scratch_shapes=[ht_scratch],
        compiler_params=_params("arbitrary", "arbitrary"),
        name="ssd_bwd",
    )(*args)
    return o_b, h_f, h_b


def _sgu_tile(u_ref, v_ref, g_ref, gs_ref, ws_ref, bs_ref, o_ref):
    v = v_ref[...].astype(_F32)
    mu = jnp.mean(v, axis=-1, keepdims=True)
    vc = v - mu
    var = jnp.mean(vc * vc, axis=-1, keepdims=True)
    vn = (vc * lax.rsqrt(var + EPS) * gs_ref[...]).astype(_BF16)
    ge = SGU_W // SGU_GROUPS
    for ch in range(v.shape[0] // SGU_CHUNK):
        rsl = slice(ch * SGU_CHUNK, (ch + 1) * SGU_CHUNK)
        for g in range(SGU_GROUPS):
            csl = slice(g * ge, (g + 1) * ge)
            vs = _dot(ws_ref[g], vn[rsl, csl]) + bs_ref[:, g:g + 1]
            y = u_ref[rsl, csl].astype(_F32) * vs * _silu(g_ref[rsl, csl].astype(_F32))
            o_ref[rsl, csl] = y.astype(_BF16)


def _merge_body(oa_ref, ob_ref, u_ref, vc_ref, gc_ref, ga_ref, gb_ref, gcm_ref, x_ref, mod_ref, gpost_ref,
                gs_ref, ws_ref, bs_ref, wa_ref, wb_ref, wc_ref, wo_ref, out_ref, oc_scr):
    _sgu_tile(u_ref, vc_ref, gc_ref, gs_ref, ws_ref, bs_ref, oc_scr)
    merged = _sigmoid(ga_ref[...].astype(_F32)) * _dot(oa_ref[...], wa_ref[...])
    merged = merged + _sigmoid(gb_ref[...].astype(_F32)) * _dot(ob_ref[...], wb_ref[...])
    merged = merged + _sigmoid(gcm_ref[...].astype(_F32)) * _dot(oc_scr[...], wc_ref[...])
    y = _dot(merged.astype(_BF16), wo_ref[...])
    y = y * lax.rsqrt(jnp.mean(y * y, axis=-1, keepdims=True) + EPS) * gpost_ref[...]
    out_ref[...] = x_ref[...] + mod_ref[:, 2 * D_MODEL:] * y


def _merge_call(o_a, o_b, p2, x2d, mod_l, prm, mod_row):
    t = x2d.shape[0]
    tm = MERGE_TM
    gm_blk = COL_GM // 2
    once = pl.Buffered(1)

    def row_spec(w, col=0):
        return pl.BlockSpec((tm, w), lambda i: (i, col))

    def const_spec(shape):
        return pl.BlockSpec(shape, lambda i: (0,) * len(shape))

    def w_spec(kdim):
        return pl.BlockSpec((kdim, D_MODEL), lambda i: (0, 0), pipeline_mode=once)

    return pl.pallas_call(
        _merge_body,
        grid=(t // tm,),
        in_specs=[row_spec(NA_W), row_spec(SSM_W),
                  row_spec(SGU_W, COL_U), row_spec(SGU_W, COL_VC), row_spec(SGU_W, COL_GC),
                  row_spec(D_MODEL, gm_blk), row_spec(D_MODEL, gm_blk + 1), row_spec(D_MODEL, gm_blk + 2),
                  row_spec(D_MODEL),
                  pl.BlockSpec((None, 1, 3 * D_MODEL), lambda i: (mod_row(i, tm), 0, 0)),
                  const_spec((1, D_MODEL)), const_spec((1, SGU_W)),
                  const_spec((SGU_GROUPS, SGU_CHUNK, SGU_CHUNK)), const_spec((SGU_CHUNK, SGU_GROUPS)),
                  w_spec(NA_W), w_spec(SSM_W), w_spec(SGU_W), w_spec(D_MODEL)],
        out_specs=row_spec(D_MODEL),
        out_shape=jax.ShapeDtypeStruct((t, D_MODEL), _F32),
        scratch_shapes=[pltpu.VMEM((tm, SGU_W), _BF16)],
        compiler_params=_params("arbitrary"),
        name="merge_out",
    )(o_a, o_b, p2, p2, p2, p2, p2, p2, x2d, mod_l, prm["g_post"], prm["g_sgu"], prm["w_s"], prm["b_s_t"],
      prm["w_br_a"], prm["w_br_b"], prm["w_br_c"], prm["w_out"])


def _layer(x3, mod_l, wts, mod_row, ctx, layer, caches=None):
    b, l, _ = x3.shape
    t = b * l
    x2d = x3.reshape(t, D_MODEL)
    p2, dt2, *kv = _inproj_call(x2d, mod_l, wts["g_pre"], wts["w_main"], layer, wts["w_dt"], mod_row,
                                want_kv=ctx is None)
    p3 = p2.reshape(b, l, N_MAIN)
    dt3 = dt2.reshape(b, l, DT_PAD)
    if ctx is None:
        o_a, caches = _ctx_attn_call(p3, kv[0].reshape(b, l, 2 * NA_W), layer, caches)
        h0_f = h0_b = None
    else:
        cache_k4, cache_v4, bias_tbl, h0_f, h0_b = ctx
        o_a = _na_call(p3, cache_k4, cache_v4, bias_tbl, layer)
    o_b, h_f, h_b = _ssd_calls(p3, dt3, wts, h0_f, h0_b, layer)
    y2d = _merge_call(o_a.reshape(t, NA_W), o_b.reshape(t, SSM_W), p2, x2d, mod_l, wts, mod_row)
    return y2d.reshape(b, l, D_MODEL), (caches, h_f, h_b)


def _layer_weights(l, g_pre, g_post, w_main, w_dt, conv_w, conv_b, dt_bias, a_log, d_skip, g_ssm, w_s, b_s, g_sgu,
                   w_br_a, w_br_b, w_br_c, w_out):
    dtb = jnp.pad(dt_bias[l].reshape(N_DT).astype(_F32), (0, DT_PAD - N_DT))
    alog = jnp.pad(a_log[l].reshape(N_DT).astype(_F32), (0, DT_PAD - N_DT))
    return {
        "g_pre": g_pre[l].reshape(1, D_MODEL), "g_post": g_post[l].reshape(1, D_MODEL),
        "w_main": w_main, "w_dt": w_dt,
        "conv_wt": conv_w[l].T, "conv_b": conv_b[l].reshape(1, CONV_CH),
        "dtb_row": dtb.reshape(1, DT_PAD), "alog_row": alog.reshape(1, DT_PAD),
        "dtb_col": dtb.reshape(DT_PAD, 1), "alog_col": alog.reshape(DT_PAD, 1),
        "dskip": jnp.repeat(d_skip[l].astype(_F32), SSM_HEADDIM).reshape(1, SSM_W),
        "g_ssm": g_ssm[l].reshape(1, SSM_W),
        "g_sgu": g_sgu[l].reshape(1, SGU_W), "w_s": w_s[l].astype(_BF16), "b_s_t": b_s[l].T,
        "w_br_a": w_br_a[l].astype(_BF16), "w_br_b": w_br_b[l].astype(_BF16),
        "w_br_c": w_br_c[l].astype(_BF16), "w_out": w_out[l].astype(_BF16),
    }


def kernel(x_prompt, x_sample, c, cache_k, cache_v, state_ssm_fwd, state_ssm_bwd, c_ctx, w_mod, b_mod, g_pre,
           g_post, w_in, rpb, conv_w, conv_b, dt_bias, a_log, d_skip, g_ssm, w_s, b_s, g_sgu, w_br_a, w_br_b,
           w_br_c, w_out):
    nb, ls, _ = x_sample.shape
    bp, lp, _ = x_prompt.shape
    assert 1 + nb <= MOD_ROWS
    assert ls % INPROJ_TM == 0 and (bp * lp) % INPROJ_TM == 0 and ls % MERGE_TM == 0
    assert lp % SSM_CHUNK == 0 and ls % (SSD_CHUNKS_PER_STEP * SSM_CHUNK) == 0 and MERGE_TM % SGU_CHUNK == 0
    assert ls % GRID_W == 0 and ls // GRID_W >= NA_KH
    cvecs = jnp.concatenate([c_ctx[None, :], c, jnp.zeros((MOD_ROWS - 1 - nb, D_MODEL), _F32)], axis=0)
    mod = _mod_call(cvecs, w_mod, b_mod).reshape(DEPTH, MOD_ROWS, 1, 3 * D_MODEL)
    past = cache_k.shape[2]
    cache_k4 = cache_k.reshape(nb, DEPTH, past, NA_W)
    cache_v4 = cache_v.reshape(nb, DEPTH, past, NA_W)

    def prompt_row(i, tm):
        return 0

    def sample_row(i, tm):
        return 1 + (i * tm) // ls

    y_p, y_s = x_prompt, x_sample
    caches = (jnp.zeros((bp, DEPTH, lp, NA_W), _F32), jnp.zeros((bp, DEPTH, lp, NA_W), _F32))
    hf_l, hb_l = [], []
    w_main, w_dt = _repack_call(jnp.swapaxes(w_in, 1, 2))
    for l in range(DEPTH):
        wts = _layer_weights(l, g_pre, g_post, w_main, w_dt, conv_w, conv_b, dt_bias, a_log, d_skip, g_ssm, w_s,
                             b_s, g_sgu, w_br_a, w_br_b, w_br_c, w_out)
        y_p, (caches, h_f, h_b) = _layer(y_p, mod[l], wts, prompt_row, None, l, caches)
        hf_l.append(h_f)
        hb_l.append(h_b)
        ctx = (cache_k4, cache_v4, _na_bias_table(rpb[l]), state_ssm_fwd, state_ssm_bwd)
        y_s, _ = _layer(y_s, mod[l], wts, sample_row, ctx, l)
    new_k = caches[0].reshape(bp, DEPTH, lp, NA_HEADS, NA_HEAD_DIM)
    new_v = caches[1].reshape(bp, DEPTH, lp, NA_HEADS, NA_HEAD_DIM)
    return (y_p, y_s, new_k, new_v, jnp.stack(hf_l, axis=1), jnp.stack(hb_l, axis=1))
```

```python
import functools

import jax
import jax.numpy as jnp
from jax import lax
from jax.experimental import pallas as pl
from jax.experimental.pallas import tpu as pltpu

D_MODEL = 2048
DEPTH = 2
EPS = 1e-6
GRID_W = 64
NA_HEAD_DIM = 64
NA_W = D_MODEL // 2
NA_HEADS = NA_W // NA_HEAD_DIM
NA_KH = 8
NA_KW = 16
SSM_HEADDIM = 64
SSM_W = D_MODEL // 2
SSM_HEADS = SSM_W // SSM_HEADDIM
SSM_GROUPS = 4
SSM_STATE = 128
SSM_CHUNK = 128
CONV_CH = SSM_W + 2 * SSM_GROUPS * SSM_STATE
SGU_W = D_MODEL // 2
SGU_GROUPS = 8
SGU_CHUNK = 128

SEG_W = 1024
N_MAIN = 16 * SEG_W
DT_PAD = 128
COL_Q, COL_K, COL_V, COL_GA, COL_XBC, COL_Z, COL_U, COL_VC, COL_GC, COL_GM = 0, 1, 2, 3, 4, 6, 7, 8, 9, 10

NEG = -1e30
LOG2E = 1.4426950408889634
MOD_ROWS = 8

V7X_VMEM_BYTES = 64 * 1024 * 1024
VMEM_LIMIT = V7X_VMEM_BYTES * 7 // 8
INPROJ_TM = 1024
MERGE_TM = 256
MOD_TN = 1024

_F32 = jnp.float32
_BF16 = jnp.bfloat16
_NT = (((1,), (1,)), ((), ()))


def _sigmoid(x):
    return 1.0 / (1.0 + jnp.exp(-x))


def _silu(x):
    return x * _sigmoid(x)


def _softplus(x):
    return jnp.maximum(x, 0.0) + jnp.log(1.0 + jnp.exp(-jnp.abs(x)))


def _dot(a, b):
    return jnp.dot(a, b, preferred_element_type=_F32)


def _params(*sem):
    return pltpu.CompilerParams(dimension_semantics=sem, vmem_limit_bytes=VMEM_LIMIT)


def _mod_body(c_ref, w_ref, b_ref, o_ref):
    s = _silu(c_ref[...]).astype(_BF16)
    o_ref[...] = _dot(s, w_ref[...].astype(_BF16)) + b_ref[...]


def _mod_call(cvecs, w_mod, b_mod):
    tn = MOD_TN
    n3 = 3 * D_MODEL
    return pl.pallas_call(
        _mod_body,
        grid=(DEPTH, n3 // tn),
        in_specs=[pl.BlockSpec((MOD_ROWS, D_MODEL), lambda l, j: (0, 0)),
                  pl.BlockSpec((None, D_MODEL, tn), lambda l, j: (l, 0, j)),
                  pl.BlockSpec((None, 1, tn), lambda l, j: (l, 0, j))],
        out_specs=pl.BlockSpec((None, MOD_ROWS, tn), lambda l, j: (l, 0, j)),
        out_shape=jax.ShapeDtypeStruct((DEPTH, MOD_ROWS, n3), _F32),
        compiler_params=_params("arbitrary", "arbitrary"),
        name="modulation",
    )(cvecs, w_mod, b_mod.reshape(DEPTH, 1, n3))


N_DT = 2 * SSM_HEADS
OFF_DT = 4 * NA_W + CONV_CH + SSM_W
REPACK_BLK = 1024


def _repack_body(a_ref, b_ref, o_ref, odt_ref):
    r = pl.program_id(1)
    cut = OFF_DT // REPACK_BLK

    @pl.when(r < cut)
    def _():
        o_ref[...] = a_ref[...].T.astype(_BF16)

    @pl.when(r >= cut)
    def _():
        o_ref[...] = jnp.concatenate([a_ref[N_DT:, :], b_ref[...]], axis=0).T.astype(_BF16)

    @pl.when(r == cut)
    def _():
        dt_rows = jnp.concatenate([a_ref[:N_DT, :], jnp.zeros((DT_PAD - N_DT, D_MODEL), _F32)], axis=0)
        odt_ref[...] = dt_rows.T.astype(_BF16)


def _repack_call(w_in_t):
    return pl.pallas_call(
        _repack_body,
        grid=(DEPTH, N_MAIN // REPACK_BLK),
        in_specs=[pl.BlockSpec((None, REPACK_BLK, D_MODEL), lambda l, r: (l, r, 0)),
                  pl.BlockSpec((None, N_DT, D_MODEL), lambda l, r: (l, (r + 1) * (REPACK_BLK // N_DT), 0))],
        out_specs=[pl.BlockSpec((None, D_MODEL, REPACK_BLK), lambda l, r: (l, 0, r)),
                   pl.BlockSpec((None, D_MODEL, DT_PAD), lambda l, r: (l, 0, 0))],
        out_shape=[jax.ShapeDtypeStruct((DEPTH, D_MODEL, N_MAIN), _BF16),
                   jax.ShapeDtypeStruct((DEPTH, D_MODEL, DT_PAD), _BF16)],
        compiler_params=_params("arbitrary", "arbitrary"),
        name="repack_w_in",
    )(w_in_t, w_in_t)


def _inproj_body(*refs, want_kv):
    x_ref, mod_ref, g_ref, w_ref, wdt_ref, o_ref, dt_ref = refs[:7]
    h_scr = refs[-1]
    j = pl.program_id(1)

    @pl.when(j == 0)
    def _():
        x = x_ref[...]
        xn = x * lax.rsqrt(jnp.mean(x * x, axis=-1, keepdims=True) + EPS) * g_ref[...]
        shift = mod_ref[:, 0:D_MODEL]
        scale = mod_ref[:, D_MODEL:2 * D_MODEL]
        h = (xn * (1.0 + scale) + shift).astype(_BF16)
        h_scr[...] = h
        dt_ref[...] = _dot(h, wdt_ref[...])

    res = _dot(h_scr[...], w_ref[...])
    o_ref[...] = res.astype(_BF16)
    if want_kv:
        kv_ref = refs[7]

        @pl.when((j == COL_K) | (j == COL_V))
        def _():
            kv_ref[...] = res


def _inproj_call(x2d, mod_l, g_pre_l, w_main, layer, w_dt, mod_row, want_kv):
    t = x2d.shape[0]
    tm = INPROJ_TM
    tn = SEG_W if want_kv else 2 * SEG_W
    out_specs = [pl.BlockSpec((tm, tn), lambda i, j: (i, j)),
                 pl.BlockSpec((tm, DT_PAD), lambda i, j: (i, 0))]
    out_shape = [jax.ShapeDtypeStruct((t, N_MAIN), _BF16),
                 jax.ShapeDtypeStruct((t, DT_PAD), _F32)]
    if want_kv:
        out_specs.append(pl.BlockSpec((tm, NA_W), lambda i, j: (i, jnp.clip(j - COL_K, 0, COL_V - COL_K))))
        out_shape.append(jax.ShapeDtypeStruct((t, 2 * NA_W), _F32))
    return pl.pallas_call(
        functools.partial(_inproj_body, want_kv=want_kv),
        grid=(t // tm, N_MAIN // tn),
        in_specs=[pl.BlockSpec((tm, D_MODEL), lambda i, j: (i, 0)),
                  pl.BlockSpec((None, 1, 3 * D_MODEL), lambda i, j: (mod_row(i, tm), 0, 0)),
                  pl.BlockSpec((1, D_MODEL), lambda i, j: (0, 0)),
                  pl.BlockSpec((None, D_MODEL, tn), lambda i, j: (layer, 0, j)),
                  pl.BlockSpec((None, D_MODEL, DT_PAD), lambda i, j: (layer, 0, 0))],
        out_specs=out_specs,
        out_shape=out_shape,
        scratch_shapes=[pltpu.VMEM((tm, D_MODEL), _BF16)],
        compiler_params=_params("arbitrary", "arbitrary"),
        name="inproj",
    )(x2d, mod_l, g_pre_l, w_main, w_dt)


def _stack_heads(q, head0):
    q = q * (NA_HEAD_DIM ** -0.5)
    zero = jnp.zeros_like(q)
    return jnp.concatenate([jnp.where(head0, q, zero), jnp.where(head0, zero, q)], axis=0)


def _ctx_attn_body(q_ref, k_ref, v_ref, ga_ref, kf_ref, vf_ref, kbuf_ref, vbuf_ref, o_ref, ko_ref, vo_ref):
    del kbuf_ref, vbuf_ref
    l = q_ref.shape[0]
    hp = 2 * NA_HEAD_DIM
    ko_ref[...] = kf_ref[...]
    vo_ref[...] = vf_ref[...]
    head0 = lax.broadcasted_iota(jnp.int32, (l, hp), 1) < NA_HEAD_DIM
    ones = jnp.ones((l, hp), _BF16)
    for j in range(NA_HEADS // 2):
        cols = slice(j * hp, (j + 1) * hp)
        q2 = _stack_heads(q_ref[:, cols], head0)
        s = lax.dot_general(q2, k_ref[:, cols], _NT, preferred_element_type=_F32)
        p = jnp.exp(s - jnp.max(s, axis=-1, keepdims=True)).astype(_BF16)
        v_ext = jnp.concatenate([v_ref[:, cols], ones], axis=1)
        oe = _dot(p, v_ext)
        on = oe[:, :hp] / oe[:, hp:]
        o = jnp.where(head0, on[:l], on[l:])
        o_ref[:, cols] = (o * _silu(ga_ref[:, cols].astype(_F32))).astype(_BF16)


def _ctx_attn_call(p3, kv3, layer, caches):
    b, l, _ = p3.shape

    def spec(col):
        return pl.BlockSpec((None, l, NA_W), lambda i: (i, 0, col))

    cache_spec = pl.BlockSpec((None, None, l, NA_W), lambda i: (i, layer, 0, 0))
    cache_shape = jax.ShapeDtypeStruct((b, DEPTH, l, NA_W), _F32)
    any_spec = pl.BlockSpec(memory_space=pl.ANY)
    o_a, k_all, v_all = pl.pallas_call(
        _ctx_attn_body,
        grid=(b,),
        in_specs=[spec(COL_Q), spec(COL_K), spec(COL_V), spec(COL_GA), spec(0), spec(1), any_spec, any_spec],
        out_specs=[pl.BlockSpec((None, l, NA_W), lambda i: (i, 0, 0)), cache_spec, cache_spec],
        out_shape=[jax.ShapeDtypeStruct((b, l, NA_W), _BF16), cache_shape, cache_shape],
        input_output_aliases={6: 1, 7: 2},
        compiler_params=_params("arbitrary"),
        name="ctx_attn",
    )(p3, p3, p3, p3, kv3, kv3, *caches)
    return o_a, (k_all, v_all)


def _na_bias_table(rpb_l):
    nk = 2 * NA_KW - 1
    j = jnp.arange(GRID_W)[None, :, None]
    c = jnp.arange(GRID_W)[None, None, :]
    k = jnp.arange(nk)[:, None, None]
    cs = jnp.clip(j - NA_KW // 2, 0, GRID_W - NA_KW)
    valid = (c >= cs) & (c < cs + NA_KW)
    sel = (valid & (c - j + (NA_KW - 1) == k)).astype(_F32)
    zeros = jnp.zeros_like(sel)
    sel2 = jnp.concatenate([jnp.concatenate([sel, zeros], axis=-1), jnp.concatenate([zeros, sel], axis=-1)], axis=0)
    r = rpb_l.astype(_F32)
    r2 = jnp.concatenate([r[:, :-1], r[:, 1:]], axis=-1)
    tbl = jnp.einsum("hdm,mjn->hdjn", r2, sel2, precision=lax.Precision.HIGHEST)
    mask = jnp.where(valid, 0.0, NEG)[0]
    return tbl + jnp.concatenate([mask, mask], axis=-1)[None, None]


def _na_body(q_ref, k_ref, v_ref, ga_ref, kc_ref, vc_ref, bias_ref, o_ref, vb_scr, kcb_scr, vcb_scr, *, rows):
    hp = 2 * NA_HEAD_DIM
    win = NA_KH * GRID_W
    vb_scr[:, :hp] = v_ref[...]
    vb_scr[:, hp:] = jnp.ones((vb_scr.shape[0], hp), _BF16)
    kcb_scr[...] = kc_ref[...].astype(_BF16)
    vcb_scr[:, :hp] = vc_ref[...].astype(_BF16)
    vcb_scr[:, hp:] = jnp.ones((vcb_scr.shape[0], hp), _BF16)
    head0 = lax.broadcasted_iota(jnp.int32, (GRID_W, hp), 1) < NA_HEAD_DIM

    def row(r, carry):
        rs = jnp.clip(r - NA_KH // 2, 0, rows - NA_KH)
        dr0 = rs - r + (NA_KH - 1)
        q0 = pl.multiple_of(r * GRID_W, GRID_W)
        k0 = pl.multiple_of(rs * GRID_W, GRID_W)
        q2 = _stack_heads(q_ref[pl.ds(q0, GRID_W), :], head0)
        bias = jnp.concatenate(
            [jnp.concatenate([bias_ref[hh, dr0 + a] for a in range(0, NA_KH, 2)], axis=1) for hh in range(2)],
            axis=0)
        s_loc = lax.dot_general(q2, k_ref[pl.ds(k0, win), :], _NT, preferred_element_type=_F32) + bias
        s_ctx = lax.dot_general(q2, kcb_scr[...], _NT, preferred_element_type=_F32)
        m = jnp.maximum(jnp.max(s_loc, axis=-1, keepdims=True), jnp.max(s_ctx, axis=-1, keepdims=True))
        p_loc = jnp.exp(s_loc - m).astype(_BF16)
        p_ctx = jnp.exp(s_ctx - m).astype(_BF16)
        oe = _dot(p_loc, vb_scr[pl.ds(k0, win), :]) + _dot(p_ctx, vcb_scr[...])
        on = oe[:, :hp] / oe[:, hp:]
        o = jnp.where(head0, on[:GRID_W], on[GRID_W:])
        o_ref[pl.ds(q0, GRID_W), :] = (o * _silu(ga_ref[pl.ds(q0, GRID_W), :].astype(_F32))).astype(_BF16)
        return carry

    lax.fori_loop(0, rows, row, 0, unroll=True)


def _na_call(p3, cache_k4, cache_v4, bias_tbl, layer):
    b, l, _ = p3.shape
    rows = l // GRID_W
    assert rows >= NA_KH and rows % 2 == 0
    hp = 2 * NA_HEAD_DIM
    nblk = SEG_W // hp
    lc = cache_k4.shape[2]

    def spec(col):
        return pl.BlockSpec((None, l, hp), lambda i, j: (i, 0, col * nblk + j))

    cspec = pl.BlockSpec((None, None, lc, hp), lambda i, j: (i, layer, 0, j))
    return pl.pallas_call(
        functools.partial(_na_body, rows=rows),
        grid=(b, NA_HEADS // 2),
        in_specs=[spec(COL_Q), spec(COL_K), spec(COL_V), spec(COL_GA), cspec, cspec,
                  pl.BlockSpec((2, 2 * NA_KH - 2, GRID_W, 2 * GRID_W), lambda i, j: (j, 0, 0, 0))],
        out_specs=pl.BlockSpec((None, l, hp), lambda i, j: (i, 0, j)),
        out_shape=jax.ShapeDtypeStruct((b, l, NA_W), _BF16),
        scratch_shapes=[pltpu.VMEM((l, 2 * hp), _BF16),
                        pltpu.VMEM((lc, hp), _BF16), pltpu.VMEM((lc, 2 * hp), _BF16)],
        compiler_params=_params("arbitrary", "arbitrary"),
        name="na_attn",
    )(p3, p3, p3, p3, cache_k4, cache_v4, bias_tbl)


HEADS_PER_GROUP = SSM_HEADS // SSM_GROUPS
GROUP_W = HEADS_PER_GROUP * SSM_HEADDIM
SSD_CHUNKS_PER_STEP = 4
SSD_HALO = 16


def _split3(x):
    hi = x.astype(_BF16)
    r1 = x - hi.astype(_F32)
    mid = r1.astype(_BF16)
    lo = (r1 - mid.astype(_F32)).astype(_BF16)
    return hi, mid, lo


def _silu_tanh(x):
    h = 0.5 * x
    return h + h * jnp.tanh(h)


def _ssd_chunk(xs_b, bm, cm, dtraw, dtb_row_ref, alog_row_ref, dtb_col_ref, alog_col_ref, ht_scr, reverse):
    q = SSM_CHUNK
    d = 1 if reverse else 0
    last = 0 if reverse else q - 1

    hs = slice(d * SSM_HEADS, (d + 1) * SSM_HEADS)
    dta_col = _softplus(dtraw + dtb_row_ref[...]) * (-jnp.exp(alog_row_ref[...]))
    dt_row = _softplus(dtraw.T[hs, :] + dtb_col_ref[hs, :])
    dta_row = dt_row * (-jnp.exp(alog_col_ref[hs, :]))

    ri = lax.broadcasted_iota(jnp.int32, (q, q), 0)
    ci = lax.broadcasted_iota(jnp.int32, (q, q), 1)
    causal = (ci >= ri) if reverse else (ci <= ri)
    t_col = jnp.where(causal, 1.0, 0.0).astype(_BF16)
    t_row = jnp.where((ri >= ci) if reverse else (ri <= ci), 1.0, 0.0).astype(_BF16)
    acum_col = sum(_dot(t_col, part) for part in _split3(dta_col)) * LOG2E
    acum_row = sum(_dot(part, t_row) for part in _split3(dta_row)) * LOG2E

    a_last = jnp.broadcast_to(acum_row[:, last:last + 1], (SSM_HEADS, q))
    dtde_row = dt_row * jnp.exp2(a_last - acum_row)
    cdecay = jnp.exp2(a_last)

    lo_half = lax.broadcasted_iota(jnp.int32, (1, q), 1) < SSM_HEADDIM
    lane_head = lax.broadcasted_iota(jnp.int32, (1, GROUP_W), 1) // SSM_HEADDIM
    zero_b = jnp.zeros((q, GROUP_W), _BF16)

    def per_head_lanes(vals):
        return jnp.concatenate([jnp.where(lo_half, vals[0], vals[1]), jnp.where(lo_half, vals[2], vals[3])], axis=1)

    ys = []
    for g in range(SSM_GROUPS):
        nsl = slice(g * SSM_STATE, (g + 1) * SSM_STATE)
        b_g = bm[:, nsl]
        c_g = cm[:, nsl]
        cb = lax.dot_general(c_g, b_g, _NT, preferred_element_type=_F32)
        b_t = b_g.astype(_F32).T
        xs_g = xs_b[:, g * GROUP_W:(g + 1) * GROUP_W]
        m_parts, bt_parts, xbd_parts, bcs = [], [], [], []
        for hh in range(HEADS_PER_GROUP):
            h = g * HEADS_PER_GROUP + hh
            lane = d * SSM_HEADS + h
            bc = jnp.broadcast_to(acum_col[:, lane:lane + 1], (q, q))
            lmat = jnp.exp2(jnp.where(causal, bc - acum_row[h:h + 1, :], NEG))
            m_parts.append((cb * lmat * dt_row[h:h + 1, :]).astype(_BF16))
            bt_parts.append((b_t * dtde_row[h:h + 1, :]).astype(_BF16))
            xbd_parts.append(jnp.where(lane_head == hh, xs_g, zero_b))
            bcs.append(bc)
        lhs = jnp.concatenate([jnp.concatenate(m_parts, axis=1), jnp.concatenate(bt_parts, axis=1)], axis=0)
        res = _dot(lhs, jnp.concatenate(xbd_parts, axis=0))
        h_t = ht_scr[g]
        y_off = _dot(c_g, h_t.astype(_BF16)) * jnp.exp2(per_head_lanes(bcs))
        ys.append(res[:q] + y_off)
        h0 = g * HEADS_PER_GROUP
        cd = per_head_lanes([cdecay[h0 + hh:h0 + hh + 1, :] for hh in range(HEADS_PER_GROUP)])
        ht_scr[g] = h_t * cd + res[q:]
    return ys


def _ssd_state_io(c, nc, h0_ref, stack_ref, hout_ref, ht_scr):
    @pl.when(c == 0)
    def _():
        for g in range(SSM_GROUPS):
            if h0_ref is None:
                ht_scr[g] = jnp.zeros((SSM_STATE, GROUP_W), _F32)
            else:
                hs = h0_ref[g * HEADS_PER_GROUP:(g + 1) * HEADS_PER_GROUP]
                ht_scr[g] = hs.reshape(GROUP_W, SSM_STATE).T

    def store_final():
        @pl.when(c == nc - 1)
        def _():
            slot = hout_ref.shape[0] - 1
            if stack_ref is not None:
                hout_ref[0:slot] = stack_ref[...]
            for g in range(SSM_GROUPS):
                hout_ref[slot, g * HEADS_PER_GROUP:(g + 1) * HEADS_PER_GROUP] = ht_scr[g].T.reshape(
                    HEADS_PER_GROUP, SSM_HEADDIM, SSM_STATE)

    return store_final


def _ssd_fwd_body(*refs, has_h0, has_stack, nc):
    it = iter(refs)
    x_ref, prev_ref, next_ref, dt_ref, cw_ref, cb_ref = (next(it) for _ in range(6))
    dec_refs = [next(it) for _ in range(4)]
    dskip_ref = next(it)
    h0_ref = next(it) if has_h0 else None
    stack_ref = next(it) if has_stack else None
    y_ref, act_ref, hout_ref, ht_scr = (next(it) for _ in range(4))
    q = SSM_CHUNK
    qb = x_ref.shape[0]
    c = pl.program_id(1)
    store_final = _ssd_state_io(c, nc, h0_ref, stack_ref, hout_ref, ht_scr)

    x = x_ref[...].astype(_F32)
    xp = jnp.where(c > 0, prev_ref[SSD_HALO - 1:SSD_HALO, :].astype(_F32), 0.0)
    xn = jnp.where(c < nc - 1, next_ref[0:1, :].astype(_F32), 0.0)
    sub = lax.broadcasted_iota(jnp.int32, (8, 1), 0)
    x_m1 = pltpu.roll(x, 1, axis=0)
    x_m1 = jnp.concatenate([jnp.where(sub == 0, xp, x_m1[:8]), x_m1[8:]], axis=0)
    x_p1 = pltpu.roll(x, qb - 1, axis=0)
    x_p1 = jnp.concatenate([x_p1[:qb - 8], jnp.where(sub == 7, xn, x_p1[qb - 8:])], axis=0)
    act = _silu_tanh(cw_ref[0:1, :] * x_m1 + cw_ref[1:2, :] * x + cw_ref[2:3, :] * x_p1 + cb_ref[...])
    act_b = act.astype(_BF16)
    act_ref[...] = act_b

    for s in range(qb // q):
        rows = slice(s * q, (s + 1) * q)
        ys = _ssd_chunk(act_b[rows, :SSM_W], act_b[rows, SSM_W:SSM_W + SSM_GROUPS * SSM_STATE],
                        act_b[rows, SSM_W + SSM_GROUPS * SSM_STATE:], dt_ref[rows, :], *dec_refs, ht_scr, False)
        y_ref[rows, :] = jnp.concatenate(ys, axis=1) + act[rows, :SSM_W] * dskip_ref[...]
    store_final()


def _ssd_bwd_body(*refs, has_h0, has_stack, nc):
    it = iter(refs)
    act_ref, dt_ref = next(it), next(it)
    dec_refs = [next(it) for _ in range(4)]
    yf_ref, z_ref, gssm_ref = next(it), next(it), next(it)
    h0_ref = next(it) if has_h0 else None
    stack_ref = next(it) if has_stack else None
    o_ref, hout_ref, ht_scr = next(it), next(it), next(it)
    c = pl.program_id(1)
    store_final = _ssd_state_io(c, nc, h0_ref, stack_ref, hout_ref, ht_scr)

    q = SSM_CHUNK
    for s in reversed(range(act_ref.shape[0] // q)):
        rows = slice(s * q, (s + 1) * q)
        ys = _ssd_chunk(act_ref[rows, :SSM_W], act_ref[rows, SSM_W:SSM_W + SSM_GROUPS * SSM_STATE],
                        act_ref[rows, SSM_W + SSM_GROUPS * SSM_STATE:], dt_ref[rows, :], *dec_refs, ht_scr, True)
        y = (yf_ref[rows, :] + jnp.concatenate(ys, axis=1)) * _silu_tanh(z_ref[rows, :].astype(_F32))
        y = y * lax.rsqrt(jnp.mean(y * y, axis=-1, keepdims=True) + EPS) * gssm_ref[...]
        o_ref[rows, :] = y.astype(_BF16)
    store_final()


def _ssd_calls(p3, dt3, prm, h0_f, h0_b, layer, stacks):
    b, l, _ = p3.shape
    q = min(SSD_CHUNKS_PER_STEP * SSM_CHUNK, l)
    nc = l // q
    has_h0 = h0_f is not None
    has_stack = stacks is not None
    slots = stacks[0].shape[0] + 1 if has_stack else 1
    xbc_blk = COL_XBC // 2

    def vec(w):
        return pl.BlockSpec((1, w), lambda i, c: (0, 0))

    dec_specs = [vec(DT_PAD), vec(DT_PAD),
                 pl.BlockSpec((DT_PAD, 1), lambda i, c: (0, 0)), pl.BlockSpec((DT_PAD, 1), lambda i, c: (0, 0))]
    dec_args = [prm["dtb_row"], prm["alog_row"], prm["dtb_col"], prm["alog_col"]]
    h0_spec = pl.BlockSpec((None, None, SSM_HEADS, SSM_HEADDIM, SSM_STATE), lambda i, c: (i, layer, 0, 0, 0))
    stack_spec = pl.BlockSpec((slots - 1, None, SSM_HEADS, SSM_HEADDIM, SSM_STATE), lambda i, c: (0, i, 0, 0, 0))
    state_spec = pl.BlockSpec((slots, None, SSM_HEADS, SSM_HEADDIM, SSM_STATE), lambda i, c: (0, i, 0, 0, 0))
    state_shape = jax.ShapeDtypeStruct((slots, b, SSM_HEADS, SSM_HEADDIM, SSM_STATE), _F32)
    ht_scratch = pltpu.VMEM((SSM_GROUPS, SSM_STATE, GROUP_W), _F32)

    in_specs = [
        pl.BlockSpec((None, q, CONV_CH), lambda i, c: (i, c, xbc_blk)),
        pl.BlockSpec((None, SSD_HALO, CONV_CH),
                     lambda i, c: (i, jnp.maximum(c * (q // SSD_HALO) - 1, 0), xbc_blk)),
        pl.BlockSpec((None, SSD_HALO, CONV_CH),
                     lambda i, c: (i, jnp.minimum((c + 1) * (q // SSD_HALO), l // SSD_HALO - 1), xbc_blk)),
        pl.BlockSpec((None, q, DT_PAD), lambda i, c: (i, c, 0)),
        pl.BlockSpec((3, CONV_CH), lambda i, c: (0, 0)), vec(CONV_CH)] + dec_specs + [vec(SSM_W)]
    args = [p3, p3, p3, dt3, prm["conv_wt"], prm["conv_b"]] + dec_args + [prm["dskip"]]
    if has_h0:
        in_specs.append(h0_spec)
        args.append(h0_f)
    if has_stack:
        in_specs.append(stack_spec)
        args.append(stacks[0])
    y_f, act, h_f = pl.pallas_call(
        functools.partial(_ssd_fwd_body, has_h0=has_h0, has_stack=has_stack, nc=nc),
        grid=(b, nc),
        in_specs=in_specs,
        out_specs=[pl.BlockSpec((None, q, SSM_W), lambda i, c: (i, c, 0)),
                   pl.BlockSpec((None, q, CONV_CH), lambda i, c: (i, c, 0)), state_spec],
        out_shape=[jax.ShapeDtypeStruct((b, l, SSM_W), _F32),
                   jax.ShapeDtypeStruct((b, l, CONV_CH), _BF16), state_shape],
        scratch_shapes=[ht_scratch],
        compiler_params=_params("arbitrary", "arbitrary"),
        name="ssd_fwd",
    )(*args)

    def rc(c):
        return nc - 1 - c

    in_specs = [pl.BlockSpec((None, q, CONV_CH), lambda i, c: (i, rc(c), 0)),
                pl.BlockSpec((None, q, DT_PAD), lambda i, c: (i, rc(c), 0))] + dec_specs + [
        pl.BlockSpec((None, q, SSM_W), lambda i, c: (i, rc(c), 0)),
        pl.BlockSpec((None, q, SSM_W), lambda i, c: (i, rc(c), COL_Z)), vec(SSM_W)]
    args = [act, dt3] + dec_args + [y_f, p3, prm["g_ssm"]]
    if has_h0:
        in_specs.append(h0_spec)
        args.append(h0_b)
    if has_stack:
        in_specs.append(stack_spec)
        args.append(stacks[1])
    o_b, h_b = pl.pallas_call(
        functools.partial(_ssd_bwd_body, has_h0=has_h0, has_stack=has_stack, nc=nc),
        grid=(b, nc),
        in_specs=in_specs,
        out_specs=[pl.BlockSpec((None, q, SSM_W), lambda i, c: (i, rc(c), 0)), state_spec],
        out_shape=[jax.ShapeDtypeStruct((b, l, SSM_W), _BF16), state_shape],
        scratch_shapes=[ht_scratch],
        compiler_params=_params("arbitrary", "arbitrary"),
        name="ssd_bwd",
    )(*args)
    return o_b, (h_f, h_b)


def _sgu_tile(u_ref, v_ref, g_ref, gs_ref, ws_ref, bs_ref, o_ref):
    v = v_ref[...].astype(_F32)
    mu = jnp.mean(v, axis=-1, keepdims=True)
    vc = v - mu
    var = jnp.mean(vc * vc, axis=-1, keepdims=True)
    vn = (vc * lax.rsqrt(var + EPS) * gs_ref[...]).astype(_BF16)
    ge = SGU_W // SGU_GROUPS
    for ch in range(v.shape[0] // SGU_CHUNK):
        rsl = slice(ch * SGU_CHUNK, (ch + 1) * SGU_CHUNK)
        for g in range(SGU_GROUPS):
            csl = slice(g * ge, (g + 1) * ge)
            vs = _dot(ws_ref[g], vn[rsl, csl]) + bs_ref[:, g:g + 1]
            y = u_ref[rsl, csl].astype(_F32) * vs * _silu(g_ref[rsl, csl].astype(_F32))
            o_ref[rsl, csl] = y.astype(_BF16)


def _merge_body(oa_ref, ob_ref, u_ref, vc_ref, gc_ref, ga_ref, gb_ref, gcm_ref, x_ref, mod_ref, gpost_ref,
                gs_ref, ws_ref, bs_ref, wa_ref, wb_ref, wc_ref, wo_ref, out_ref, oc_scr):
    _sgu_tile(u_ref, vc_ref, gc_ref, gs_ref, ws_ref, bs_ref, oc_scr)
    merged = _sigmoid(ga_ref[...].astype(_F32)) * _dot(oa_ref[...], wa_ref[...])
    merged = merged + _sigmoid(gb_ref[...].astype(_F32)) * _dot(ob_ref[...], wb_ref[...])
    merged = merged + _sigmoid(gcm_ref[...].astype(_F32)) * _dot(oc_scr[...], wc_ref[...])
    y = _dot(merged.astype(_BF16), wo_ref[...])
    y = y * lax.rsqrt(jnp.mean(y * y, axis=-1, keepdims=True) + EPS) * gpost_ref[...]
    out_ref[...] = x_ref[...] + mod_ref[:, 2 * D_MODEL:] * y


def _merge_call(o_a, o_b, p2, x2d, mod_l, prm, mod_row):
    t = x2d.shape[0]
    tm = MERGE_TM
    gm_blk = COL_GM // 2
    once = pl.Buffered(1)

    def row_spec(w, col=0):
        return pl.BlockSpec((tm, w), lambda i: (i, col))

    def const_spec(shape):
        return pl.BlockSpec(shape, lambda i: (0,) * len(shape))

    def w_spec(kdim):
        return pl.BlockSpec((kdim, D_MODEL), lambda i: (0, 0), pipeline_mode=once)

    return pl.pallas_call(
        _merge_body,
        grid=(t // tm,),
        in_specs=[row_spec(NA_W), row_spec(SSM_W),
                  row_spec(SGU_W, COL_U), row_spec(SGU_W, COL_VC), row_spec(SGU_W, COL_GC),
                  row_spec(D_MODEL, gm_blk), row_spec(D_MODEL, gm_blk + 1), row_spec(D_MODEL, gm_blk + 2),
                  row_spec(D_MODEL),
                  pl.BlockSpec((None, 1, 3 * D_MODEL), lambda i: (mod_row(i, tm), 0, 0)),
                  const_spec((1, D_MODEL)), const_spec((1, SGU_W)),
                  const_spec((SGU_GROUPS, SGU_CHUNK, SGU_CHUNK)), const_spec((SGU_CHUNK, SGU_GROUPS)),
                  w_spec(NA_W), w_spec(SSM_W), w_spec(SGU_W), w_spec(D_MODEL)],
        out_specs=row_spec(D_MODEL),
        out_shape=jax.ShapeDtypeStruct((t, D_MODEL), _F32),
        scratch_shapes=[pltpu.VMEM((tm, SGU_W), _BF16)],
        compiler_params=_params("arbitrary"),
        name="merge_out",
    )(o_a, o_b, p2, p2, p2, p2, p2, p2, x2d, mod_l, prm["g_post"], prm["g_sgu"], prm["w_s"], prm["b_s_t"],
      prm["w_br_a"], prm["w_br_b"], prm["w_br_c"], prm["w_out"])


def _layer(x3, mod_l, wts, mod_row, ctx, layer, caches=None, stacks=None):
    b, l, _ = x3.shape
    t = b * l
    x2d = x3.reshape(t, D_MODEL)
    p2, dt2, *kv = _inproj_call(x2d, mod_l, wts["g_pre"], wts["w_main"], layer, wts["w_dt"], mod_row,
                                want_kv=ctx is None)
    p3 = p2.reshape(b, l, N_MAIN)
    dt3 = dt2.reshape(b, l, DT_PAD)
    if ctx is None:
        o_a, caches = _ctx_attn_call(p3, kv[0].reshape(b, l, 2 * NA_W), layer, caches)
        h0_f = h0_b = None
    else:
        cache_k4, cache_v4, bias_tbl, h0_f, h0_b = ctx
        o_a = _na_call(p3, cache_k4, cache_v4, bias_tbl, layer)
    o_b, stacks = _ssd_calls(p3, dt3, wts, h0_f, h0_b, layer, stacks)
    y2d = _merge_call(o_a.reshape(t, NA_W), o_b.reshape(t, SSM_W), p2, x2d, mod_l, wts, mod_row)
    return y2d.reshape(b, l, D_MODEL), (caches, stacks)


def _layer_weights(l, g_pre, g_post, w_main, w_dt, conv_w, conv_b, dt_bias, a_log, d_skip, g_ssm, w_s, b_s, g_sgu,
                   w_br_a, w_br_b, w_br_c, w_out):
    dtb = jnp.pad(dt_bias[l].reshape(N_DT).astype(_F32), (0, DT_PAD - N_DT))
    alog = jnp.pad(a_log[l].reshape(N_DT).astype(_F32), (0, DT_PAD - N_DT))
    return {
        "g_pre": g_pre[l].reshape(1, D_MODEL), "g_post": g_post[l].reshape(1, D_MODEL),
        "w_main": w_main, "w_dt": w_dt,
        "conv_wt": conv_w[l].T, "conv_b": conv_b[l].reshape(1, CONV_CH),
        "dtb_row": dtb.reshape(1, DT_PAD), "alog_row": alog.reshape(1, DT_PAD),
        "dtb_col": dtb.reshape(DT_PAD, 1), "alog_col": alog.reshape(DT_PAD, 1),
        "dskip": jnp.repeat(d_skip[l].astype(_F32), SSM_HEADDIM).reshape(1, SSM_W),
        "g_ssm": g_ssm[l].reshape(1, SSM_W),
        "g_sgu": g_sgu[l].reshape(1, SGU_W), "w_s": w_s[l].astype(_BF16), "b_s_t": b_s[l].T,
        "w_br_a": w_br_a[l].astype(_BF16), "w_br_b": w_br_b[l].astype(_BF16),
        "w_br_c": w_br_c[l].astype(_BF16), "w_out": w_out[l].astype(_BF16),
    }


def kernel(x_prompt, x_sample, c, cache_k, cache_v, state_ssm_fwd, state_ssm_bwd, c_ctx, w_mod, b_mod, g_pre,
           g_post, w_in, rpb, conv_w, conv_b, dt_bias, a_log, d_skip, g_ssm, w_s, b_s, g_sgu, w_br_a, w_br_b,
           w_br_c, w_out):
    nb, ls, _ = x_sample.shape
    bp, lp, _ = x_prompt.shape
    assert 1 + nb <= MOD_ROWS
    assert ls % INPROJ_TM == 0 and (bp * lp) % INPROJ_TM == 0 and ls % MERGE_TM == 0
    assert lp % SSM_CHUNK == 0 and ls % (SSD_CHUNKS_PER_STEP * SSM_CHUNK) == 0 and MERGE_TM % SGU_CHUNK == 0
    assert ls % GRID_W == 0 and ls // GRID_W >= NA_KH
    cvecs = jnp.concatenate([c_ctx[None, :], c, jnp.zeros((MOD_ROWS - 1 - nb, D_MODEL), _F32)], axis=0)
    mod = _mod_call(cvecs, w_mod, b_mod).reshape(DEPTH, MOD_ROWS, 1, 3 * D_MODEL)
    past = cache_k.shape[2]
    cache_k4 = cache_k.reshape(nb, DEPTH, past, NA_W)
    cache_v4 = cache_v.reshape(nb, DEPTH, past, NA_W)

    def prompt_row(i, tm):
        return 0

    def sample_row(i, tm):
        return 1 + (i * tm) // ls

    y_p, y_s = x_prompt, x_sample
    caches = (jnp.zeros((bp, DEPTH, lp, NA_W), _F32), jnp.zeros((bp, DEPTH, lp, NA_W), _F32))
    stacks = None
    w_main, w_dt = _repack_call(jnp.swapaxes(w_in, 1, 2))
    for l in range(DEPTH):
        wts = _layer_weights(l, g_pre, g_post, w_main, w_dt, conv_w, conv_b, dt_bias, a_log, d_skip, g_ssm, w_s,
                             b_s, g_sgu, w_br_a, w_br_b, w_br_c, w_out)
        y_p, (caches, stacks) = _layer(y_p, mod[l], wts, prompt_row, None, l, caches, stacks)
        ctx = (cache_k4, cache_v4, _na_bias_table(rpb[l]), state_ssm_fwd, state_ssm_bwd)
        y_s, _ = _layer(y_s, mod[l], wts, sample_row, ctx, l)
    new_k = caches[0].reshape(bp, DEPTH, lp, NA_HEADS, NA_HEAD_DIM)
    new_v = caches[1].reshape(bp, DEPTH, lp, NA_HEADS, NA_HEAD_DIM)
    return (y_p, y_s, new_k, new_v, jnp.swapaxes(stacks[0], 0, 1), jnp.swapaxes(stacks[1], 0, 1))
```

```python
import functools

import jax
import jax.numpy as jnp
from jax import lax
from jax.experimental import pallas as pl
from jax.experimental.pallas import tpu as pltpu

D_MODEL = 2048
DEPTH = 2
EPS = 1e-6
GRID_W = 64
NA_HEAD_DIM = 64
NA_W = D_MODEL // 2
NA_HEADS = NA_W // NA_HEAD_DIM
NA_KH = 8
NA_KW = 16
SSM_HEADDIM = 64
SSM_W = D_MODEL // 2
SSM_HEADS = SSM_W // SSM_HEADDIM
SSM_GROUPS = 4
SSM_STATE = 128
SSM_CHUNK = 128
CONV_CH = SSM_W + 2 * SSM_GROUPS * SSM_STATE
SGU_W = D_MODEL // 2
SGU_GROUPS = 8
SGU_CHUNK = 128

SEG_W = 1024
N_MAIN = 16 * SEG_W
DT_PAD = 128
COL_Q, COL_K, COL_V, COL_GA, COL_XBC, COL_Z, COL_U, COL_VC, COL_GC, COL_GM = 0, 1, 2, 3, 4, 6, 7, 8, 9, 10

NEG = -1e30
LOG2E = 1.4426950408889634
MOD_ROWS = 8

V7X_VMEM_BYTES = 64 * 1024 * 1024
VMEM_LIMIT = V7X_VMEM_BYTES * 7 // 8
INPROJ_TM = 1024
MERGE_TM = 256
MOD_TN = 1024

_F32 = jnp.float32
_BF16 = jnp.bfloat16
_NT = (((1,), (1,)), ((), ()))


def _sigmoid(x):
    return 1.0 / (1.0 + jnp.exp(-x))


def _silu(x):
    return x * _sigmoid(x)


def _softplus(x):
    return jnp.maximum(x, 0.0) + jnp.log(1.0 + jnp.exp(-jnp.abs(x)))


def _dot(a, b):
    return jnp.dot(a, b, preferred_element_type=_F32)


def _params(*sem):
    return pltpu.CompilerParams(dimension_semantics=sem, vmem_limit_bytes=VMEM_LIMIT)


def _mod_body(c_ref, w_ref, b_ref, o_ref):
    s = _silu(c_ref[...]).astype(_BF16)
    o_ref[...] = _dot(s, w_ref[...].astype(_BF16)) + b_ref[...]


def _mod_call(cvecs, w_mod, b_mod):
    tn = MOD_TN
    n3 = 3 * D_MODEL
    return pl.pallas_call(
        _mod_body,
        grid=(DEPTH, n3 // tn),
        in_specs=[pl.BlockSpec((MOD_ROWS, D_MODEL), lambda l, j: (0, 0)),
                  pl.BlockSpec((None, D_MODEL, tn), lambda l, j: (l, 0, j)),
                  pl.BlockSpec((None, 1, tn), lambda l, j: (l, 0, j))],
        out_specs=pl.BlockSpec((None, MOD_ROWS, tn), lambda l, j: (l, 0, j)),
        out_shape=jax.ShapeDtypeStruct((DEPTH, MOD_ROWS, n3), _F32),
        compiler_params=_params("arbitrary", "arbitrary"),
        name="modulation",
    )(cvecs, w_mod, b_mod.reshape(DEPTH, 1, n3))


N_DT = 2 * SSM_HEADS
OFF_DT = 4 * NA_W + CONV_CH + SSM_W
REPACK_BLK = 1024


def _repack_body(a_ref, b_ref, o_ref, odt_ref):
    r = pl.program_id(1)
    cut = OFF_DT // REPACK_BLK

    @pl.when(r < cut)
    def _():
        o_ref[...] = a_ref[...].T.astype(_BF16)

    @pl.when(r >= cut)
    def _():
        o_ref[...] = jnp.concatenate([a_ref[N_DT:, :], b_ref[...]], axis=0).T.astype(_BF16)

    @pl.when(r == cut)
    def _():
        dt_rows = jnp.concatenate([a_ref[:N_DT, :], jnp.zeros((DT_PAD - N_DT, D_MODEL), _F32)], axis=0)
        odt_ref[...] = dt_rows.T.astype(_BF16)


def _repack_call(w_in_t):
    return pl.pallas_call(
        _repack_body,
        grid=(DEPTH, N_MAIN // REPACK_BLK),
        in_specs=[pl.BlockSpec((None, REPACK_BLK, D_MODEL), lambda l, r: (l, r, 0)),
                  pl.BlockSpec((None, N_DT, D_MODEL), lambda l, r: (l, (r + 1) * (REPACK_BLK // N_DT), 0))],
        out_specs=[pl.BlockSpec((None, D_MODEL, REPACK_BLK), lambda l, r: (l, 0, r)),
                   pl.BlockSpec((None, D_MODEL, DT_PAD), lambda l, r: (l, 0, 0))],
        out_shape=[jax.ShapeDtypeStruct((DEPTH, D_MODEL, N_MAIN), _BF16),
                   jax.ShapeDtypeStruct((DEPTH, D_MODEL, DT_PAD), _BF16)],
        compiler_params=_params("arbitrary", "arbitrary"),
        name="repack_w_in",
    )(w_in_t, w_in_t)


def _inproj_body(*refs, want_kv):
    x_ref, mod_ref, g_ref, w_ref, wdt_ref, o_ref, dt_ref = refs[:7]
    h_scr = refs[-1]
    j = pl.program_id(1)

    @pl.when(j == 0)
    def _():
        x = x_ref[...]
        xn = x * lax.rsqrt(jnp.mean(x * x, axis=-1, keepdims=True) + EPS) * g_ref[...]
        shift = mod_ref[:, 0:D_MODEL]
        scale = mod_ref[:, D_MODEL:2 * D_MODEL]
        h = (xn * (1.0 + scale) + shift).astype(_BF16)
        h_scr[...] = h
        dt_ref[...] = _dot(h, wdt_ref[...])

    res = _dot(h_scr[...], w_ref[...])
    o_ref[...] = res.astype(_BF16)
    if want_kv:
        kv_ref = refs[7]

        @pl.when((j == COL_K) | (j == COL_V))
        def _():
            kv_ref[...] = res


def _inproj_call(x2d, mod_l, g_pre_l, w_main, layer, w_dt, mod_row, want_kv):
    t = x2d.shape[0]
    tm = INPROJ_TM
    tn = SEG_W if want_kv else 2 * SEG_W
    out_specs = [pl.BlockSpec((tm, tn), lambda i, j: (i, j)),
                 pl.BlockSpec((tm, DT_PAD), lambda i, j: (i, 0))]
    out_shape = [jax.ShapeDtypeStruct((t, N_MAIN), _BF16),
                 jax.ShapeDtypeStruct((t, DT_PAD), _F32)]
    if want_kv:
        out_specs.append(pl.BlockSpec((tm, NA_W), lambda i, j: (i, jnp.clip(j - COL_K, 0, COL_V - COL_K))))
        out_shape.append(jax.ShapeDtypeStruct((t, 2 * NA_W), _F32))
    return pl.pallas_call(
        functools.partial(_inproj_body, want_kv=want_kv),
        grid=(t // tm, N_MAIN // tn),
        in_specs=[pl.BlockSpec((tm, D_MODEL), lambda i, j: (i, 0)),
                  pl.BlockSpec((None, 1, 3 * D_MODEL), lambda i, j: (mod_row(i, tm), 0, 0)),
                  pl.BlockSpec((1, D_MODEL), lambda i, j: (0, 0)),
                  pl.BlockSpec((None, D_MODEL, tn), lambda i, j: (layer, 0, j)),
                  pl.BlockSpec((None, D_MODEL, DT_PAD), lambda i, j: (layer, 0, 0))],
        out_specs=out_specs,
        out_shape=out_shape,
        scratch_shapes=[pltpu.VMEM((tm, D_MODEL), _BF16)],
        compiler_params=_params("arbitrary", "arbitrary"),
        name="inproj",
    )(x2d, mod_l, g_pre_l, w_main, w_dt)


def _stack_heads(q, head0):
    q = q * (NA_HEAD_DIM ** -0.5)
    zero = jnp.zeros_like(q)
    return jnp.concatenate([jnp.where(head0, q, zero), jnp.where(head0, zero, q)], axis=0)


def _ctx_attn_body(q_ref, k_ref, v_ref, ga_ref, kf_ref, vf_ref, kbuf_ref, vbuf_ref, o_ref, ko_ref, vo_ref):
    del kbuf_ref, vbuf_ref
    l = q_ref.shape[0]
    hp = 2 * NA_HEAD_DIM
    ko_ref[...] = kf_ref[...]
    vo_ref[...] = vf_ref[...]
    head0 = lax.broadcasted_iota(jnp.int32, (l, hp), 1) < NA_HEAD_DIM
    ones = jnp.ones((l, hp), _BF16)
    for j in range(NA_HEADS // 2):
        cols = slice(j * hp, (j + 1) * hp)
        q2 = _stack_heads(q_ref[:, cols], head0)
        s = lax.dot_general(q2, k_ref[:, cols], _NT, preferred_element_type=_F32)
        p = jnp.exp(s - jnp.max(s, axis=-1, keepdims=True)).astype(_BF16)
        v_ext = jnp.concatenate([v_ref[:, cols], ones], axis=1)
        oe = _dot(p, v_ext)
        on = oe[:, :hp] / oe[:, hp:]
        o = jnp.where(head0, on[:l], on[l:])
        o_ref[:, cols] = (o * _silu(ga_ref[:, cols].astype(_F32))).astype(_BF16)


def _ctx_attn_call(p3, kv3, layer, caches):
    b, l, _ = p3.shape

    def spec(col):
        return pl.BlockSpec((None, l, NA_W), lambda i: (i, 0, col))

    cache_spec = pl.BlockSpec((None, None, l, NA_W), lambda i: (i, layer, 0, 0))
    cache_shape = jax.ShapeDtypeStruct((b, DEPTH, l, NA_W), _F32)
    any_spec = pl.BlockSpec(memory_space=pl.ANY)
    o_a, k_all, v_all = pl.pallas_call(
        _ctx_attn_body,
        grid=(b,),
        in_specs=[spec(COL_Q), spec(COL_K), spec(COL_V), spec(COL_GA), spec(0), spec(1), any_spec, any_spec],
        out_specs=[pl.BlockSpec((None, l, NA_W), lambda i: (i, 0, 0)), cache_spec, cache_spec],
        out_shape=[jax.ShapeDtypeStruct((b, l, NA_W), _BF16), cache_shape, cache_shape],
        input_output_aliases={6: 1, 7: 2},
        compiler_params=_params("arbitrary"),
        name="ctx_attn",
    )(p3, p3, p3, p3, kv3, kv3, *caches)
    return o_a, (k_all, v_all)


def _na_bias_table(rpb_l):
    nk = 2 * NA_KW - 1
    j = jnp.arange(GRID_W)[None, :, None]
    c = jnp.arange(GRID_W)[None, None, :]
    k = jnp.arange(nk)[:, None, None]
    cs = jnp.clip(j - NA_KW // 2, 0, GRID_W - NA_KW)
    valid = (c >= cs) & (c < cs + NA_KW)
    sel = (valid & (c - j + (NA_KW - 1) == k)).astype(_F32)
    zeros = jnp.zeros_like(sel)
    sel2 = jnp.concatenate([jnp.concatenate([sel, zeros], axis=-1), jnp.concatenate([zeros, sel], axis=-1)], axis=0)
    r = rpb_l.astype(_F32)
    r2 = jnp.concatenate([r[:, :-1], r[:, 1:]], axis=-1)
    tbl = jnp.einsum("hdm,mjn->hdjn", r2, sel2, precision=lax.Precision.HIGHEST)
    mask = jnp.where(valid, 0.0, NEG)[0]
    return tbl + jnp.concatenate([mask, mask], axis=-1)[None, None]


def _na_body(q_ref, k_ref, v_ref, ga_ref, kc_ref, vc_ref, bias_ref, o_ref, vb_scr, kcb_scr, vcb_scr, *, rows):
    hp = 2 * NA_HEAD_DIM
    win = NA_KH * GRID_W
    vb_scr[:, :hp] = v_ref[...]
    vb_scr[:, hp:] = jnp.ones((vb_scr.shape[0], hp), _BF16)
    kcb_scr[...] = kc_ref[...].astype(_BF16)
    vcb_scr[:, :hp] = vc_ref[...].T.astype(_BF16)
    vcb_scr[:, hp:] = jnp.ones((vcb_scr.shape[0], hp), _BF16)
    head0 = lax.broadcasted_iota(jnp.int32, (GRID_W, hp), 1) < NA_HEAD_DIM

    def row(r, carry):
        rs = jnp.clip(r - NA_KH // 2, 0, rows - NA_KH)
        dr0 = rs - r + (NA_KH - 1)
        q0 = pl.multiple_of(r * GRID_W, GRID_W)
        k0 = pl.multiple_of(rs * GRID_W, GRID_W)
        q2 = _stack_heads(q_ref[pl.ds(q0, GRID_W), :], head0)
        bias = jnp.concatenate(
            [jnp.concatenate([bias_ref[hh, dr0 + a] for a in range(0, NA_KH, 2)], axis=1) for hh in range(2)],
            axis=0)
        s_loc = lax.dot_general(q2, k_ref[pl.ds(k0, win), :], _NT, preferred_element_type=_F32) + bias
        s_ctx = _dot(q2, kcb_scr[...])
        m = jnp.maximum(jnp.max(s_loc, axis=-1, keepdims=True), jnp.max(s_ctx, axis=-1, keepdims=True))
        p_loc = jnp.exp(s_loc - m).astype(_BF16)
        p_ctx = jnp.exp(s_ctx - m).astype(_BF16)
        oe = _dot(p_loc, vb_scr[pl.ds(k0, win), :]) + _dot(p_ctx, vcb_scr[...])
        on = oe[:, :hp] / oe[:, hp:]
        o = jnp.where(head0, on[:GRID_W], on[GRID_W:])
        o_ref[pl.ds(q0, GRID_W), :] = (o * _silu(ga_ref[pl.ds(q0, GRID_W), :].astype(_F32))).astype(_BF16)
        return carry

    lax.fori_loop(0, rows, row, 0, unroll=True)


def _na_call(p3, cache_k4, cache_v4, bias_tbl, layer):
    b, l, _ = p3.shape
    rows = l // GRID_W
    assert rows >= NA_KH
    hp = 2 * NA_HEAD_DIM
    nblk = SEG_W // hp
    lc = cache_k4.shape[3]

    def spec(col):
        return pl.BlockSpec((None, l, hp), lambda i, j: (i, 0, col * nblk + j))

    cspec = pl.BlockSpec((None, None, hp, lc), lambda i, j: (i, layer, j, 0))
    return pl.pallas_call(
        functools.partial(_na_body, rows=rows),
        grid=(b, NA_HEADS // 2),
        in_specs=[spec(COL_Q), spec(COL_K), spec(COL_V), spec(COL_GA), cspec, cspec,
                  pl.BlockSpec((2, 2 * NA_KH - 2, GRID_W, 2 * GRID_W), lambda i, j: (j, 0, 0, 0))],
        out_specs=pl.BlockSpec((None, l, hp), lambda i, j: (i, 0, j)),
        out_shape=jax.ShapeDtypeStruct((b, l, NA_W), _BF16),
        scratch_shapes=[pltpu.VMEM((l, 2 * hp), _BF16),
                        pltpu.VMEM((hp, lc), _BF16), pltpu.VMEM((lc, 2 * hp), _BF16)],
        compiler_params=_params("arbitrary", "arbitrary"),
        name="na_attn",
    )(p3, p3, p3, p3, cache_k4, cache_v4, bias_tbl)


HEADS_PER_GROUP = SSM_HEADS // SSM_GROUPS
GROUP_W = HEADS_PER_GROUP * SSM_HEADDIM
SSD_CHUNKS_PER_STEP = 4
SSD_HALO = 16


def _split3(x):
    hi = x.astype(_BF16)
    r1 = x - hi.astype(_F32)
    mid = r1.astype(_BF16)
    lo = (r1 - mid.astype(_F32)).astype(_BF16)
    return hi, mid, lo


def _silu_tanh(x):
    h = 0.5 * x
    return h + h * jnp.tanh(h)


def _ssd_chunk(xs_b, bm, cm, dtraw, dtb_row_ref, alog_row_ref, dtb_col_ref, alog_col_ref, ht_scr, reverse):
    q = SSM_CHUNK
    d = 1 if reverse else 0
    last = 0 if reverse else q - 1

    hs = slice(d * SSM_HEADS, (d + 1) * SSM_HEADS)
    dta_col = _softplus(dtraw + dtb_row_ref[...]) * (-jnp.exp(alog_row_ref[...]))
    dt_row = _softplus(dtraw.T[hs, :] + dtb_col_ref[hs, :])
    dta_row = dt_row * (-jnp.exp(alog_col_ref[hs, :]))

    ri = lax.broadcasted_iota(jnp.int32, (q, q), 0)
    ci = lax.broadcasted_iota(jnp.int32, (q, q), 1)
    causal = (ci >= ri) if reverse else (ci <= ri)
    t_col = jnp.where(causal, 1.0, 0.0).astype(_BF16)
    t_row = jnp.where((ri >= ci) if reverse else (ri <= ci), 1.0, 0.0).astype(_BF16)
    acum_col = sum(_dot(t_col, part) for part in _split3(dta_col)) * LOG2E
    acum_row = sum(_dot(part, t_row) for part in _split3(dta_row)) * LOG2E

    a_last = jnp.broadcast_to(acum_row[:, last:last + 1], (SSM_HEADS, q))
    dtde_row = dt_row * jnp.exp2(a_last - acum_row)
    cdecay = jnp.exp2(a_last)

    lo_half = lax.broadcasted_iota(jnp.int32, (1, q), 1) < SSM_HEADDIM
    lane_head = lax.broadcasted_iota(jnp.int32, (1, GROUP_W), 1) // SSM_HEADDIM
    zero_b = jnp.zeros((q, GROUP_W), _BF16)

    def per_head_lanes(vals):
        return jnp.concatenate([jnp.where(lo_half, vals[0], vals[1]), jnp.where(lo_half, vals[2], vals[3])], axis=1)

    ys = []
    for g in range(SSM_GROUPS):
        nsl = slice(g * SSM_STATE, (g + 1) * SSM_STATE)
        b_g = bm[:, nsl]
        c_g = cm[:, nsl]
        cb = lax.dot_general(c_g, b_g, _NT, preferred_element_type=_F32)
        b_t = b_g.astype(_F32).T
        xs_g = xs_b[:, g * GROUP_W:(g + 1) * GROUP_W]
        m_parts, bt_parts, xbd_parts, bcs = [], [], [], []
        for hh in range(HEADS_PER_GROUP):
            h = g * HEADS_PER_GROUP + hh
            lane = d * SSM_HEADS + h
            bc = jnp.broadcast_to(acum_col[:, lane:lane + 1], (q, q))
            lmat = jnp.exp2(jnp.where(causal, bc - acum_row[h:h + 1, :], NEG))
            m_parts.append((cb * lmat * dt_row[h:h + 1, :]).astype(_BF16))
            bt_parts.append((b_t * dtde_row[h:h + 1, :]).astype(_BF16))
            xbd_parts.append(jnp.where(lane_head == hh, xs_g, zero_b))
            bcs.append(bc)
        lhs = jnp.concatenate([jnp.concatenate(m_parts, axis=1), jnp.concatenate(bt_parts, axis=1)], axis=0)
        res = _dot(lhs, jnp.concatenate(xbd_parts, axis=0))
        h_t = ht_scr[g]
        y_off = _dot(c_g, h_t.astype(_BF16)) * jnp.exp2(per_head_lanes(bcs))
        ys.append(res[:q] + y_off)
        h0 = g * HEADS_PER_GROUP
        cd = per_head_lanes([cdecay[h0 + hh:h0 + hh + 1, :] for hh in range(HEADS_PER_GROUP)])
        ht_scr[g] = h_t * cd + res[q:]
    return ys


def _ssd_state_io(c, nc, h0_ref, stack_ref, hout_ref, ht_scr):
    @pl.when(c == 0)
    def _():
        for g in range(SSM_GROUPS):
            if h0_ref is None:
                ht_scr[g] = jnp.zeros((SSM_STATE, GROUP_W), _F32)
            else:
                hs = h0_ref[g * HEADS_PER_GROUP:(g + 1) * HEADS_PER_GROUP]
                ht_scr[g] = hs.reshape(GROUP_W, SSM_STATE).T

    def store_final():
        @pl.when(c == nc - 1)
        def _():
            slot = hout_ref.shape[0] - 1
            if stack_ref is not None:
                hout_ref[0:slot] = stack_ref[...]
            for g in range(SSM_GROUPS):
                hout_ref[slot, g * HEADS_PER_GROUP:(g + 1) * HEADS_PER_GROUP] = ht_scr[g].T.reshape(
                    HEADS_PER_GROUP, SSM_HEADDIM, SSM_STATE)

    return store_final


def _ssd_fwd_body(*refs, has_h0, has_stack, nc):
    it = iter(refs)
    x_ref, prev_ref, next_ref, dt_ref, cw_ref, cb_ref = (next(it) for _ in range(6))
    dec_refs = [next(it) for _ in range(4)]
    dskip_ref = next(it)
    h0_ref = next(it) if has_h0 else None
    stack_ref = next(it) if has_stack else None
    y_ref, act_ref, hout_ref, ht_scr = (next(it) for _ in range(4))
    q = SSM_CHUNK
    qb = x_ref.shape[0]
    c = pl.program_id(1)
    store_final = _ssd_state_io(c, nc, h0_ref, stack_ref, hout_ref, ht_scr)

    x = x_ref[...].astype(_F32)
    xp = jnp.where(c > 0, prev_ref[SSD_HALO - 1:SSD_HALO, :].astype(_F32), 0.0)
    xn = jnp.where(c < nc - 1, next_ref[0:1, :].astype(_F32), 0.0)
    sub = lax.broadcasted_iota(jnp.int32, (8, 1), 0)
    x_m1 = pltpu.roll(x, 1, axis=0)
    x_m1 = jnp.concatenate([jnp.where(sub == 0, xp, x_m1[:8]), x_m1[8:]], axis=0)
    x_p1 = pltpu.roll(x, qb - 1, axis=0)
    x_p1 = jnp.concatenate([x_p1[:qb - 8], jnp.where(sub == 7, xn, x_p1[qb - 8:])], axis=0)
    act = _silu_tanh(cw_ref[0:1, :] * x_m1 + cw_ref[1:2, :] * x + cw_ref[2:3, :] * x_p1 + cb_ref[...])
    act_b = act.astype(_BF16)
    act_ref[...] = act_b

    for s in range(qb // q):
        rows = slice(s * q, (s + 1) * q)
        ys = _ssd_chunk(act_b[rows, :SSM_W], act_b[rows, SSM_W:SSM_W + SSM_GROUPS * SSM_STATE],
                        act_b[rows, SSM_W + SSM_GROUPS * SSM_STATE:], dt_ref[rows, :], *dec_refs, ht_scr, False)
        y_ref[rows, :] = jnp.concatenate(ys, axis=1) + act[rows, :SSM_W] * dskip_ref[...]
    store_final()


def _ssd_bwd_body(*refs, has_h0, has_stack, nc):
    it = iter(refs)
    act_ref, dt_ref = next(it), next(it)
    dec_refs = [next(it) for _ in range(4)]
    yf_ref, z_ref, gssm_ref = next(it), next(it), next(it)
    h0_ref = next(it) if has_h0 else None
    stack_ref = next(it) if has_stack else None
    o_ref, hout_ref, ht_scr = next(it), next(it), next(it)
    c = pl.program_id(1)
    store_final = _ssd_state_io(c, nc, h0_ref, stack_ref, hout_ref, ht_scr)

    q = SSM_CHUNK
    for s in reversed(range(act_ref.shape[0] // q)):
        rows = slice(s * q, (s + 1) * q)
        ys = _ssd_chunk(act_ref[rows, :SSM_W], act_ref[rows, SSM_W:SSM_W + SSM_GROUPS * SSM_STATE],
                        act_ref[rows, SSM_W + SSM_GROUPS * SSM_STATE:], dt_ref[rows, :], *dec_refs, ht_scr, True)
        y = (yf_ref[rows, :] + jnp.concatenate(ys, axis=1)) * _silu_tanh(z_ref[rows, :].astype(_F32))
        y = y * lax.rsqrt(jnp.mean(y * y, axis=-1, keepdims=True) + EPS) * gssm_ref[...]
        o_ref[rows, :] = y.astype(_BF16)
    store_final()


def _ssd_calls(p3, dt3, prm, h0_f, h0_b, layer, stacks):
    b, l, _ = p3.shape
    q = min(SSD_CHUNKS_PER_STEP * SSM_CHUNK, l)
    nc = l // q
    has_h0 = h0_f is not None
    has_stack = stacks is not None
    slots = stacks[0].shape[0] + 1 if has_stack else 1
    xbc_blk = COL_XBC // 2

    def vec(w):
        return pl.BlockSpec((1, w), lambda i, c: (0, 0))

    dec_specs = [vec(DT_PAD), vec(DT_PAD),
                 pl.BlockSpec((DT_PAD, 1), lambda i, c: (0, 0)), pl.BlockSpec((DT_PAD, 1), lambda i, c: (0, 0))]
    dec_args = [prm["dtb_row"], prm["alog_row"], prm["dtb_col"], prm["alog_col"]]
    h0_spec = pl.BlockSpec((None, None, SSM_HEADS, SSM_HEADDIM, SSM_STATE), lambda i, c: (i, layer, 0, 0, 0))
    stack_spec = pl.BlockSpec((slots - 1, None, SSM_HEADS, SSM_HEADDIM, SSM_STATE), lambda i, c: (0, i, 0, 0, 0))
    state_spec = pl.BlockSpec((slots, None, SSM_HEADS, SSM_HEADDIM, SSM_STATE), lambda i, c: (0, i, 0, 0, 0))
    state_shape = jax.ShapeDtypeStruct((slots, b, SSM_HEADS, SSM_HEADDIM, SSM_STATE), _F32)
    ht_scratch = pltpu.VMEM((SSM_GROUPS, SSM_STATE, GROUP_W), _F32)

    in_specs = [
        pl.BlockSpec((None, q, CONV_CH), lambda i, c: (i, c, xbc_blk)),
        pl.BlockSpec((None, SSD_HALO, CONV_CH),
                     lambda i, c: (i, jnp.maximum(c * (q // SSD_HALO) - 1, 0), xbc_blk)),
        pl.BlockSpec((None, SSD_HALO, CONV_CH),
                     lambda i, c: (i, jnp.minimum((c + 1) * (q // SSD_HALO), l // SSD_HALO - 1), xbc_blk)),
        pl.BlockSpec((None, q, DT_PAD), lambda i, c: (i, c, 0)),
        pl.BlockSpec((3, CONV_CH), lambda i, c: (0, 0)), vec(CONV_CH)] + dec_specs + [vec(SSM_W)]
    args = [p3, p3, p3, dt3, prm["conv_wt"], prm["conv_b"]] + dec_args + [prm["dskip"]]
    if has_h0:
        in_specs.append(h0_spec)
        args.append(h0_f)
    if has_stack:
        in_specs.append(stack_spec)
        args.append(stacks[0])
    y_f, act, h_f = pl.pallas_call(
        functools.partial(_ssd_fwd_body, has_h0=has_h0, has_stack=has_stack, nc=nc),
        grid=(b, nc),
        in_specs=in_specs,
        out_specs=[pl.BlockSpec((None, q, SSM_W), lambda i, c: (i, c, 0)),
                   pl.BlockSpec((None, q, CONV_CH), lambda i, c: (i, c, 0)), state_spec],
        out_shape=[jax.ShapeDtypeStruct((b, l, SSM_W), _F32),
                   jax.ShapeDtypeStruct((b, l, CONV_CH), _BF16), state_shape],
        scratch_shapes=[ht_scratch],
        compiler_params=_params("arbitrary", "arbitrary"),
        name="ssd_fwd",
    )(*args)

    def rc(c):
        return nc - 1 - c

    in_specs = [pl.BlockSpec((None, q, CONV_CH), lambda i, c: (i, rc(c), 0)),
                pl.BlockSpec((None, q, DT_PAD), lambda i, c: (i, rc(c), 0))] + dec_specs + [
        pl.BlockSpec((None, q, SSM_W), lambda i, c: (i, rc(c), 0)),
        pl.BlockSpec((None, q, SSM_W), lambda i, c: (i, rc(c), COL_Z)), vec(SSM_W)]
    args = [act, dt3] + dec_args + [y_f, p3, prm["g_ssm"]]
    if has_h0:
        in_specs.append(h0_spec)
        args.append(h0_b)
    if has_stack:
        in_specs.append(stack_spec)
        args.append(stacks[1])
    o_b, h_b = pl.pallas_call(
        functools.partial(_ssd_bwd_body, has_h0=has_h0, has_stack=has_stack, nc=nc),
        grid=(b, nc),
        in_specs=in_specs,
        out_specs=[pl.BlockSpec((None, q, SSM_W), lambda i, c: (i, rc(c), 0)), state_spec],
        out_shape=[jax.ShapeDtypeStruct((b, l, SSM_W), _BF16), state_shape],
        scratch_shapes=[ht_scratch],
        compiler_params=_params("arbitrary", "arbitrary"),
        name="ssd_bwd",
    )(*args)
    return o_b, (h_f, h_b)


def _sgu_tile(u_ref, v_ref, g_ref, gs_ref, ws_ref, bs_ref, o_ref):
    v = v_ref[...].astype(_F32)
    mu = jnp.mean(v, axis=-1, keepdims=True)
    vc = v - mu
    var = jnp.mean(vc * vc, axis=-1, keepdims=True)
    vn = (vc * lax.rsqrt(var + EPS) * gs_ref[...]).astype(_BF16)
    ge = SGU_W // SGU_GROUPS
    for ch in range(v.shape[0] // SGU_CHUNK):
        rsl = slice(ch * SGU_CHUNK, (ch + 1) * SGU_CHUNK)
        for g in range(SGU_GROUPS):
            csl = slice(g * ge, (g + 1) * ge)
            vs = _dot(ws_ref[g], vn[rsl, csl]) + bs_ref[:, g:g + 1]
            y = u_ref[rsl, csl].astype(_F32) * vs * _silu(g_ref[rsl, csl].astype(_F32))
            o_ref[rsl, csl] = y.astype(_BF16)


def _merge_body(oa_ref, ob_ref, u_ref, vc_ref, gc_ref, ga_ref, gb_ref, gcm_ref, x_ref, mod_ref, gpost_ref,
                gs_ref, ws_ref, bs_ref, wa_ref, wb_ref, wc_ref, wo_ref, out_ref, oc_scr):
    _sgu_tile(u_ref, vc_ref, gc_ref, gs_ref, ws_ref, bs_ref, oc_scr)
    merged = _sigmoid(ga_ref[...].astype(_F32)) * _dot(oa_ref[...], wa_ref[...])
    merged = merged + _sigmoid(gb_ref[...].astype(_F32)) * _dot(ob_ref[...], wb_ref[...])
    merged = merged + _sigmoid(gcm_ref[...].astype(_F32)) * _dot(oc_scr[...], wc_ref[...])
    y = _dot(merged.astype(_BF16), wo_ref[...])
    y = y * lax.rsqrt(jnp.mean(y * y, axis=-1, keepdims=True) + EPS) * gpost_ref[...]
    out_ref[...] = x_ref[...] + mod_ref[:, 2 * D_MODEL:] * y


def _merge_call(o_a, o_b, p2, x2d, mod_l, prm, mod_row):
    t = x2d.shape[0]
    tm = MERGE_TM
    gm_blk = COL_GM // 2
    once = pl.Buffered(1)

    def row_spec(w, col=0):
        return pl.BlockSpec((tm, w), lambda i: (i, col))

    def const_spec(shape):
        return pl.BlockSpec(shape, lambda i: (0,) * len(shape))

    def w_spec(kdim):
        return pl.BlockSpec((kdim, D_MODEL), lambda i: (0, 0), pipeline_mode=once)

    return pl.pallas_call(
        _merge_body,
        grid=(t // tm,),
        in_specs=[row_spec(NA_W), row_spec(SSM_W),
                  row_spec(SGU_W, COL_U), row_spec(SGU_W, COL_VC), row_spec(SGU_W, COL_GC),
                  row_spec(D_MODEL, gm_blk), row_spec(D_MODEL, gm_blk + 1), row_spec(D_MODEL, gm_blk + 2),
                  row_spec(D_MODEL),
                  pl.BlockSpec((None, 1, 3 * D_MODEL), lambda i: (mod_row(i, tm), 0, 0)),
                  const_spec((1, D_MODEL)), const_spec((1, SGU_W)),
                  const_spec((SGU_GROUPS, SGU_CHUNK, SGU_CHUNK)), const_spec((SGU_CHUNK, SGU_GROUPS)),
                  w_spec(NA_W), w_spec(SSM_W), w_spec(SGU_W), w_spec(D_MODEL)],
        out_specs=row_spec(D_MODEL),
        out_shape=jax.ShapeDtypeStruct((t, D_MODEL), _F32),
        scratch_shapes=[pltpu.VMEM((tm, SGU_W), _BF16)],
        compiler_params=_params("arbitrary"),
        name="merge_out",
    )(o_a, o_b, p2, p2, p2, p2, p2, p2, x2d, mod_l, prm["g_post"], prm["g_sgu"], prm["w_s"], prm["b_s_t"],
      prm["w_br_a"], prm["w_br_b"], prm["w_br_c"], prm["w_out"])


def _layer(x3, mod_l, wts, mod_row, ctx, layer, caches=None, stacks=None):
    b, l, _ = x3.shape
    t = b * l
    x2d = x3.reshape(t, D_MODEL)
    p2, dt2, *kv = _inproj_call(x2d, mod_l, wts["g_pre"], wts["w_main"], layer, wts["w_dt"], mod_row,
                                want_kv=ctx is None)
    p3 = p2.reshape(b, l, N_MAIN)
    dt3 = dt2.reshape(b, l, DT_PAD)
    if ctx is None:
        o_a, caches = _ctx_attn_call(p3, kv[0].reshape(b, l, 2 * NA_W), layer, caches)
        h0_f = h0_b = None
    else:
        cache_k4, cache_v4, bias_tbl, h0_f, h0_b = ctx
        o_a = _na_call(p3, cache_k4, cache_v4, bias_tbl, layer)
    o_b, stacks = _ssd_calls(p3, dt3, wts, h0_f, h0_b, layer, stacks)
    y2d = _merge_call(o_a.reshape(t, NA_W), o_b.reshape(t, SSM_W), p2, x2d, mod_l, wts, mod_row)
    return y2d.reshape(b, l, D_MODEL), (caches, stacks)


def _layer_weights(l, g_pre, g_post, w_main, w_dt, conv_w, conv_b, dt_bias, a_log, d_skip, g_ssm, w_s, b_s, g_sgu,
                   w_br_a, w_br_b, w_br_c, w_out):
    dtb = jnp.pad(dt_bias[l].reshape(N_DT).astype(_F32), (0, DT_PAD - N_DT))
    alog = jnp.pad(a_log[l].reshape(N_DT).astype(_F32), (0, DT_PAD - N_DT))
    return {
        "g_pre": g_pre[l].reshape(1, D_MODEL), "g_post": g_post[l].reshape(1, D_MODEL),
        "w_main": w_main, "w_dt": w_dt,
        "conv_wt": conv_w[l].T, "conv_b": conv_b[l].reshape(1, CONV_CH),
        "dtb_row": dtb.reshape(1, DT_PAD), "alog_row": alog.reshape(1, DT_PAD),
        "dtb_col": dtb.reshape(DT_PAD, 1), "alog_col": alog.reshape(DT_PAD, 1),
        "dskip": jnp.repeat(d_skip[l].astype(_F32), SSM_HEADDIM).reshape(1, SSM_W),
        "g_ssm": g_ssm[l].reshape(1, SSM_W),
        "g_sgu": g_sgu[l].reshape(1, SGU_W), "w_s": w_s[l].astype(_BF16), "b_s_t": b_s[l].T,
        "w_br_a": w_br_a[l].astype(_BF16), "w_br_b": w_br_b[l].astype(_BF16),
        "w_br_c": w_br_c[l].astype(_BF16), "w_out": w_out[l].astype(_BF16),
    }


def kernel(x_prompt, x_sample, c, cache_k, cache_v, state_ssm_fwd, state_ssm_bwd, c_ctx, w_mod, b_mod, g_pre,
           g_post, w_in, rpb, conv_w, conv_b, dt_bias, a_log, d_skip, g_ssm, w_s, b_s, g_sgu, w_br_a, w_br_b,
           w_br_c, w_out):
    nb, ls, _ = x_sample.shape
    bp, lp, _ = x_prompt.shape
    assert 1 + nb <= MOD_ROWS
    assert ls % INPROJ_TM == 0 and (bp * lp) % INPROJ_TM == 0 and ls % MERGE_TM == 0
    assert lp % SSM_CHUNK == 0 and ls % (SSD_CHUNKS_PER_STEP * SSM_CHUNK) == 0 and MERGE_TM % SGU_CHUNK == 0
    assert ls % GRID_W == 0 and ls // GRID_W >= NA_KH
    cvecs = jnp.concatenate([c_ctx[None, :], c, jnp.zeros((MOD_ROWS - 1 - nb, D_MODEL), _F32)], axis=0)
    mod = _mod_call(cvecs, w_mod, b_mod).reshape(DEPTH, MOD_ROWS, 1, 3 * D_MODEL)
    past = cache_k.shape[2]
    cache_k4 = cache_k.transpose(0, 1, 3, 4, 2).reshape(nb, DEPTH, NA_W, past)
    cache_v4 = cache_v.transpose(0, 1, 3, 4, 2).reshape(nb, DEPTH, NA_W, past)

    def prompt_row(i, tm):
        return 0

    def sample_row(i, tm):
        return 1 + (i * tm) // ls

    y_p, y_s = x_prompt, x_sample
    caches = (jnp.zeros((bp, DEPTH, lp, NA_W), _F32), jnp.zeros((bp, DEPTH, lp, NA_W), _F32))
    stacks = None
    w_main, w_dt = _repack_call(jnp.swapaxes(w_in, 1, 2))
    for l in range(DEPTH):
        wts = _layer_weights(l, g_pre, g_post, w_main, w_dt, conv_w, conv_b, dt_bias, a_log, d_skip, g_ssm, w_s,
                             b_s, g_sgu, w_br_a, w_br_b, w_br_c, w_out)
        y_p, (caches, stacks) = _layer(y_p, mod[l], wts, prompt_row, None, l, caches, stacks)
        ctx = (cache_k4, cache_v4, _na_bias_table(rpb[l]), state_ssm_fwd, state_ssm_bwd)
        y_s, _ = _layer(y_s, mod[l], wts, sample_row, ctx, l)
    new_k = caches[0].reshape(bp, DEPTH, lp, NA_HEADS, NA_HEAD_DIM)
    new_v = caches[1].reshape(bp, DEPTH, lp, NA_HEADS, NA_HEAD_DIM)
    return (y_p, y_s, new_k, new_v, jnp.swapaxes(stacks[0], 0, 1), jnp.swapaxes(stacks[1], 0, 1))
```

```python
import functools

import jax
import jax.numpy as jnp
from jax import lax
from jax.experimental import pallas as pl
from jax.experimental.pallas import tpu as pltpu

D_MODEL = 2048
DEPTH = 2
EPS = 1e-6
GRID_W = 64
NA_HEAD_DIM = 64
NA_W = D_MODEL // 2
NA_HEADS = NA_W // NA_HEAD_DIM
NA_KH = 8
NA_KW = 16
SSM_HEADDIM = 64
SSM_W = D_MODEL // 2
SSM_HEADS = SSM_W // SSM_HEADDIM
SSM_GROUPS = 4
SSM_STATE = 128
SSM_CHUNK = 128
CONV_CH = SSM_W + 2 * SSM_GROUPS * SSM_STATE
SGU_W = D_MODEL // 2
SGU_GROUPS = 8
SGU_CHUNK = 128

SEG_W = 1024
N_MAIN = 16 * SEG_W
DT_PAD = 128
COL_Q, COL_K, COL_V, COL_GA, COL_XBC, COL_Z, COL_U, COL_VC, COL_GC, COL_GM = 0, 1, 2, 3, 4, 6, 7, 8, 9, 10

NEG = -1e30
LOG2E = 1.4426950408889634
MOD_ROWS = 8

V7X_VMEM_BYTES = 64 * 1024 * 1024
VMEM_LIMIT = V7X_VMEM_BYTES * 7 // 8
INPROJ_TM = 1024
MERGE_TM = 256
MOD_TN = 1024

_F32 = jnp.float32
_BF16 = jnp.bfloat16
_NT = (((1,), (1,)), ((), ()))


def _sigmoid(x):
    return 1.0 / (1.0 + jnp.exp(-x))


def _silu(x):
    return x * _sigmoid(x)


def _softplus(x):
    return jnp.maximum(x, 0.0) + jnp.log(1.0 + jnp.exp(-jnp.abs(x)))


def _dot(a, b):
    return jnp.dot(a, b, preferred_element_type=_F32)


def _params(*sem):
    return pltpu.CompilerParams(dimension_semantics=sem, vmem_limit_bytes=VMEM_LIMIT)


def _mod_body(c_ref, w_ref, b_ref, o_ref):
    s = _silu(c_ref[...]).astype(_BF16)
    o_ref[...] = _dot(s, w_ref[...].astype(_BF16)) + b_ref[...]


def _mod_call(cvecs, w_mod, b_mod):
    tn = MOD_TN
    n3 = 3 * D_MODEL
    return pl.pallas_call(
        _mod_body,
        grid=(DEPTH, n3 // tn),
        in_specs=[pl.BlockSpec((MOD_ROWS, D_MODEL), lambda l, j: (0, 0)),
                  pl.BlockSpec((None, D_MODEL, tn), lambda l, j: (l, 0, j)),
                  pl.BlockSpec((None, 1, tn), lambda l, j: (l, 0, j))],
        out_specs=pl.BlockSpec((None, MOD_ROWS, tn), lambda l, j: (l, 0, j)),
        out_shape=jax.ShapeDtypeStruct((DEPTH, MOD_ROWS, n3), _F32),
        compiler_params=_params("arbitrary", "arbitrary"),
        name="modulation",
    )(cvecs, w_mod, b_mod.reshape(DEPTH, 1, n3))


N_DT = 2 * SSM_HEADS
OFF_DT = 4 * NA_W + CONV_CH + SSM_W
REPACK_BLK = 1024


def _repack_body(a_ref, b_ref, o_ref, odt_ref):
    r = pl.program_id(1)
    cut = OFF_DT // REPACK_BLK

    @pl.when(r < cut)
    def _():
        o_ref[...] = a_ref[...].T.astype(_BF16)

    @pl.when(r >= cut)
    def _():
        o_ref[...] = jnp.concatenate([a_ref[N_DT:, :], b_ref[...]], axis=0).T.astype(_BF16)

    @pl.when(r == cut)
    def _():
        dt_rows = jnp.concatenate([a_ref[:N_DT, :], jnp.zeros((DT_PAD - N_DT, D_MODEL), _F32)], axis=0)
        odt_ref[...] = dt_rows.T.astype(_BF16)


def _repack_call(w_in_t):
    return pl.pallas_call(
        _repack_body,
        grid=(DEPTH, N_MAIN // REPACK_BLK),
        in_specs=[pl.BlockSpec((None, REPACK_BLK, D_MODEL), lambda l, r: (l, r, 0)),
                  pl.BlockSpec((None, N_DT, D_MODEL), lambda l, r: (l, (r + 1) * (REPACK_BLK // N_DT), 0))],
        out_specs=[pl.BlockSpec((None, D_MODEL, REPACK_BLK), lambda l, r: (l, 0, r)),
                   pl.BlockSpec((None, D_MODEL, DT_PAD), lambda l, r: (l, 0, 0))],
        out_shape=[jax.ShapeDtypeStruct((DEPTH, D_MODEL, N_MAIN), _BF16),
                   jax.ShapeDtypeStruct((DEPTH, D_MODEL, DT_PAD), _BF16)],
        compiler_params=_params("arbitrary", "arbitrary"),
        name="repack_w_in",
    )(w_in_t, w_in_t)


def _inproj_body(*refs, want_kv):
    x_ref, mod_ref, g_ref, w_ref, wdt_ref, o_ref, dt_ref = refs[:7]
    h_scr = refs[-1]
    j = pl.program_id(1)

    @pl.when(j == 0)
    def _():
        x = x_ref[...]
        xn = x * lax.rsqrt(jnp.mean(x * x, axis=-1, keepdims=True) + EPS) * g_ref[...]
        shift = mod_ref[:, 0:D_MODEL]
        scale = mod_ref[:, D_MODEL:2 * D_MODEL]
        h = (xn * (1.0 + scale) + shift).astype(_BF16)
        h_scr[...] = h
        dt_ref[...] = _dot(h, wdt_ref[...])

    res = _dot(h_scr[...], w_ref[...])
    o_ref[...] = res.astype(_BF16)
    if want_kv:
        kv_ref = refs[7]

        @pl.when((j == COL_K) | (j == COL_V))
        def _():
            kv_ref[...] = res


def _inproj_call(x2d, mod_l, g_pre_l, w_main, layer, w_dt, mod_row, want_kv):
    t = x2d.shape[0]
    tm = INPROJ_TM
    tn = SEG_W if want_kv else 2 * SEG_W
    out_specs = [pl.BlockSpec((tm, tn), lambda i, j: (i, j)),
                 pl.BlockSpec((tm, DT_PAD), lambda i, j: (i, 0))]
    out_shape = [jax.ShapeDtypeStruct((t, N_MAIN), _BF16),
                 jax.ShapeDtypeStruct((t, DT_PAD), _F32)]
    if want_kv:
        out_specs.append(pl.BlockSpec((tm, NA_W), lambda i, j: (i, jnp.clip(j - COL_K, 0, COL_V - COL_K))))
        out_shape.append(jax.ShapeDtypeStruct((t, 2 * NA_W), _F32))
    return pl.pallas_call(
        functools.partial(_inproj_body, want_kv=want_kv),
        grid=(t // tm, N_MAIN // tn),
        in_specs=[pl.BlockSpec((tm, D_MODEL), lambda i, j: (i, 0)),
                  pl.BlockSpec((None, 1, 3 * D_MODEL), lambda i, j: (mod_row(i, tm), 0, 0)),
                  pl.BlockSpec((1, D_MODEL), lambda i, j: (0, 0)),
                  pl.BlockSpec((None, D_MODEL, tn), lambda i, j: (layer, 0, j)),
                  pl.BlockSpec((None, D_MODEL, DT_PAD), lambda i, j: (layer, 0, 0))],
        out_specs=out_specs,
        out_shape=out_shape,
        scratch_shapes=[pltpu.VMEM((tm, D_MODEL), _BF16)],
        compiler_params=_params("arbitrary", "arbitrary"),
        name="inproj",
    )(x2d, mod_l, g_pre_l, w_main, w_dt)


def _stack_heads(q, head0):
    q = q * (NA_HEAD_DIM ** -0.5)
    zero = jnp.zeros_like(q)
    return jnp.concatenate([jnp.where(head0, q, zero), jnp.where(head0, zero, q)], axis=0)


def _ctx_attn_body(q_ref, k_ref, v_ref, ga_ref, kf_ref, vf_ref, kbuf_ref, vbuf_ref, o_ref, ko_ref, vo_ref):
    del kbuf_ref, vbuf_ref
    l = q_ref.shape[0]
    hp = 2 * NA_HEAD_DIM
    ko_ref[...] = kf_ref[...]
    vo_ref[...] = vf_ref[...]
    head0 = lax.broadcasted_iota(jnp.int32, (l, hp), 1) < NA_HEAD_DIM
    ones = jnp.ones((l, hp), _BF16)
    for j in range(NA_HEADS // 2):
        cols = slice(j * hp, (j + 1) * hp)
        q2 = _stack_heads(q_ref[:, cols], head0)
        s = lax.dot_general(q2, k_ref[:, cols], _NT, preferred_element_type=_F32)
        p = jnp.exp(s - jnp.max(s, axis=-1, keepdims=True)).astype(_BF16)
        v_ext = jnp.concatenate([v_ref[:, cols], ones], axis=1)
        oe = _dot(p, v_ext)
        on = oe[:, :hp] / oe[:, hp:]
        o = jnp.where(head0, on[:l], on[l:])
        o_ref[:, cols] = (o * _silu(ga_ref[:, cols].astype(_F32))).astype(_BF16)


def _ctx_attn_call(p3, kv3, layer, caches):
    b, l, _ = p3.shape

    def spec(col):
        return pl.BlockSpec((None, l, NA_W), lambda i: (i, 0, col))

    cache_spec = pl.BlockSpec((None, None, l, NA_W), lambda i: (i, layer, 0, 0))
    cache_shape = jax.ShapeDtypeStruct((b, DEPTH, l, NA_W), _F32)
    any_spec = pl.BlockSpec(memory_space=pl.ANY)
    o_a, k_all, v_all = pl.pallas_call(
        _ctx_attn_body,
        grid=(b,),
        in_specs=[spec(COL_Q), spec(COL_K), spec(COL_V), spec(COL_GA), spec(0), spec(1), any_spec, any_spec],
        out_specs=[pl.BlockSpec((None, l, NA_W), lambda i: (i, 0, 0)), cache_spec, cache_spec],
        out_shape=[jax.ShapeDtypeStruct((b, l, NA_W), _BF16), cache_shape, cache_shape],
        input_output_aliases={6: 1, 7: 2},
        compiler_params=_params("arbitrary"),
        name="ctx_attn",
    )(p3, p3, p3, p3, kv3, kv3, *caches)
    return o_a, (k_all, v_all)


def _na_bias_table(rpb_l):
    nk = 2 * NA_KW - 1
    j = jnp.arange(GRID_W)[None, :, None]
    c = jnp.arange(GRID_W)[None, None, :]
    k = jnp.arange(nk)[:, None, None]
    cs = jnp.clip(j - NA_KW // 2, 0, GRID_W - NA_KW)
    valid = (c >= cs) & (c < cs + NA_KW)
    sel = (valid & (c - j + (NA_KW - 1) == k)).astype(_F32)
    zeros = jnp.zeros_like(sel)
    sel2 = jnp.concatenate([jnp.concatenate([sel, zeros], axis=-1), jnp.concatenate([zeros, sel], axis=-1)], axis=0)
    r = rpb_l.astype(_F32)
    r2 = jnp.concatenate([r[:, :-1], r[:, 1:]], axis=-1)
    mask = jnp.where(valid, 0.0, NEG)
    sel2 = jnp.concatenate([sel2, jnp.concatenate([mask, mask], axis=-1)], axis=0)
    r2 = jnp.concatenate([r2, jnp.ones(r2.shape[:2] + (1,), _F32)], axis=-1)
    return jnp.einsum("hdm,mjn->hdjn", r2, sel2, precision=lax.Precision.HIGHEST)


def _na_body(q_ref, k_ref, v_ref, ga_ref, kc_ref, vc_ref, bias_ref, o_ref, vb_scr, kcb_scr, vcb_scr, *, rows):
    hp = 2 * NA_HEAD_DIM
    win = NA_KH * GRID_W
    vb_scr[:, :hp] = v_ref[...]
    vb_scr[:, hp:] = jnp.ones((vb_scr.shape[0], hp), _BF16)
    kcb_scr[...] = kc_ref[...].astype(_BF16)
    vcb_scr[:, :hp] = vc_ref[...].T.astype(_BF16)
    vcb_scr[:, hp:] = jnp.ones((vcb_scr.shape[0], hp), _BF16)
    head0 = lax.broadcasted_iota(jnp.int32, (GRID_W, hp), 1) < NA_HEAD_DIM

    def row(r, carry):
        rs = jnp.clip(r - NA_KH // 2, 0, rows - NA_KH)
        dr0 = rs - r + (NA_KH - 1)
        q0 = pl.multiple_of(r * GRID_W, GRID_W)
        k0 = pl.multiple_of(rs * GRID_W, GRID_W)
        q2 = _stack_heads(q_ref[pl.ds(q0, GRID_W), :], head0)
        bias = jnp.concatenate(
            [jnp.concatenate([bias_ref[hh, dr0 + a] for a in range(0, NA_KH, 2)], axis=1) for hh in range(2)],
            axis=0)
        s_loc = lax.dot_general(q2, k_ref[pl.ds(k0, win), :], _NT, preferred_element_type=_F32) + bias
        s_ctx = _dot(q2, kcb_scr[...])
        m = jnp.maximum(jnp.max(s_loc, axis=-1, keepdims=True), jnp.max(s_ctx, axis=-1, keepdims=True))
        p_loc = jnp.exp(s_loc - m).astype(_BF16)
        p_ctx = jnp.exp(s_ctx - m).astype(_BF16)
        oe = _dot(p_loc, vb_scr[pl.ds(k0, win), :]) + _dot(p_ctx, vcb_scr[...])
        on = oe[:, :hp] / oe[:, hp:]
        o = jnp.where(head0, on[:GRID_W], on[GRID_W:])
        o_ref[pl.ds(q0, GRID_W), :] = (o * _silu(ga_ref[pl.ds(q0, GRID_W), :].astype(_F32))).astype(_BF16)
        return carry

    lax.fori_loop(0, rows, row, 0, unroll=True)


def _na_call(p3, cache_k4, cache_v4, bias_tbl, layer):
    b, l, _ = p3.shape
    rows = l // GRID_W
    assert rows >= NA_KH
    hp = 2 * NA_HEAD_DIM
    nblk = SEG_W // hp
    lc = cache_k4.shape[3]

    def spec(col):
        return pl.BlockSpec((None, l, hp), lambda i, j: (i, 0, col * nblk + j))

    cspec = pl.BlockSpec((None, None, hp, lc), lambda i, j: (i, layer, j, 0))
    return pl.pallas_call(
        functools.partial(_na_body, rows=rows),
        grid=(b, NA_HEADS // 2),
        in_specs=[spec(COL_Q), spec(COL_K), spec(COL_V), spec(COL_GA), cspec, cspec,
                  pl.BlockSpec((2, 2 * NA_KH - 2, GRID_W, 2 * GRID_W), lambda i, j: (j, 0, 0, 0))],
        out_specs=pl.BlockSpec((None, l, hp), lambda i, j: (i, 0, j)),
        out_shape=jax.ShapeDtypeStruct((b, l, NA_W), _BF16),
        scratch_shapes=[pltpu.VMEM((l, 2 * hp), _BF16),
                        pltpu.VMEM((hp, lc), _BF16), pltpu.VMEM((lc, 2 * hp), _BF16)],
        compiler_params=_params("arbitrary", "arbitrary"),
        name="na_attn",
    )(p3, p3, p3, p3, cache_k4, cache_v4, bias_tbl)


HEADS_PER_GROUP = SSM_HEADS // SSM_GROUPS
GROUP_W = HEADS_PER_GROUP * SSM_HEADDIM
SSD_CHUNKS_PER_STEP = 4
SSD_HALO = 16


def _split3(x):
    hi = x.astype(_BF16)
    r1 = x - hi.astype(_F32)
    mid = r1.astype(_BF16)
    lo = (r1 - mid.astype(_F32)).astype(_BF16)
    return hi, mid, lo


def _silu_tanh(x):
    h = 0.5 * x
    return h + h * jnp.tanh(h)


def _ssd_chunk(xs_b, bm, cm, dtraw, dtb_row_ref, alog_row_ref, dtb_col_ref, alog_col_ref, ht_scr, reverse):
    q = SSM_CHUNK
    d = 1 if reverse else 0
    last = 0 if reverse else q - 1

    hs = slice(d * SSM_HEADS, (d + 1) * SSM_HEADS)
    dta_col = _softplus(dtraw + dtb_row_ref[...]) * (-jnp.exp(alog_row_ref[...]))
    dt_row = _softplus(dtraw.T[hs, :] + dtb_col_ref[hs, :])
    dta_row = dt_row * (-jnp.exp(alog_col_ref[hs, :]))

    ri = lax.broadcasted_iota(jnp.int32, (q, q), 0)
    ci = lax.broadcasted_iota(jnp.int32, (q, q), 1)
    causal = (ci >= ri) if reverse else (ci <= ri)
    t_col = jnp.where(causal, 1.0, 0.0).astype(_BF16)
    t_row = jnp.where((ri >= ci) if reverse else (ri <= ci), 1.0, 0.0).astype(_BF16)
    acum_col = sum(_dot(t_col, part) for part in _split3(dta_col)) * LOG2E
    acum_row = sum(_dot(part, t_row) for part in _split3(dta_row)) * LOG2E

    a_last = jnp.broadcast_to(acum_row[:, last:last + 1], (SSM_HEADS, q))
    dtde_row = dt_row * jnp.exp2(a_last - acum_row)
    cdecay = jnp.exp2(a_last)

    lo_half = lax.broadcasted_iota(jnp.int32, (1, q), 1) < SSM_HEADDIM
    lane_head = lax.broadcasted_iota(jnp.int32, (1, GROUP_W), 1) // SSM_HEADDIM
    zero_b = jnp.zeros((q, GROUP_W), _BF16)

    def per_head_lanes(vals):
        return jnp.concatenate([jnp.where(lo_half, vals[0], vals[1]), jnp.where(lo_half, vals[2], vals[3])], axis=1)

    ys = []
    for g in range(SSM_GROUPS):
        nsl = slice(g * SSM_STATE, (g + 1) * SSM_STATE)
        b_g = bm[:, nsl]
        c_g = cm[:, nsl]
        cb = lax.dot_general(c_g, b_g, _NT, preferred_element_type=_F32)
        b_t = b_g.astype(_F32).T
        xs_g = xs_b[:, g * GROUP_W:(g + 1) * GROUP_W]
        m_parts, bt_parts, xbd_parts, bcs = [], [], [], []
        for hh in range(HEADS_PER_GROUP):
            h = g * HEADS_PER_GROUP + hh
            lane = d * SSM_HEADS + h
            bc = jnp.broadcast_to(acum_col[:, lane:lane + 1], (q, q))
            lmat = jnp.exp2(jnp.where(causal, bc - acum_row[h:h + 1, :], NEG))
            m_parts.append((cb * lmat * dt_row[h:h + 1, :]).astype(_BF16))
            bt_parts.append((b_t * dtde_row[h:h + 1, :]).astype(_BF16))
            xbd_parts.append(jnp.where(lane_head == hh, xs_g, zero_b))
            bcs.append(bc)
        lhs = jnp.concatenate([jnp.concatenate(m_parts, axis=1), jnp.concatenate(bt_parts, axis=1)], axis=0)
        res = _dot(lhs, jnp.concatenate(xbd_parts, axis=0))
        h_t = ht_scr[g]
        y_off = _dot(c_g, h_t.astype(_BF16)) * jnp.exp2(per_head_lanes(bcs))
        ys.append(res[:q] + y_off)
        h0 = g * HEADS_PER_GROUP
        cd = per_head_lanes([cdecay[h0 + hh:h0 + hh + 1, :] for hh in range(HEADS_PER_GROUP)])
        ht_scr[g] = h_t * cd + res[q:]
    return ys


def _ssd_state_io(c, nc, h0_ref, stack_ref, hout_ref, ht_scr):
    @pl.when(c == 0)
    def _():
        for g in range(SSM_GROUPS):
            if h0_ref is None:
                ht_scr[g] = jnp.zeros((SSM_STATE, GROUP_W), _F32)
            else:
                hs = h0_ref[g * HEADS_PER_GROUP:(g + 1) * HEADS_PER_GROUP]
                ht_scr[g] = hs.reshape(GROUP_W, SSM_STATE).T

    def store_final():
        @pl.when(c == nc - 1)
        def _():
            slot = hout_ref.shape[0] - 1
            if stack_ref is not None:
                hout_ref[0:slot] = stack_ref[...]
            for g in range(SSM_GROUPS):
                hout_ref[slot, g * HEADS_PER_GROUP:(g + 1) * HEADS_PER_GROUP] = ht_scr[g].T.reshape(
                    HEADS_PER_GROUP, SSM_HEADDIM, SSM_STATE)

    return store_final


def _ssd_fwd_body(*refs, has_h0, has_stack, nc):
    it = iter(refs)
    x_ref, prev_ref, next_ref, dt_ref, cw_ref, cb_ref = (next(it) for _ in range(6))
    dec_refs = [next(it) for _ in range(4)]
    dskip_ref = next(it)
    h0_ref = next(it) if has_h0 else None
    stack_ref = next(it) if has_stack else None
    y_ref, act_ref, hout_ref, ht_scr = (next(it) for _ in range(4))
    q = SSM_CHUNK
    qb = x_ref.shape[0]
    c = pl.program_id(1)
    store_final = _ssd_state_io(c, nc, h0_ref, stack_ref, hout_ref, ht_scr)

    x = x_ref[...].astype(_F32)
    xp = jnp.where(c > 0, prev_ref[SSD_HALO - 1:SSD_HALO, :].astype(_F32), 0.0)
    xn = jnp.where(c < nc - 1, next_ref[0:1, :].astype(_F32), 0.0)
    sub = lax.broadcasted_iota(jnp.int32, (8, 1), 0)
    x_m1 = pltpu.roll(x, 1, axis=0)
    x_m1 = jnp.concatenate([jnp.where(sub == 0, xp, x_m1[:8]), x_m1[8:]], axis=0)
    x_p1 = pltpu.roll(x, qb - 1, axis=0)
    x_p1 = jnp.concatenate([x_p1[:qb - 8], jnp.where(sub == 7, xn, x_p1[qb - 8:])], axis=0)
    act = _silu_tanh(cw_ref[0:1, :] * x_m1 + cw_ref[1:2, :] * x + cw_ref[2:3, :] * x_p1 + cb_ref[...])
    act_b = act.astype(_BF16)
    act_ref[...] = act_b

    for s in range(qb // q):
        rows = slice(s * q, (s + 1) * q)
        ys = _ssd_chunk(act_b[rows, :SSM_W], act_b[rows, SSM_W:SSM_W + SSM_GROUPS * SSM_STATE],
                        act_b[rows, SSM_W + SSM_GROUPS * SSM_STATE:], dt_ref[rows, :], *dec_refs, ht_scr, False)
        y_ref[rows, :] = jnp.concatenate(ys, axis=1) + act[rows, :SSM_W] * dskip_ref[...]
    store_final()


def _ssd_bwd_body(*refs, has_h0, has_stack, nc):
    it = iter(refs)
    act_ref, dt_ref = next(it), next(it)
    dec_refs = [next(it) for _ in range(4)]
    yf_ref, z_ref, gssm_ref = next(it), next(it), next(it)
    h0_ref = next(it) if has_h0 else None
    stack_ref = next(it) if has_stack else None
    o_ref, hout_ref, ht_scr = next(it), next(it), next(it)
    c = pl.program_id(1)
    store_final = _ssd_state_io(c, nc, h0_ref, stack_ref, hout_ref, ht_scr)

    q = SSM_CHUNK
    for s in reversed(range(act_ref.shape[0] // q)):
        rows = slice(s * q, (s + 1) * q)
        ys = _ssd_chunk(act_ref[rows, :SSM_W], act_ref[rows, SSM_W:SSM_W + SSM_GROUPS * SSM_STATE],
                        act_ref[rows, SSM_W + SSM_GROUPS * SSM_STATE:], dt_ref[rows, :], *dec_refs, ht_scr, True)
        y = (yf_ref[rows, :] + jnp.concatenate(ys, axis=1)) * _silu_tanh(z_ref[rows, :].astype(_F32))
        y = y * lax.rsqrt(jnp.mean(y * y, axis=-1, keepdims=True) + EPS) * gssm_ref[...]
        o_ref[rows, :] = y.astype(_BF16)
    store_final()


def _ssd_calls(p3, dt3, prm, h0_f, h0_b, layer, stacks):
    b, l, _ = p3.shape
    q = min(SSD_CHUNKS_PER_STEP * SSM_CHUNK, l)
    nc = l // q
    has_h0 = h0_f is not None
    has_stack = stacks is not None
    slots = stacks[0].shape[0] + 1 if has_stack else 1
    xbc_blk = COL_XBC // 2

    def vec(w):
        return pl.BlockSpec((1, w), lambda i, c: (0, 0))

    dec_specs = [vec(DT_PAD), vec(DT_PAD),
                 pl.BlockSpec((DT_PAD, 1), lambda i, c: (0, 0)), pl.BlockSpec((DT_PAD, 1), lambda i, c: (0, 0))]
    dec_args = [prm["dtb_row"], prm["alog_row"], prm["dtb_col"], prm["alog_col"]]
    h0_spec = pl.BlockSpec((None, None, SSM_HEADS, SSM_HEADDIM, SSM_STATE), lambda i, c: (i, layer, 0, 0, 0))
    stack_spec = pl.BlockSpec((slots - 1, None, SSM_HEADS, SSM_HEADDIM, SSM_STATE), lambda i, c: (0, i, 0, 0, 0))
    state_spec = pl.BlockSpec((slots, None, SSM_HEADS, SSM_HEADDIM, SSM_STATE), lambda i, c: (0, i, 0, 0, 0))
    state_shape = jax.ShapeDtypeStruct((slots, b, SSM_HEADS, SSM_HEADDIM, SSM_STATE), _F32)
    ht_scratch = pltpu.VMEM((SSM_GROUPS, SSM_STATE, GROUP_W), _F32)

    in_specs = [
        pl.BlockSpec((None, q, CONV_CH), lambda i, c: (i, c, xbc_blk)),
        pl.BlockSpec((None, SSD_HALO, CONV_CH),
                     lambda i, c: (i, jnp.maximum(c * (q // SSD_HALO) - 1, 0), xbc_blk)),
        pl.BlockSpec((None, SSD_HALO, CONV_CH),
                     lambda i, c: (i, jnp.minimum((c + 1) * (q // SSD_HALO), l // SSD_HALO - 1), xbc_blk)),
        pl.BlockSpec((None, q, DT_PAD), lambda i, c: (i, c, 0)),
        pl.BlockSpec((3, CONV_CH), lambda i, c: (0, 0)), vec(CONV_CH)] + dec_specs + [vec(SSM_W)]
    args = [p3, p3, p3, dt3, prm["conv_wt"], prm["conv_b"]] + dec_args + [prm["dskip"]]
    if has_h0:
        in_specs.append(h0_spec)
        args.append(h0_f)
    if has_stack:
        in_specs.append(stack_spec)
        args.append(stacks[0])
    y_f, act, h_f = pl.pallas_call(
        functools.partial(_ssd_fwd_body, has_h0=has_h0, has_stack=has_stack, nc=nc),
        grid=(b, nc),
        in_specs=in_specs,
        out_specs=[pl.BlockSpec((None, q, SSM_W), lambda i, c: (i, c, 0)),
                   pl.BlockSpec((None, q, CONV_CH), lambda i, c: (i, c, 0)), state_spec],
        out_shape=[jax.ShapeDtypeStruct((b, l, SSM_W), _F32),
                   jax.ShapeDtypeStruct((b, l, CONV_CH), _BF16), state_shape],
        scratch_shapes=[ht_scratch],
        compiler_params=_params("arbitrary", "arbitrary"),
        name="ssd_fwd",
    )(*args)

    def rc(c):
        return nc - 1 - c

    in_specs = [pl.BlockSpec((None, q, CONV_CH), lambda i, c: (i, rc(c), 0)),
                pl.BlockSpec((None, q, DT_PAD), lambda i, c: (i, rc(c), 0))] + dec_specs + [
        pl.BlockSpec((None, q, SSM_W), lambda i, c: (i, rc(c), 0)),
        pl.BlockSpec((None, q, SSM_W), lambda i, c: (i, rc(c), COL_Z)), vec(SSM_W)]
    args = [act, dt3] + dec_args + [y_f, p3, prm["g_ssm"]]
    if has_h0:
        in_specs.append(h0_spec)
        args.append(h0_b)
    if has_stack:
        in_specs.append(stack_spec)
        args.append(stacks[1])
    o_b, h_b = pl.pallas_call(
        functools.partial(_ssd_bwd_body, has_h0=has_h0, has_stack=has_stack, nc=nc),
        grid=(b, nc),
        in_specs=in_specs,
        out_specs=[pl.BlockSpec((None, q, SSM_W), lambda i, c: (i, rc(c), 0)), state_spec],
        out_shape=[jax.ShapeDtypeStruct((b, l, SSM_W), _BF16), state_shape],
        scratch_shapes=[ht_scratch],
        compiler_params=_params("arbitrary", "arbitrary"),
        name="ssd_bwd",
    )(*args)
    return o_b, (h_f, h_b)


def _sgu_tile(u_ref, v_ref, g_ref, gs_ref, ws_ref, bs_ref, o_ref):
    v = v_ref[...].astype(_F32)
    mu = jnp.mean(v, axis=-1, keepdims=True)
    vc = v - mu
    var = jnp.mean(vc * vc, axis=-1, keepdims=True)
    vn = (vc * lax.rsqrt(var + EPS) * gs_ref[...]).astype(_BF16)
    ge = SGU_W // SGU_GROUPS
    for ch in range(v.shape[0] // SGU_CHUNK):
        rsl = slice(ch * SGU_CHUNK, (ch + 1) * SGU_CHUNK)
        for g in range(SGU_GROUPS):
            csl = slice(g * ge, (g + 1) * ge)
            vs = _dot(ws_ref[g], vn[rsl, csl]) + bs_ref[:, g:g + 1]
            y = u_ref[rsl, csl].astype(_F32) * vs * _silu(g_ref[rsl, csl].astype(_F32))
            o_ref[rsl, csl] = y.astype(_BF16)


def _merge_body(oa_ref, ob_ref, u_ref, vc_ref, gc_ref, ga_ref, gb_ref, gcm_ref, x_ref, mod_ref, gpost_ref,
                gs_ref, ws_ref, bs_ref, wa_ref, wb_ref, wc_ref, wo_ref, out_ref, oc_scr):
    _sgu_tile(u_ref, vc_ref, gc_ref, gs_ref, ws_ref, bs_ref, oc_scr)
    merged = _sigmoid(ga_ref[...].astype(_F32)) * _dot(oa_ref[...], wa_ref[...])
    merged = merged + _sigmoid(gb_ref[...].astype(_F32)) * _dot(ob_ref[...], wb_ref[...])
    merged = merged + _sigmoid(gcm_ref[...].astype(_F32)) * _dot(oc_scr[...], wc_ref[...])
    y = _dot(merged.astype(_BF16), wo_ref[...])
    y = y * lax.rsqrt(jnp.mean(y * y, axis=-1, keepdims=True) + EPS) * gpost_ref[...]
    out_ref[...] = x_ref[...] + mod_ref[:, 2 * D_MODEL:] * y


def _merge_call(o_a, o_b, p2, x2d, mod_l, prm, mod_row):
    t = x2d.shape[0]
    tm = MERGE_TM
    gm_blk = COL_GM // 2
    once = pl.Buffered(1)

    def row_spec(w, col=0):
        return pl.BlockSpec((tm, w), lambda i: (i, col))

    def const_spec(shape):
        return pl.BlockSpec(shape, lambda i: (0,) * len(shape))

    def w_spec(kdim):
        return pl.BlockSpec((kdim, D_MODEL), lambda i: (0, 0), pipeline_mode=once)

    return pl.pallas_call(
        _merge_body,
        grid=(t // tm,),
        in_specs=[row_spec(NA_W), row_spec(SSM_W),
                  row_spec(SGU_W, COL_U), row_spec(SGU_W, COL_VC), row_spec(SGU_W, COL_GC),
                  row_spec(D_MODEL, gm_blk), row_spec(D_MODEL, gm_blk + 1), row_spec(D_MODEL, gm_blk + 2),
                  row_spec(D_MODEL),
                  pl.BlockSpec((None, 1, 3 * D_MODEL), lambda i: (mod_row(i, tm), 0, 0)),
                  const_spec((1, D_MODEL)), const_spec((1, SGU_W)),
                  const_spec((SGU_GROUPS, SGU_CHUNK, SGU_CHUNK)), const_spec((SGU_CHUNK, SGU_GROUPS)),
                  w_spec(NA_W), w_spec(SSM_W), w_spec(SGU_W), w_spec(D_MODEL)],
        out_specs=row_spec(D_MODEL),
        out_shape=jax.ShapeDtypeStruct((t, D_MODEL), _F32),
        scratch_shapes=[pltpu.VMEM((tm, SGU_W), _BF16)],
        compiler_params=_params("arbitrary"),
        name="merge_out",
    )(o_a, o_b, p2, p2, p2, p2, p2, p2, x2d, mod_l, prm["g_post"], prm["g_sgu"], prm["w_s"], prm["b_s_t"],
      prm["w_br_a"], prm["w_br_b"], prm["w_br_c"], prm["w_out"])


def _layer(x3, mod_l, wts, mod_row, ctx, layer, caches=None, stacks=None):
    b, l, _ = x3.shape
    t = b * l
    x2d = x3.reshape(t, D_MODEL)
    p2, dt2, *kv = _inproj_call(x2d, mod_l, wts["g_pre"], wts["w_main"], layer, wts["w_dt"], mod_row,
                                want_kv=ctx is None)
    p3 = p2.reshape(b, l, N_MAIN)
    dt3 = dt2.reshape(b, l, DT_PAD)
    if ctx is None:
        o_a, caches = _ctx_attn_call(p3, kv[0].reshape(b, l, 2 * NA_W), layer, caches)
        h0_f = h0_b = None
    else:
        cache_k4, cache_v4, bias_tbl, h0_f, h0_b = ctx
        o_a = _na_call(p3, cache_k4, cache_v4, bias_tbl, layer)
    o_b, stacks = _ssd_calls(p3, dt3, wts, h0_f, h0_b, layer, stacks)
    y2d = _merge_call(o_a.reshape(t, NA_W), o_b.reshape(t, SSM_W), p2, x2d, mod_l, wts, mod_row)
    return y2d.reshape(b, l, D_MODEL), (caches, stacks)


def _layer_weights(l, g_pre, g_post, w_main, w_dt, conv_w, conv_b, dt_bias, a_log, d_skip, g_ssm, w_s, b_s, g_sgu,
                   w_br_a, w_br_b, w_br_c, w_out):
    dtb = jnp.pad(dt_bias[l].reshape(N_DT).astype(_F32), (0, DT_PAD - N_DT))
    alog = jnp.pad(a_log[l].reshape(N_DT).astype(_F32), (0, DT_PAD - N_DT))
    return {
        "g_pre": g_pre[l].reshape(1, D_MODEL), "g_post": g_post[l].reshape(1, D_MODEL),
        "w_main": w_main, "w_dt": w_dt,
        "conv_wt": conv_w[l].T, "conv_b": conv_b[l].reshape(1, CONV_CH),
        "dtb_row": dtb.reshape(1, DT_PAD), "alog_row": alog.reshape(1, DT_PAD),
        "dtb_col": dtb.reshape(DT_PAD, 1), "alog_col": alog.reshape(DT_PAD, 1),
        "dskip": jnp.repeat(d_skip[l].astype(_F32), SSM_HEADDIM).reshape(1, SSM_W),
        "g_ssm": g_ssm[l].reshape(1, SSM_W),
        "g_sgu": g_sgu[l].reshape(1, SGU_W), "w_s": w_s[l].astype(_BF16), "b_s_t": b_s[l].T,
        "w_br_a": w_br_a[l].astype(_BF16), "w_br_b": w_br_b[l].astype(_BF16),
        "w_br_c": w_br_c[l].astype(_BF16), "w_out": w_out[l].astype(_BF16),
    }


def kernel(x_prompt, x_sample, c, cache_k, cache_v, state_ssm_fwd, state_ssm_bwd, c_ctx, w_mod, b_mod, g_pre,
           g_post, w_in, rpb, conv_w, conv_b, dt_bias, a_log, d_skip, g_ssm, w_s, b_s, g_sgu, w_br_a, w_br_b,
           w_br_c, w_out):
    nb, ls, _ = x_sample.shape
    bp, lp, _ = x_prompt.shape
    assert 1 + nb <= MOD_ROWS
    assert ls % INPROJ_TM == 0 and (bp * lp) % INPROJ_TM == 0 and ls % MERGE_TM == 0
    assert lp % SSM_CHUNK == 0 and ls % (SSD_CHUNKS_PER_STEP * SSM_CHUNK) == 0 and MERGE_TM % SGU_CHUNK == 0
    assert ls % GRID_W == 0 and ls // GRID_W >= NA_KH
    cvecs = jnp.concatenate([c_ctx[None, :], c, jnp.zeros((MOD_ROWS - 1 - nb, D_MODEL), _F32)], axis=0)
    mod = _mod_call(cvecs, w_mod, b_mod).reshape(DEPTH, MOD_ROWS, 1, 3 * D_MODEL)
    past = cache_k.shape[2]
    cache_k4 = cache_k.transpose(0, 1, 3, 4, 2).reshape(nb, DEPTH, NA_W, past)
    cache_v4 = cache_v.transpose(0, 1, 3, 4, 2).reshape(nb, DEPTH, NA_W, past)

    def prompt_row(i, tm):
        return 0

    def sample_row(i, tm):
        return 1 + (i * tm) // ls

    y_p, y_s = x_prompt, x_sample
    caches = (jnp.zeros((bp, DEPTH, lp, NA_W), _F32), jnp.zeros((bp, DEPTH, lp, NA_W), _F32))
    stacks = None
    w_main, w_dt = _repack_call(jnp.swapaxes(w_in, 1, 2))
    for l in range(DEPTH):
        wts = _layer_weights(l, g_pre, g_post, w_main, w_dt, conv_w, conv_b, dt_bias, a_log, d_skip, g_ssm, w_s,
                             b_s, g_sgu, w_br_a, w_br_b, w_br_c, w_out)
        y_p, (caches, stacks) = _layer(y_p, mod[l], wts, prompt_row, None, l, caches, stacks)
        ctx = (cache_k4, cache_v4, _na_bias_table(rpb[l]), state_ssm_fwd, state_ssm_bwd)
        y_s, _ = _layer(y_s, mod[l], wts, sample_row, ctx, l)
    new_k = caches[0].reshape(bp, DEPTH, lp, NA_HEADS, NA_HEAD_DIM)
    new_v = caches[1].reshape(bp, DEPTH, lp, NA_HEADS, NA_HEAD_DIM)
    return (y_p, y_s, new_k, new_v, jnp.swapaxes(stacks[0], 0, 1), jnp.swapaxes(stacks[1], 0, 1))
```
